```python
import jax
import jax.numpy as jnp
from jax import lax
import numpy as np

D_MODEL = 1024
BATCH = 2
SEQ = 8192
DEPTH = 1
DEC_BATCH = 128
DEC_SEQ = 1
PAST_LEN = 16384
PAGE_SIZE = 128

CONV_DIM = D_MODEL // 2
CONV_WIDTH = 3
N_HEADS = 8
QK_NOPE_DIM = D_MODEL // 16
QK_ROPE_DIM = D_MODEL // 32
QK_HEAD_DIM = QK_NOPE_DIM + QK_ROPE_DIM
V_HEAD_DIM = D_MODEL // 16
Q_LORA_RANK = 3 * D_MODEL // 8
KV_LORA_RANK = D_MODEL // 4
ROPE_BASE = 10000.0
N_MEM = 256
CA_HEADS = 4
CA_HEAD_DIM = D_MODEL // CA_HEADS
D_FF = 4 * D_MODEL
Q_BLOCK = 128
RMS_EPS = 1e-6
NEG_INF = -1e30
IN_SIZES = (CONV_DIM, CONV_DIM, CONV_DIM, Q_LORA_RANK, KV_LORA_RANK, QK_ROPE_DIM, D_MODEL, D_MODEL)
IN_COLS = sum(IN_SIZES)

kernel_name = 'hybrid_conv_mla_mem_decoder_step'


def rms_norm(x, g):
    xf = x.astype(jnp.float32)
    y = xf * lax.rsqrt(jnp.mean(xf * xf, axis=-1, keepdims=True) + RMS_EPS)
    return (y * g.astype(jnp.float32)).astype(x.dtype)


def rope_tables(pos):
    inv = 1.0 / (ROPE_BASE ** (jnp.arange(0, QK_ROPE_DIM, 2, dtype=jnp.float32) / QK_ROPE_DIM))
    ang = pos.astype(jnp.float32)[:, None] * inv[None, :]
    return jnp.cos(ang), jnp.sin(ang)


def apply_rope(x, cos, sin):
    x1, x2 = jnp.split(x, 2, axis=-1)
    c = cos.astype(x.dtype)
    s = sin.astype(x.dtype)
    return jnp.concatenate([x1 * c - x2 * s, x1 * s + x2 * c], axis=-1)


def short_conv(u_ext, w):
    t = u_ext.shape[1] - (CONV_WIDTH - 1)
    y = u_ext[:, 0:t] * w[0]
    for k in range(1, CONV_WIDTH):
        y = y + u_ext[:, k:k + t] * w[k]
    return y


def mla_prompt_attention(q_nope, q_rope, ckv, kr, w_uk, w_uv):
    b, s_len = ckv.shape[0], ckv.shape[1]
    k_nope = jnp.einsum('bsl,lhd->bshd', ckv, w_uk)
    v = jnp.einsum('bsl,lhd->bshd', ckv, w_uv)
    scale = QK_HEAD_DIM ** -0.5
    key_pos = jnp.arange(s_len)

    def block(i):
        start = i * Q_BLOCK
        qn = lax.dynamic_slice_in_dim(q_nope, start, Q_BLOCK, axis=1)
        qr = lax.dynamic_slice_in_dim(q_rope, start, Q_BLOCK, axis=1)
        s = jnp.einsum('bqhd,bkhd->bhqk', qn, k_nope) + jnp.einsum('bqhr,bkr->bhqk', qr, kr)
        s = s.astype(jnp.float32) * scale
        q_pos = start + jnp.arange(Q_BLOCK)
        s = jnp.where(key_pos[None, :] <= q_pos[:, None], s, NEG_INF)
        p = jax.nn.softmax(s, axis=-1).astype(v.dtype)
        return jnp.einsum('bhqk,bkhd->bqhd', p, v)

    out = lax.map(block, jnp.arange(s_len // Q_BLOCK))
    return out.transpose(1, 0, 2, 3, 4).reshape(b, s_len, N_HEADS, V_HEAD_DIM)


def mla_sample_attention(q_nope, q_rope, ckv_new, kr_new, ckv_past, kr_past, w_uk, w_uv):
    t = ckv_new.shape[1]
    p_len = ckv_past.shape[1]
    scale = QK_HEAD_DIM ** -0.5
    q_lat = jnp.einsum('bthd,lhd->bthl', q_nope, w_uk)
    s_past = jnp.einsum('bthl,bpl->bhtp', q_lat, ckv_past) + jnp.einsum('bthr,bpr->bhtp', q_rope, kr_past)
    s_new = jnp.einsum('bthl,bul->bhtu', q_lat, ckv_new) + jnp.einsum('bthr,bur->bhtu', q_rope, kr_new)
    s_past = s_past.astype(jnp.float32) * scale
    causal = jnp.arange(t)[:, None] >= jnp.arange(t)[None, :]
    s_new = jnp.where(causal, s_new.astype(jnp.float32) * scale, NEG_INF)
    p = jax.nn.softmax(jnp.concatenate([s_past, s_new], axis=-1), axis=-1).astype(ckv_new.dtype)
    o_lat = jnp.einsum('bhtp,bpl->bthl', p[..., :p_len], ckv_past) + jnp.einsum('bhtu,bul->bthl', p[..., p_len:], ckv_new)
    return jnp.einsum('bthl,lhd->bthd', o_lat, w_uv)


def memory_kv(mem, mem_norm_g, w_ca_k, w_ca_v):
    mn = rms_norm(mem, mem_norm_g)
    k = jnp.einsum('bmd,dhe->bmhe', mn, w_ca_k)
    v = jnp.einsum('bmd,dhe->bmhe', mn, w_ca_v)
    return k, v


def cross_attend(xn, mem_k, mem_v, w_ca_q, w_ca_o):
    q = jnp.einsum('btd,dhe->bthe', xn, w_ca_q)
    s = jnp.einsum('bthe,bmhe->bhtm', q, mem_k).astype(jnp.float32) * (CA_HEAD_DIM ** -0.5)
    p = jax.nn.softmax(s, axis=-1).astype(mem_v.dtype)
    o = jnp.einsum('bhtm,bmhe->bthe', p, mem_v)
    return jnp.einsum('bthe,hed->btd', o, w_ca_o)


def mixer_block(xn, u_prev, past, cos, sin, w_in, conv_w, w_conv_out, q_norm_g, w_uq,
                kv_norm_g, w_uk, w_uv, w_mla_out, w_mix_out):
    b, t = xn.shape[0], xn.shape[1]
    split_pts = np.cumsum(np.array(IN_SIZES))[:-1].tolist()
    h, gate_b, gate_c, cq, ckv, kr, g_conv, g_mla = jnp.split(xn @ w_in, split_pts, axis=-1)
    u = gate_c * h
    u_ext = jnp.concatenate([u_prev, u], axis=1)
    y_conv = (gate_b * short_conv(u_ext, conv_w)) @ w_conv_out
    cq = rms_norm(cq, q_norm_g)
    q = jnp.einsum('btr,rhd->bthd', cq, w_uq)
    q_nope = q[..., :QK_NOPE_DIM]
    q_rope = apply_rope(q[..., QK_NOPE_DIM:], cos[:, None, :], sin[:, None, :])
    ckv = rms_norm(ckv, kv_norm_g)
    kr = apply_rope(kr, cos, sin)
    if past is None:
        o = mla_prompt_attention(q_nope, q_rope, ckv, kr, w_uk, w_uv)
    else:
        o = mla_sample_attention(q_nope, q_rope, ckv, kr, past[0], past[1], w_uk, w_uv)
    y_mla = o.reshape(b, t, N_HEADS * V_HEAD_DIM) @ w_mla_out
    merged = jax.nn.sigmoid(g_conv) * y_conv + jax.nn.sigmoid(g_mla) * y_mla
    return merged @ w_mix_out, ckv, kr, u_ext[:, -(CONV_WIDTH - 1):]


def decoder_layer(x, u_prev, past, mem_k, mem_v, cos, sin,
                  norm_mix_pre_g, w_in, conv_w, w_conv_out, q_norm_g, w_uq, kv_norm_g, w_uk, w_uv,
                  w_mla_out, w_mix_out, norm_mix_post_g, norm_ca_pre_g, w_ca_q, w_ca_o, norm_ca_post_g,
                  norm_mlp_pre_g, w_ff_up, w_ff_down, norm_mlp_post_g):
    y, ckv, kr, u_last = mixer_block(rms_norm(x, norm_mix_pre_g), u_prev, past, cos, sin, w_in, conv_w,
                                     w_conv_out, q_norm_g, w_uq, kv_norm_g, w_uk, w_uv, w_mla_out, w_mix_out)
    x = x + rms_norm(y, norm_mix_post_g)
    x = x + rms_norm(cross_attend(rms_norm(x, norm_ca_pre_g), mem_k, mem_v, w_ca_q, w_ca_o), norm_ca_post_g)
    hid = jnp.square(jax.nn.relu(rms_norm(x, norm_mlp_pre_g) @ w_ff_up))
    x = x + rms_norm(hid @ w_ff_down, norm_mlp_post_g)
    return x, ckv, kr, u_last


def setup_inputs(seed: int = 0) -> dict:
    key = jax.random.key(seed)
    ks = iter(jax.random.split(key, 48))
    f32 = jnp.float32

    def nrm(shape, scale=1.0):
        return scale * jax.random.normal(next(ks), shape, f32)

    def gain(n):
        return 1.0 + 0.02 * nrm((DEPTH, n))

    n_pages = PAST_LEN // PAGE_SIZE
    in_use = DEC_BATCH * n_pages
    n_pool = in_use + max(1, in_use // 4)
    page_table = jax.random.permutation(next(ks), n_pool)[:in_use].reshape(DEC_BATCH, n_pages).astype(jnp.int32)
    return {
        'x_prompt': nrm((BATCH, SEQ, D_MODEL)),
        'x_sample': nrm((DEC_BATCH, DEC_SEQ, D_MODEL)),
        'mem_prompt': nrm((BATCH, N_MEM, D_MODEL)),
        'cache_ckv': nrm((DEPTH, n_pool, PAGE_SIZE, KV_LORA_RANK)),
        'cache_krope': nrm((DEPTH, n_pool, PAGE_SIZE, QK_ROPE_DIM)),
        'state_conv': nrm((DEPTH, DEC_BATCH, CONV_WIDTH - 1, CONV_DIM)),
        'cache_mem_k': nrm((DEPTH, DEC_BATCH, N_MEM, CA_HEADS, CA_HEAD_DIM)),
        'cache_mem_v': nrm((DEPTH, DEC_BATCH, N_MEM, CA_HEADS, CA_HEAD_DIM)),
        'page_table': page_table,
        'norm_mix_pre_g': gain(D_MODEL),
        'w_in': nrm((DEPTH, D_MODEL, IN_COLS), D_MODEL ** -0.5),
        'conv_w': nrm((DEPTH, CONV_WIDTH, CONV_DIM), CONV_WIDTH ** -0.5),
        'w_conv_out': nrm((DEPTH, CONV_DIM, D_MODEL), CONV_DIM ** -0.5),
        'q_norm_g': gain(Q_LORA_RANK),
        'w_uq': nrm((DEPTH, Q_LORA_RANK, N_HEADS, QK_HEAD_DIM), Q_LORA_RANK ** -0.5),
        'kv_norm_g': gain(KV_LORA_RANK),
        'w_uk': nrm((DEPTH, KV_LORA_RANK, N_HEADS, QK_NOPE_DIM), KV_LORA_RANK ** -0.5),
        'w_uv': nrm((DEPTH, KV_LORA_RANK, N_HEADS, V_HEAD_DIM), KV_LORA_RANK ** -0.5),
        'w_mla_out': nrm((DEPTH, N_HEADS * V_HEAD_DIM, D_MODEL), (N_HEADS * V_HEAD_DIM) ** -0.5),
        'w_mix_out': nrm((DEPTH, D_MODEL, D_MODEL), D_MODEL ** -0.5),
        'norm_mix_post_g': gain(D_MODEL),
        'norm_ca_pre_g': gain(D_MODEL),
        'mem_norm_g': gain(D_MODEL),
        'w_ca_q': nrm((DEPTH, D_MODEL, CA_HEADS, CA_HEAD_DIM), D_MODEL ** -0.5),
        'w_ca_k': nrm((DEPTH, D_MODEL, CA_HEADS, CA_HEAD_DIM), D_MODEL ** -0.5),
        'w_ca_v': nrm((DEPTH, D_MODEL, CA_HEADS, CA_HEAD_DIM), D_MODEL ** -0.5),
        'w_ca_o': nrm((DEPTH, CA_HEADS, CA_HEAD_DIM, D_MODEL), (CA_HEADS * CA_HEAD_DIM) ** -0.5),
        'norm_ca_post_g': gain(D_MODEL),
        'norm_mlp_pre_g': gain(D_MODEL),
        'w_ff_up': nrm((DEPTH, D_MODEL, D_FF), D_MODEL ** -0.5),
        'w_ff_down': nrm((DEPTH, D_FF, D_MODEL), D_FF ** -0.5),
        'norm_mlp_post_g': gain(D_MODEL),
    }


def reference(x_prompt, x_sample, mem_prompt, cache_ckv, cache_krope, state_conv, cache_mem_k, cache_mem_v,
              page_table, norm_mix_pre_g, w_in, conv_w, w_conv_out, q_norm_g, w_uq, kv_norm_g, w_uk, w_uv,
              w_mla_out, w_mix_out, norm_mix_post_g, norm_ca_pre_g, mem_norm_g, w_ca_q, w_ca_k, w_ca_v, w_ca_o,
              norm_ca_post_g, norm_mlp_pre_g, w_ff_up, w_ff_down, norm_mlp_post_g):
    seq = x_prompt.shape[1]
    dec_b, t_s = x_sample.shape[0], x_sample.shape[1]
    past_len = page_table.shape[1] * cache_ckv.shape[2]
    cos_p, sin_p = rope_tables(jnp.arange(seq))
    cos_s, sin_s = rope_tables(past_len + jnp.arange(t_s))
    xp, xs = x_prompt, x_sample
    ckv_p_l, kr_p_l, conv_p_l, mk_p_l, mv_p_l = [], [], [], [], []
    ckv_s_l, kr_s_l, conv_s_l = [], [], []
    for l in range(DEPTH):
        lw = (norm_mix_pre_g[l], w_in[l], conv_w[l], w_conv_out[l], q_norm_g[l], w_uq[l], kv_norm_g[l],
              w_uk[l], w_uv[l], w_mla_out[l], w_mix_out[l], norm_mix_post_g[l], norm_ca_pre_g[l], w_ca_q[l],
              w_ca_o[l], norm_ca_post_g[l], norm_mlp_pre_g[l], w_ff_up[l], w_ff_down[l], norm_mlp_post_g[l])
        mk_p, mv_p = memory_kv(mem_prompt, mem_norm_g[l], w_ca_k[l], w_ca_v[l])
        u0 = jnp.zeros((xp.shape[0], CONV_WIDTH - 1, CONV_DIM), xp.dtype)
        xp, ckv_p, kr_p, u_p = decoder_layer(xp, u0, None, mk_p, mv_p, cos_p, sin_p, *lw)
        ckv_past = cache_ckv[l][page_table].reshape(dec_b, past_len, KV_LORA_RANK)
        kr_past = cache_krope[l][page_table].reshape(dec_b, past_len, QK_ROPE_DIM)
        xs, ckv_s, kr_s, u_s = decoder_layer(xs, state_conv[l], (ckv_past, kr_past), cache_mem_k[l],
                                             cache_mem_v[l], cos_s, sin_s, *lw)
        ckv_p_l.append(ckv_p)
        kr_p_l.append(kr_p)
        conv_p_l.append(u_p)
        mk_p_l.append(mk_p)
        mv_p_l.append(mv_p)
        ckv_s_l.append(ckv_s)
        kr_s_l.append(kr_s)
        conv_s_l.append(u_s)
    return (xp, xs, jnp.stack(ckv_p_l), jnp.stack(kr_p_l), jnp.stack(conv_p_l), jnp.stack(mk_p_l),
            jnp.stack(mv_p_l), jnp.stack(ckv_s_l), jnp.stack(kr_s_l), jnp.stack(conv_s_l))
```

```python
import functools

import numpy as np
import jax
import jax.numpy as jnp
from jax import lax
from jax.experimental import pallas as pl
from jax.experimental.pallas import tpu as pltpu

F32 = jnp.float32
BF16 = jnp.bfloat16

RMS_EPS = 1e-6
NEG_INF = -1e30
ROPE_BASE = 10000.0
CONV_WIDTH = 3

LANES = 128
SUBLANES = 8
VMEM_LIMIT_BYTES = 56 * 1024 * 1024

ROW_TILE = 512
ATTN_TQ = 512
ATTN_TK = 512
DEC_PAGES_PER_CHUNK = 8
DEC_NBUF = 3
CA_SAMPLES_PER_STEP = 4


def _rms(x, g):
    return x * lax.rsqrt(jnp.mean(x * x, axis=-1, keepdims=True) + RMS_EPS) * g


def _bdot(a, b):
    return jnp.dot(a.astype(BF16), b.astype(BF16), preferred_element_type=F32)


def _bdot_nt(a, b):
    return lax.dot_general(a.astype(BF16), b.astype(BF16), (((1,), (1,)), ((), ())),
                           preferred_element_type=F32)


def _softmax_rows(s):
    m = jnp.max(s, axis=-1, keepdims=True)
    p = jnp.exp(s - m)
    return p / jnp.sum(p, axis=-1, keepdims=True)


def _const_spec(shape):
    nd = len(shape)
    return pl.BlockSpec(shape, lambda *_: (0,) * nd, pipeline_mode=pl.Buffered(1))


def _params(*sem):
    return pltpu.CompilerParams(dimension_semantics=tuple(sem) if sem else None,
                                vmem_limit_bytes=VMEM_LIMIT_BYTES)


def _memkv_kernel(mem_ref, g_ref, wk_ref, wv_ref, k_ref, v_ref):
    mn = _rms(mem_ref[...], g_ref[...]).astype(BF16)
    k_ref[...] = jnp.dot(mn, wk_ref[...], preferred_element_type=F32)
    v_ref[...] = jnp.dot(mn, wv_ref[...], preferred_element_type=F32)


def _memory_kv(mem2d, g, wk, wv):
    rows, d = mem2d.shape
    out = jax.ShapeDtypeStruct((rows, wk.shape[1]), F32)
    return pl.pallas_call(_memkv_kernel, out_shape=(out, out), name="mem_kv",
                          compiler_params=_params())(mem2d, g, wk, wv)


class _PreDims:
    def __init__(self, d_model, conv_dim, q_rank, kv_rank, n_heads):
        self.d_model, self.conv_dim, self.q_rank, self.kv_rank = d_model, conv_dim, q_rank, kv_rank
        self.n_heads = n_heads
        c = conv_dim
        self.o_h, self.o_gb, self.o_gc = 0, c, 2 * c
        self.o_cq = 3 * c
        self.o_ckv = self.o_cq + q_rank
        self.o_kra = self.o_ckv + kv_rank
        self.o_krb = self.o_kra + LANES
        self.o_gconv = self.o_krb + LANES
        self.o_gmla = self.o_gconv + d_model
        self.n_in = self.o_gmla + d_model
        self.hw = n_heads * LANES


def _pre_common(dims, x, ctab, stab, gpre_ref, win_ref, qg_ref, wuqa_ref, wuqb_ref, kvg_ref, scale):
    d = dims
    xn = _rms(x, gpre_ref[...]).astype(BF16)

    def proj(lo, hi):
        return jnp.dot(xn, win_ref[:, lo:hi], preferred_element_type=F32)

    h = proj(d.o_h, d.o_gb)
    gate_b = proj(d.o_gb, d.o_gc)
    gate_c = proj(d.o_gc, d.o_cq)
    cq = proj(d.o_cq, d.o_ckv)
    ckv = proj(d.o_ckv, d.o_kra)
    kra = proj(d.o_kra, d.o_krb)
    krb = proj(d.o_krb, d.o_gconv)
    g_conv = proj(d.o_gconv, d.o_gmla)
    g_mla = proj(d.o_gmla, d.n_in)

    u = gate_c * h
    cqn = _rms(cq, qg_ref[...]).astype(BF16)
    qa = jnp.dot(cqn, wuqa_ref[...], preferred_element_type=F32)
    qb = jnp.dot(cqn, wuqb_ref[...], preferred_element_type=F32)
    ct = jnp.concatenate([ctab] * d.n_heads, axis=1)
    st = jnp.concatenate([stab] * d.n_heads, axis=1)
    q = (qa * ct + qb * st) * scale
    ckvn = _rms(ckv, kvg_ref[...])
    krr = kra * ctab + krb * stab
    return u, gate_b, g_conv, g_mla, q, ckvn, krr


def _pre_prompt_kernel(dims, scale, tm,
                       x_ref, ctab_ref, stab_ref, gpre_ref, win_ref, convw_ref, wco_ref, qg_ref,
                       wuqa_ref, wuqb_ref, kvg_ref, wuk_ref, wuv_ref,
                       q_out, k_out, v_out, ckv_out, kr_out, mc_out, sg_out, conv_out, ubuf):
    hist = SUBLANES
    u, gate_b, g_conv, g_mla, q, ckvn, krr = _pre_common(
        dims, x_ref[...], ctab_ref[...], stab_ref[...], gpre_ref, win_ref, qg_ref, wuqa_ref,
        wuqb_ref, kvg_ref, scale)

    @pl.when(pl.program_id(1) == 0)
    def _():
        ubuf[0:hist, :] = jnp.zeros((hist, dims.conv_dim), F32)

    ubuf[hist:hist + tm, :] = u
    u1 = ubuf[hist - 1:hist - 1 + tm, :]
    u2 = ubuf[hist - 2:hist - 2 + tm, :]
    cw = convw_ref[...]
    conv = u2 * cw[0:1, :] + u1 * cw[1:2, :] + u * cw[2:3, :]
    y_conv = _bdot(gate_b * conv, wco_ref[...])
    mc_out[...] = (jax.nn.sigmoid(g_conv) * y_conv).astype(mc_out.dtype)
    sg_out[...] = jax.nn.sigmoid(g_mla).astype(sg_out.dtype)
    conv_out[...] = ubuf[hist + tm - (CONV_WIDTH - 1):hist + tm, :]
    ubuf[0:hist, :] = ubuf[tm:tm + hist, :]

    q_out[...] = q.astype(q_out.dtype)
    ckv_out[...] = ckvn
    kr_out[...] = krr[:, :kr_out.shape[-1]]
    ckvb = ckvn.astype(BF16)
    ka = jnp.dot(ckvb, wuk_ref[...], preferred_element_type=F32)
    k = ka + jnp.concatenate([krr] * dims.n_heads, axis=1)
    k_out[...] = k.astype(k_out.dtype)
    v_out[...] = jnp.dot(ckvb, wuv_ref[...], preferred_element_type=F32).astype(v_out.dtype)


def _pre_prompt(dims, scale, x2d, ctab, stab, w, batch, seq, rope_dim):
    tm = ROW_TILE
    assert seq % tm == 0
    nt = seq // tm
    rows = batch * seq
    d = dims

    def row_spec(width):
        return pl.BlockSpec((tm, width), lambda b, t: (b * nt + t, 0))

    def tab_spec():
        return pl.BlockSpec((tm, LANES), lambda b, t: (t, 0))

    in_specs = [row_spec(d.d_model), tab_spec(), tab_spec(),
                _const_spec((1, d.d_model)), _const_spec((d.d_model, d.n_in)),
                _const_spec((CONV_WIDTH, d.conv_dim)), _const_spec((d.conv_dim, d.d_model)),
                _const_spec((1, d.q_rank)), _const_spec((d.q_rank, d.hw)), _const_spec((d.q_rank, d.hw)),
                _const_spec((1, d.kv_rank)), _const_spec((d.kv_rank, d.hw)), _const_spec((d.kv_rank, d.hw))]
    out_shape = (jax.ShapeDtypeStruct((rows, d.hw), BF16),
                 jax.ShapeDtypeStruct((rows, d.hw), BF16),
                 jax.ShapeDtypeStruct((rows, d.hw), BF16),
                 jax.ShapeDtypeStruct((rows, d.kv_rank), F32),
                 jax.ShapeDtypeStruct((rows, rope_dim), F32),
                 jax.ShapeDtypeStruct((rows, d.d_model), BF16),
                 jax.ShapeDtypeStruct((rows, d.d_model), BF16),
                 jax.ShapeDtypeStruct((batch, CONV_WIDTH - 1, d.conv_dim), F32))
    out_specs = (row_spec(d.hw), row_spec(d.hw), row_spec(d.hw), row_spec(d.kv_rank),
                 row_spec(rope_dim), row_spec(d.d_model), row_spec(d.d_model),
                 pl.BlockSpec((None, CONV_WIDTH - 1, d.conv_dim), lambda b, t: (b, 0, 0)))
    kern = functools.partial(_pre_prompt_kernel, dims, scale, tm)
    return pl.pallas_call(
        kern, grid=(batch, nt), in_specs=in_specs, out_specs=out_specs, out_shape=out_shape,
        scratch_shapes=[pltpu.VMEM((tm + SUBLANES, d.conv_dim), F32)],
        compiler_params=_params("arbitrary", "arbitrary"), name="pre_prompt",
    )(x2d, ctab, stab, w["g_pre"], w["w_in"], w["conv_w"], w["w_conv_out"], w["q_g"],
      w["w_uq_a"], w["w_uq_b"], w["kv_g"], w["w_uk_pad"], w["w_uv_pad"])


def _pre_sample_kernel(dims, scale,
                       x_ref, state_ref, ctab_ref, stab_ref, gpre_ref, win_ref, convw_ref, wco_ref,
                       qg_ref, wuqa_ref, wuqb_ref, kvg_ref, wukt_ref,
                       q_out, qlat_out, ckv_out, kr_out, mc_out, sg_out, conv_out):
    c = dims.conv_dim
    u, gate_b, g_conv, g_mla, q, ckvn, krr = _pre_common(
        dims, x_ref[...], ctab_ref[...], stab_ref[...], gpre_ref, win_ref, qg_ref, wuqa_ref,
        wuqb_ref, kvg_ref, scale)
    u2 = state_ref[:, 0:c]
    u1 = state_ref[:, c:2 * c]
    cw = convw_ref[...]
    conv = u2 * cw[0:1, :] + u1 * cw[1:2, :] + u * cw[2:3, :]
    y_conv = _bdot(gate_b * conv, wco_ref[...])
    mc_out[...] = jax.nn.sigmoid(g_conv) * y_conv
    sg_out[...] = jax.nn.sigmoid(g_mla)
    conv_out[:, 0:c] = u1
    conv_out[:, c:2 * c] = u
    q_out[...] = q
    ckv_out[...] = ckvn
    kr_out[...] = krr
    kvr = dims.kv_rank
    qb16 = q.astype(BF16)
    for hd in range(dims.n_heads):
        blk = qb16[:, hd * LANES:(hd + 1) * LANES]
        qlat_out[:, hd * kvr:(hd + 1) * kvr] = jnp.dot(
            blk, wukt_ref[hd * LANES:(hd + 1) * LANES, :], preferred_element_type=F32)


def _pre_sample(dims, scale, xs, state2d, ctab, stab, w):
    n = xs.shape[0]
    d = dims
    out_shape = (jax.ShapeDtypeStruct((n, d.hw), F32),
                 jax.ShapeDtypeStruct((n, d.n_heads * d.kv_rank), F32),
                 jax.ShapeDtypeStruct((n, d.kv_rank), F32),
                 jax.ShapeDtypeStruct((n, LANES), F32),
                 jax.ShapeDtypeStruct((n, d.d_model), F32),
                 jax.ShapeDtypeStruct((n, d.d_model), F32),
                 jax.ShapeDtypeStruct((n, (CONV_WIDTH - 1) * d.conv_dim), F32))
    kern = functools.partial(_pre_sample_kernel, dims, scale)
    return pl.pallas_call(kern, out_shape=out_shape, compiler_params=_params(), name="pre_sample")(
        xs, state2d, ctab, stab, w["g_pre"], w["w_in"], w["conv_w"], w["w_conv_out"], w["q_g"],
        w["w_uq_a"], w["w_uq_b"], w["kv_g"], w["w_uk_t_pad"])


def _attn_kernel(tq, tk, q_ref, k_ref, v_ref, o_ref, m_sc, l_sc, acc_sc):
    qi = pl.program_id(2)
    q = q_ref[...]
    m_sc[...] = jnp.full(m_sc.shape, NEG_INF, F32)
    l_sc[...] = jnp.zeros(l_sc.shape, F32)
    acc_sc[...] = jnp.zeros(acc_sc.shape, F32)

    def step(j, masked):
        start = pl.multiple_of(j * tk, tk)
        k = k_ref[pl.ds(start, tk), :]
        v = v_ref[pl.ds(start, tk), :]
        s = lax.dot_general(q, k, (((1,), (1,)), ((), ())), preferred_element_type=F32)
        if masked:
            row = lax.broadcasted_iota(jnp.int32, (tq, tk), 0) + qi * tq
            col = lax.broadcasted_iota(jnp.int32, (tq, tk), 1) + j * tk
            s = jnp.where(col <= row, s, NEG_INF)
        m_prev = m_sc[...]
        m_new = jnp.maximum(m_prev, jnp.max(s, axis=-1, keepdims=True))
        alpha = jnp.exp(m_prev - m_new)
        p = jnp.exp(s - m_new)
        l_sc[...] = alpha * l_sc[...] + jnp.sum(p, axis=-1, keepdims=True)
        acc_sc[...] = alpha * acc_sc[...] + jnp.dot(p.astype(BF16), v, preferred_element_type=F32)
        m_sc[...] = m_new

    n_diag = tq // tk
    n_full = qi * n_diag

    def body(j, carry):
        step(j, False)
        return carry

    lax.fori_loop(0, n_full, body, 0)
    for dj in range(n_diag):
        step(n_full + dj, True)
    o_ref[...] = (acc_sc[...] / l_sc[...]).astype(o_ref.dtype)


def _prompt_attention(q, k, v, batch, seq, n_heads):
    tq, tk = ATTN_TQ, ATTN_TK
    assert seq % tq == 0 and tq % tk == 0
    nq = seq // tq
    kern = functools.partial(_attn_kernel, tq, tk)
    return pl.pallas_call(
        kern, grid=(batch, n_heads, nq),
        in_specs=[pl.BlockSpec((tq, LANES), lambda b, h, i: (b * nq + i, h)),
                  pl.BlockSpec((seq, LANES), lambda b, h, i: (b, h)),
                  pl.BlockSpec((seq, LANES), lambda b, h, i: (b, h))],
        out_specs=pl.BlockSpec((tq, LANES), lambda b, h, i: (b * nq + i, h)),
        out_shape=jax.ShapeDtypeStruct(q.shape, BF16),
        scratch_shapes=[pltpu.VMEM((tq, 1), F32), pltpu.VMEM((tq, 1), F32), pltpu.VMEM((tq, LANES), F32)],
        compiler_params=_params("arbitrary", "arbitrary", "arbitrary"), name="prompt_attn",
    )(q, k, v)


def _mix_and_query(x, o, mc, sg, wmla_ref, wmix_ref, gmixpost_ref, gcapre_ref, wcaq_ref, ca_scale):
    y_mla = _bdot(o, wmla_ref[...])
    merged = mc.astype(F32) + sg.astype(F32) * y_mla
    y = _bdot(merged, wmix_ref[...])
    x1 = x + _rms(y, gmixpost_ref[...])
    qc = _bdot(_rms(x1, gcapre_ref[...]), wcaq_ref[...]) * ca_scale
    return x1, qc


def _mlp_tail(x1, oc, wcao_ref, gcapost_ref, gmlppre_ref, wup_ref, wdown_ref, gmlppost_ref):
    ca = _bdot(oc, wcao_ref[...])
    x2 = x1 + _rms(ca, gcapost_ref[...])
    hid = jnp.square(jnp.maximum(_bdot(_rms(x2, gmlppre_ref[...]), wup_ref[...]), 0.0))
    return x2 + _rms(_bdot(hid, wdown_ref[...]), gmlppost_ref[...])


def _post_prompt_kernel(ca_heads, ca_scale,
                        x_ref, o_ref, mc_ref, sg_ref, mk_ref, mv_ref,
                        wmla_ref, wmix_ref, gmixpost_ref, gcapre_ref, wcaq_ref, wcao_ref, gcapost_ref,
                        gmlppre_ref, wup_ref, wdown_ref, gmlppost_ref, y_ref):
    x1, qc = _mix_and_query(x_ref[...], o_ref[...], mc_ref[...], sg_ref[...], wmla_ref, wmix_ref,
                            gmixpost_ref, gcapre_ref, wcaq_ref, ca_scale)
    hd = qc.shape[1] // ca_heads
    outs = []
    for h in range(ca_heads):
        sl = slice(h * hd, (h + 1) * hd)
        p = _softmax_rows(_bdot_nt(qc[:, sl], mk_ref[:, sl]))
        outs.append(_bdot(p, mv_ref[:, sl]))
    oc = jnp.concatenate(outs, axis=1)
    y_ref[...] = _mlp_tail(x1, oc, wcao_ref, gcapost_ref, gmlppre_ref, wup_ref, wdown_ref, gmlppost_ref)


def _post_prompt(x2d, o, mc, sg, mk, mv, w, batch, seq, ca_heads, n_mem):
    tm = ROW_TILE
    nt = seq // tm
    rows, dm = x2d.shape
    dff = w["w_ff_up"].shape[1]
    ca_scale = float((dm // ca_heads) ** -0.5)

    def row_spec(width):
        return pl.BlockSpec((tm, width), lambda i: (i, 0))

    mem_spec = pl.BlockSpec((n_mem, dm), lambda i: (i // nt, 0))
    in_specs = [row_spec(dm), row_spec(o.shape[1]), row_spec(dm), row_spec(dm), mem_spec, mem_spec,
                _const_spec((o.shape[1], dm)), _const_spec((dm, dm)), _const_spec((1, dm)),
                _const_spec((1, dm)), _const_spec((dm, dm)), _const_spec((dm, dm)), _const_spec((1, dm)),
                _const_spec((1, dm)), _const_spec((dm, dff)), _const_spec((dff, dm)), _const_spec((1, dm))]
    kern = functools.partial(_post_prompt_kernel, ca_heads, ca_scale)
    return pl.pallas_call(
        kern, grid=(rows // tm,), in_specs=in_specs, out_specs=row_spec(dm),
        out_shape=jax.ShapeDtypeStruct((rows, dm), F32),
        compiler_params=_params("arbitrary"), name="post_prompt",
    )(x2d, o, mc, sg, mk, mv, w["w_mla_pad"], w["w_mix"], w["g_mix_post"], w["g_ca_pre"], w["w_ca_q"],
      w["w_ca_o"], w["g_ca_post"], w["g_mlp_pre"], w["w_ff_up"], w["w_ff_down"], w["g_mlp_post"])


def _post_sample_a_kernel(n_heads, v_dim, ca_scale,
                          x_ref, olat_ref, mc_ref, sg_ref, wuv_ref, wmla_ref, wmix_ref, gmixpost_ref,
                          gcapre_ref, wcaq_ref, x1_ref, qc_ref):
    n = x_ref.shape[0]
    full = _bdot(olat_ref[...], wuv_ref[...])
    full = full.reshape(n, n_heads, n_heads * v_dim)
    hidx = lax.broadcasted_iota(jnp.int32, full.shape, 1)
    lane_head = lax.broadcasted_iota(jnp.int32, full.shape, 2) // v_dim
    o = jnp.sum(jnp.where(hidx == lane_head, full, 0.0), axis=1)
    x1, qc = _mix_and_query(x_ref[...], o, mc_ref[...], sg_ref[...], wmla_ref, wmix_ref,
                            gmixpost_ref, gcapre_ref, wcaq_ref, ca_scale)
    x1_ref[...] = x1
    qc_ref[...] = qc


def _post_sample_b_kernel(ca_heads, q_ref, mk_ref, mv_ref, o_ref):
    g = q_ref.shape[0]
    dm = q_ref.shape[2]
    hd = dm // ca_heads
    rows = lax.broadcasted_iota(jnp.int32, (SUBLANES, dm), 0)
    lane_head = lax.broadcasted_iota(jnp.int32, (SUBLANES, dm), 1) // hd
    own = rows == lane_head
    for i in range(g):
        q_rep = jnp.where(own, jnp.broadcast_to(q_ref[i], (SUBLANES, dm)), 0.0)
        p = _softmax_rows(_bdot_nt(q_rep, mk_ref[i]))
        full = _bdot(p, mv_ref[i])
        o_ref[i] = jnp.sum(jnp.where(own, full, 0.0), axis=0, keepdims=True)


def _post_sample_c_kernel(x1_ref, oc_ref, wcao_ref, gcapost_ref, gmlppre_ref, wup_ref, wdown_ref,
                          gmlppost_ref, y_ref):
    y_ref[...] = _mlp_tail(x1_ref[...], oc_ref[...], wcao_ref, gcapost_ref, gmlppre_ref, wup_ref,
                           wdown_ref, gmlppost_ref)


def _post_sample(xs, olat, mc, sg, mem_k, mem_v, w, n_heads, v_dim, ca_heads):
    n, dm = xs.shape
    n_mem = mem_k.shape[1]
    ca_scale = float((dm // ca_heads) ** -0.5)
    kern_a = functools.partial(_post_sample_a_kernel, n_heads, v_dim, ca_scale)
    x1, qc = pl.pallas_call(
        kern_a, out_shape=(jax.ShapeDtypeStruct((n, dm), F32), jax.ShapeDtypeStruct((n, dm), F32)),
        compiler_params=_params(), name="post_sample_a",
    )(xs, olat, mc, sg, w["w_uv_flat"], w["w_mla"], w["w_mix"], w["g_mix_post"], w["g_ca_pre"], w["w_ca_q"])

    g = CA_SAMPLES_PER_STEP
    assert n % g == 0
    kern_b = functools.partial(_post_sample_b_kernel, ca_heads)
    oc = pl.pallas_call(
        kern_b, grid=(n // g,),
        in_specs=[pl.BlockSpec((g, 1, dm), lambda i: (i, 0, 0)),
                  pl.BlockSpec((g, n_mem, dm), lambda i: (i, 0, 0)),
                  pl.BlockSpec((g, n_mem, dm), lambda i: (i, 0, 0))],
        out_specs=pl.BlockSpec((g, 1, dm), lambda i: (i, 0, 0)),
        out_shape=jax.ShapeDtypeStruct((n, 1, dm), F32),
        compiler_params=_params("arbitrary"), name="post_sample_b",
    )(qc.reshape(n, 1, dm), mem_k, mem_v)

    return pl.pallas_call(
        _post_sample_c_kernel, out_shape=jax.ShapeDtypeStruct((n, dm), F32),
        compiler_params=_params(), name="post_sample_c",
    )(x1, oc.reshape(n, dm), w["w_ca_o"], w["g_ca_post"], w["g_mlp_pre"], w["w_ff_up"], w["w_ff_down"],
      w["g_mlp_post"])


def _dec_attn_kernel(n_samples, n_pages, n_heads, rope_dim,
                     pt_ref, qlat_ref, qrope_ref, ckvn_ref, krn_ref, cache_ckv, cache_kr,
                     o_ref, cbuf, kbuf, sems):
    ch = DEC_PAGES_PER_CHUNK
    nch = n_pages // ch
    total = n_samples * nch
    page, kvr = cbuf.shape[2], cbuf.shape[3]

    def copies(g, slot):
        b = g // nch
        c = g % nch
        out = []
        for i in range(ch):
            pg = pt_ref[b, c * ch + i]
            out.append(pltpu.make_async_copy(cache_ckv.at[pg], cbuf.at[slot, i], sems.at[0, slot]))
            out.append(pltpu.make_async_copy(cache_kr.at[pg], kbuf.at[slot, i], sems.at[1, slot]))
        return out

    def start(g):
        for cp in copies(g, g % DEC_NBUF):
            cp.start()

    def wait(g):
        for cp in copies(g, g % DEC_NBUF):
            cp.wait()

    for g0 in range(DEC_NBUF - 1):
        start(g0)

    def sample_body(b, carry):
        row0 = pl.multiple_of(b * n_heads, n_heads)
        ql = qlat_ref[pl.ds(row0, n_heads), :]
        qr = qrope_ref[pl.ds(row0, n_heads), :][:, :rope_dim]
        c_new = ckvn_ref[pl.ds(b, 1), :]
        r_new = krn_ref[pl.ds(b, 1), :][:, :rope_dim]
        s_new = (jnp.sum(ql * c_new, axis=-1, keepdims=True)
                 + jnp.sum(qr * r_new, axis=-1, keepdims=True))
        qlb = ql.astype(BF16)
        qrb = qr.astype(BF16)

        def chunk_body(c, st):
            m, l, acc = st
            g = b * nch + c
            wait(g)

            @pl.when(g + (DEC_NBUF - 1) < total)
            def _():
                start(g + (DEC_NBUF - 1))

            slot = g % DEC_NBUF
            ckv = cbuf[slot].reshape(ch * page, kvr).astype(BF16)
            kr = kbuf[slot].reshape(ch * page, rope_dim).astype(BF16)
            s = (lax.dot_general(qlb, ckv, (((1,), (1,)), ((), ())), preferred_element_type=F32)
                 + lax.dot_general(qrb, kr, (((1,), (1,)), ((), ())), preferred_element_type=F32))
            m_new = jnp.maximum(m, jnp.max(s, axis=-1, keepdims=True))
            alpha = jnp.exp(m - m_new)
            p = jnp.exp(s - m_new)
            l = alpha * l + jnp.sum(p, axis=-1, keepdims=True)
            acc = alpha * acc + jnp.dot(p.astype(BF16), ckv, preferred_element_type=F32)
            return m_new, l, acc

        init = (s_new, jnp.ones((n_heads, 1), F32), jnp.broadcast_to(c_new, (n_heads, kvr)))
        m, l, acc = lax.fori_loop(0, nch, chunk_body, init)
        o_ref[pl.ds(row0, n_heads), :] = acc / l
        return carry

    lax.fori_loop(0, n_samples, sample_body, 0)


def _decode_attention(page_table, qlat, qrope, ckv_new, kr_new, cache_ckv, cache_kr, n_heads, rope_dim):
    n_samples, n_pages = page_table.shape
    page, kvr = cache_ckv.shape[1], cache_ckv.shape[2]
    ch = DEC_PAGES_PER_CHUNK
    assert n_pages % ch == 0

    def whole(shape):
        nd = len(shape)
        return pl.BlockSpec(shape, lambda i, pt: (0,) * nd)

    kern = functools.partial(_dec_attn_kernel, n_samples, n_pages, n_heads, rope_dim)
    grid_spec = pltpu.PrefetchScalarGridSpec(
        num_scalar_prefetch=1, grid=(1,),
        in_specs=[whole(qlat.shape), whole(qrope.shape), whole(ckv_new.shape), whole(kr_new.shape),
                  pl.BlockSpec(memory_space=pl.ANY), pl.BlockSpec(memory_space=pl.ANY)],
        out_specs=whole(qlat.shape),
        scratch_shapes=[pltpu.VMEM((DEC_NBUF, ch, page, kvr), F32),
                        pltpu.VMEM((DEC_NBUF, ch, page, rope_dim), F32),
                        pltpu.SemaphoreType.DMA((2, DEC_NBUF))])
    return pl.pallas_call(
        kern, grid_spec=grid_spec, out_shape=jax.ShapeDtypeStruct(qlat.shape, F32),
        compiler_params=_params("arbitrary"), name="decode_attn",
    )(page_table, qlat, qrope, ckv_new, kr_new, cache_ckv, cache_kr)


def _rot_half(w):
    half = w.shape[-1] // 2
    return jnp.concatenate([-w[..., half:], w[..., :half]], axis=-1)


def _rope_tabs(pos, rope_dim, nope_dim):
    inv = 1.0 / (ROPE_BASE ** (jnp.arange(0, rope_dim, 2, dtype=F32) / rope_dim))
    ang = pos.astype(F32)[:, None] * inv[None, :]
    cos, sin = jnp.cos(ang), jnp.sin(ang)
    n = pos.shape[0]
    pad = LANES - rope_dim - nope_dim
    ctab = jnp.concatenate([cos, cos, jnp.ones((n, nope_dim), F32), jnp.zeros((n, pad), F32)], axis=1)
    stab = jnp.concatenate([sin, sin, jnp.zeros((n, LANES - rope_dim), F32)], axis=1)
    return ctab, stab


def kernel(x_prompt, x_sample, mem_prompt, cache_ckv, cache_krope, state_conv, cache_mem_k, cache_mem_v,
           page_table, norm_mix_pre_g, w_in, conv_w, w_conv_out, q_norm_g, w_uq, kv_norm_g, w_uk, w_uv,
           w_mla_out, w_mix_out, norm_mix_post_g, norm_ca_pre_g, mem_norm_g, w_ca_q, w_ca_k, w_ca_v, w_ca_o,
           norm_ca_post_g, norm_mlp_pre_g, w_ff_up, w_ff_down, norm_mlp_post_g):
    depth = w_in.shape[0]
    assert depth == 1, "single-layer step"
    batch, seq, dm = x_prompt.shape
    n_s, t_s, _ = x_sample.shape
    assert t_s == 1
    conv_dim = conv_w.shape[2]
    q_rank, n_heads, qk_dim = w_uq.shape[1:]
    kv_rank, _, nope_dim = w_uk.shape[1:]
    v_dim = w_uv.shape[3]
    rope_dim = qk_dim - nope_dim
    n_mem, ca_heads, ca_hd = cache_mem_k.shape[2:]
    n_pool, page = cache_ckv.shape[1:3]
    past_len = page_table.shape[1] * page
    assert rope_dim + nope_dim <= LANES and v_dim <= LANES
    dims = _PreDims(dm, conv_dim, q_rank, kv_rank, n_heads)
    scale = float(qk_dim ** -0.5)

    w0 = w_in[0]
    o_kr = 3 * conv_dim + q_rank + kv_rank
    w_kr = w0[:, o_kr:o_kr + rope_dim]
    zpad = jnp.zeros((dm, LANES - rope_dim), F32)
    w_in_p = jnp.concatenate([w0[:, :o_kr], w_kr, zpad, _rot_half(w_kr), zpad, w0[:, o_kr + rope_dim:]],
                             axis=1).astype(BF16)
    uq = w_uq[0]
    uq_nope, uq_rope = uq[:, :, :nope_dim], uq[:, :, nope_dim:]
    hz = lambda r, width: jnp.zeros((r, n_heads, width), F32)
    pad_tail = LANES - rope_dim - nope_dim
    w_uq_a = jnp.concatenate([uq_rope, uq_nope, hz(q_rank, pad_tail)], axis=2).reshape(q_rank, n_heads * LANES)
    w_uq_b = jnp.concatenate([_rot_half(uq_rope), hz(q_rank, LANES - rope_dim)], axis=2).reshape(
        q_rank, n_heads * LANES)
    uk = w_uk[0]
    w_uk_pad = jnp.concatenate([hz(kv_rank, rope_dim), uk, hz(kv_rank, pad_tail)], axis=2).reshape(
        kv_rank, n_heads * LANES)
    w_uk_t_pad = jnp.transpose(w_uk_pad.reshape(kv_rank, n_heads, LANES), (1, 2, 0)).reshape(
        n_heads * LANES, kv_rank)
    uv = w_uv[0]
    w_uv_pad = jnp.concatenate([uv, hz(kv_rank, LANES - v_dim)], axis=2).reshape(kv_rank, n_heads * LANES)
    w_mla = w_mla_out[0]
    w_mla_pad = jnp.concatenate([w_mla.reshape(n_heads, v_dim, dm),
                                 jnp.zeros((n_heads, LANES - v_dim, dm), F32)], axis=1).reshape(
        n_heads * LANES, dm)
    w = {
        "g_pre": norm_mix_pre_g, "w_in": w_in_p, "conv_w": conv_w[0], "w_conv_out": w_conv_out[0].astype(BF16),
        "q_g": q_norm_g, "w_uq_a": w_uq_a.astype(BF16), "w_uq_b": w_uq_b.astype(BF16), "kv_g": kv_norm_g,
        "w_uk_pad": w_uk_pad.astype(BF16), "w_uk_t_pad": w_uk_t_pad.astype(BF16),
        "w_uv_pad": w_uv_pad.astype(BF16), "w_uv_flat": uv.reshape(kv_rank, n_heads * v_dim).astype(BF16),
        "w_mla": w_mla.astype(BF16), "w_mla_pad": w_mla_pad.astype(BF16), "w_mix": w_mix_out[0].astype(BF16),
        "g_mix_post": norm_mix_post_g, "g_ca_pre": norm_ca_pre_g,
        "w_ca_q": w_ca_q[0].reshape(dm, ca_heads * ca_hd).astype(BF16),
        "w_ca_o": w_ca_o[0].reshape(ca_heads * ca_hd, dm).astype(BF16),
        "g_ca_post": norm_ca_post_g, "g_mlp_pre": norm_mlp_pre_g, "w_ff_up": w_ff_up[0].astype(BF16),
        "w_ff_down": w_ff_down[0].astype(BF16), "g_mlp_post": norm_mlp_post_g,
    }

    mk_p, mv_p = _memory_kv(mem_prompt.reshape(batch * n_mem, dm), mem_norm_g,
                            w_ca_k[0].reshape(dm, ca_heads * ca_hd).astype(BF16),
                            w_ca_v[0].reshape(dm, ca_heads * ca_hd).astype(BF16))
    ctab_p, stab_p = _rope_tabs(jnp.arange(seq), rope_dim, nope_dim)
    x2d = x_prompt.reshape(batch * seq, dm)
    q, k, v, ckv_p, kr_p, mc_p, sg_p, conv_p = _pre_prompt(dims, scale, x2d, ctab_p, stab_p, w, batch, seq,
                                                          rope_dim)
    o_p = _prompt_attention(q, k, v, batch, seq, n_heads)
    y_p = _post_prompt(x2d, o_p, mc_p, sg_p, mk_p, mv_p, w, batch, seq, ca_heads, n_mem)

    ctab_s, stab_s = _rope_tabs(jnp.full((n_s,), past_len, jnp.int32), rope_dim, nope_dim)
    xs = x_sample.reshape(n_s, dm)
    q_s, qlat_s, ckv_s, kr_s, mc_s, sg_s, conv_s = _pre_sample(
        dims, scale, xs, state_conv.reshape(n_s, (CONV_WIDTH - 1) * conv_dim), ctab_s, stab_s, w)
    olat = _decode_attention(page_table, qlat_s.reshape(n_s * n_heads, kv_rank),
                             q_s.reshape(n_s * n_heads, LANES), ckv_s, kr_s,
                             cache_ckv.reshape(n_pool, page, kv_rank),
                             cache_krope.reshape(n_pool, page, rope_dim), n_heads, rope_dim)
    y_s = _post_sample(xs, olat, mc_s, sg_s, cache_mem_k.reshape(n_s, n_mem, ca_heads * ca_hd),
                       cache_mem_v.reshape(n_s, n_mem, ca_heads * ca_hd), w, n_heads, v_dim, ca_heads)

    return (y_p.reshape(batch, seq, dm),
            y_s.reshape(n_s, t_s, dm),
            ckv_p.reshape(depth, batch, seq, kv_rank),
            kr_p.reshape(depth, batch, seq, rope_dim),
            conv_p.reshape(depth, batch, CONV_WIDTH - 1, conv_dim),
            mk_p.reshape(depth, batch, n_mem, ca_heads, ca_hd),
            mv_p.reshape(depth, batch, n_mem, ca_heads, ca_hd),
            ckv_s.reshape(depth, n_s, t_s, kv_rank),
            kr_s[:, :rope_dim].reshape(depth, n_s, t_s, rope_dim),
            conv_s.reshape(depth, n_s, CONV_WIDTH - 1, conv_dim))
```

```python
import functools

import numpy as np
import jax
import jax.numpy as jnp
from jax import lax
from jax.experimental import pallas as pl
from jax.experimental.pallas import tpu as pltpu

F32 = jnp.float32
BF16 = jnp.bfloat16

RMS_EPS = 1e-6
NEG_INF = -1e30
ROPE_BASE = 10000.0
CONV_WIDTH = 3
LOG2_E = 1.4426950408889634

LANES = 128
SUBLANES = 8
VMEM_LIMIT_BYTES = 56 * 1024 * 1024

ROW_TILE = 512
ATTN_TQ = 512
ATTN_TK = 512
ATTN_V_EXTRA_ROWS = 16
DEC_PAGES_PER_STEP = 16
DEC_NBUF = 4
CA_SAMPLES_PER_STEP = 4


def _rms(x, g):
    return x * lax.rsqrt(jnp.mean(x * x, axis=-1, keepdims=True) + RMS_EPS) * g


def _bdot(a, b):
    return jnp.dot(a.astype(BF16), b.astype(BF16), preferred_element_type=F32)


def _bdot_nt(a, b):
    return lax.dot_general(a.astype(BF16), b.astype(BF16), (((1,), (1,)), ((), ())),
                           preferred_element_type=F32)


def _softmax_rows(s):
    m = jnp.max(s, axis=-1, keepdims=True)
    p = jnp.exp(s - m)
    return p / jnp.sum(p, axis=-1, keepdims=True)


def _const_spec(shape):
    nd = len(shape)
    return pl.BlockSpec(shape, lambda *_: (0,) * nd, pipeline_mode=pl.Buffered(1))


def _params(*sem):
    return pltpu.CompilerParams(dimension_semantics=tuple(sem) if sem else None,
                                vmem_limit_bytes=VMEM_LIMIT_BYTES)


def _memkv_kernel(mem_ref, g_ref, wk_ref, wv_ref, k_ref, v_ref):
    mn = _rms(mem_ref[...], g_ref[...]).astype(BF16)
    k_ref[...] = jnp.dot(mn, wk_ref[...], preferred_element_type=F32)
    v_ref[...] = jnp.dot(mn, wv_ref[...], preferred_element_type=F32)


def _memory_kv(mem2d, g, wk, wv):
    rows, d = mem2d.shape
    out = jax.ShapeDtypeStruct((rows, wk.shape[1]), F32)
    return pl.pallas_call(_memkv_kernel, out_shape=(out, out), name="mem_kv",
                          compiler_params=_params())(mem2d, g, wk, wv)


class _PreDims:
    def __init__(self, d_model, conv_dim, q_rank, kv_rank, n_heads):
        self.d_model, self.conv_dim, self.q_rank, self.kv_rank = d_model, conv_dim, q_rank, kv_rank
        self.n_heads = n_heads
        c = conv_dim
        self.o_h, self.o_gb, self.o_gc = 0, c, 2 * c
        self.o_cq = 3 * c
        self.o_ckv = self.o_cq + q_rank
        self.o_kra = self.o_ckv + kv_rank
        self.o_krb = self.o_kra + LANES
        self.o_gconv = self.o_krb + LANES
        self.o_gmla = self.o_gconv + d_model
        self.n_in = self.o_gmla + d_model
        self.hw = n_heads * LANES


def _pre_common(dims, x, ctab, stab, gpre_ref, win_ref, qg_ref, wuqa_ref, wuqb_ref, kvg_ref, scale):
    d = dims
    xn = _rms(x, gpre_ref[...]).astype(BF16)

    def proj(lo, hi):
        return jnp.dot(xn, win_ref[:, lo:hi], preferred_element_type=F32)

    h = proj(d.o_h, d.o_gb)
    gate_b = proj(d.o_gb, d.o_gc)
    gate_c = proj(d.o_gc, d.o_cq)
    cq = proj(d.o_cq, d.o_ckv)
    ckv = proj(d.o_ckv, d.o_kra)
    kra = proj(d.o_kra, d.o_krb)
    krb = proj(d.o_krb, d.o_gconv)
    g_conv = proj(d.o_gconv, d.o_gmla)
    g_mla = proj(d.o_gmla, d.n_in)

    u = gate_c * h
    cqn = _rms(cq, qg_ref[...]).astype(BF16)
    qa = jnp.dot(cqn, wuqa_ref[...], preferred_element_type=F32)
    qb = jnp.dot(cqn, wuqb_ref[...], preferred_element_type=F32)
    ct = jnp.concatenate([ctab] * d.n_heads, axis=1)
    st = jnp.concatenate([stab] * d.n_heads, axis=1)
    q = (qa * ct + qb * st) * scale
    ckvn = _rms(ckv, kvg_ref[...])
    krr = kra * ctab + krb * stab
    return u, gate_b, g_conv, g_mla, q, ckvn, krr


def _pre_prompt_kernel(dims, scale, tm, dv, dvp,
                       x_ref, ctab_ref, stab_ref, gpre_ref, win_ref, convw_ref, wco_ref, qg_ref,
                       wuqa_ref, wuqb_ref, kvg_ref, wuk_ref, wuvt_ref,
                       q_out, k_out, vt_out, ckv_out, kr_out, mc_out, sg_out, conv_out, ubuf):
    hist = SUBLANES
    u, gate_b, g_conv, g_mla, q, ckvn, krr = _pre_common(
        dims, x_ref[...], ctab_ref[...], stab_ref[...], gpre_ref, win_ref, qg_ref, wuqa_ref,
        wuqb_ref, kvg_ref, scale)

    @pl.when(pl.program_id(1) == 0)
    def _():
        ubuf[0:hist, :] = jnp.zeros((hist, dims.conv_dim), F32)

    ubuf[hist:hist + tm, :] = u
    u1 = ubuf[hist - 1:hist - 1 + tm, :]
    u2 = ubuf[hist - 2:hist - 2 + tm, :]
    cw = convw_ref[...]
    conv = u2 * cw[0:1, :] + u1 * cw[1:2, :] + u * cw[2:3, :]
    y_conv = _bdot(gate_b * conv, wco_ref[...])
    mc_out[...] = (jax.nn.sigmoid(g_conv) * y_conv).astype(mc_out.dtype)
    sg_out[...] = jax.nn.sigmoid(g_mla).astype(sg_out.dtype)
    conv_out[...] = ubuf[hist + tm - (CONV_WIDTH - 1):hist + tm, :]
    ubuf[0:hist, :] = ubuf[tm:tm + hist, :]

    q_out[...] = q.astype(q_out.dtype)
    ckv_out[...] = ckvn
    kr_out[...] = krr[:, :kr_out.shape[-1]]
    ckvb = ckvn.astype(BF16)
    ka = jnp.dot(ckvb, wuk_ref[...], preferred_element_type=F32)
    k = ka + jnp.concatenate([krr] * dims.n_heads, axis=1)
    k_out[...] = k.astype(k_out.dtype)
    vt = lax.dot_general(wuvt_ref[...], ckvb, (((1,), (1,)), ((), ())), preferred_element_type=F32)
    rid = lax.broadcasted_iota(jnp.int32, (vt.shape[0], 1), 0)
    vt_out[...] = (vt + jnp.where(rid % dvp == dv, 1.0, 0.0)).astype(vt_out.dtype)


def _pre_prompt(dims, scale, x2d, ctab, stab, w, batch, seq, rope_dim, dv, dvp):
    tm = ROW_TILE
    assert seq % tm == 0
    nt = seq // tm
    rows = batch * seq
    d = dims
    hv = w["w_uv_t"].shape[0]

    def row_spec(width):
        return pl.BlockSpec((tm, width), lambda b, t: (b * nt + t, 0))

    def tab_spec():
        return pl.BlockSpec((tm, LANES), lambda b, t: (t, 0))

    in_specs = [row_spec(d.d_model), tab_spec(), tab_spec(),
                _const_spec((1, d.d_model)), _const_spec((d.d_model, d.n_in)),
                _const_spec((CONV_WIDTH, d.conv_dim)), _const_spec((d.conv_dim, d.d_model)),
                _const_spec((1, d.q_rank)), _const_spec((d.q_rank, d.hw)), _const_spec((d.q_rank, d.hw)),
                _const_spec((1, d.kv_rank)), _const_spec((d.kv_rank, d.hw)), _const_spec((hv, d.kv_rank))]
    out_shape = (jax.ShapeDtypeStruct((rows, d.hw), BF16),
                 jax.ShapeDtypeStruct((rows, d.hw), BF16),
                 jax.ShapeDtypeStruct((batch, hv, seq), BF16),
                 jax.ShapeDtypeStruct((rows, d.kv_rank), F32),
                 jax.ShapeDtypeStruct((rows, rope_dim), F32),
                 jax.ShapeDtypeStruct((rows, d.d_model), BF16),
                 jax.ShapeDtypeStruct((rows, d.d_model), BF16),
                 jax.ShapeDtypeStruct((batch, CONV_WIDTH - 1, d.conv_dim), F32))
    out_specs = (row_spec(d.hw), row_spec(d.hw),
                 pl.BlockSpec((None, hv, tm), lambda b, t: (b, 0, t)), row_spec(d.kv_rank),
                 row_spec(rope_dim), row_spec(d.d_model), row_spec(d.d_model),
                 pl.BlockSpec((None, CONV_WIDTH - 1, d.conv_dim), lambda b, t: (b, 0, 0)))
    kern = functools.partial(_pre_prompt_kernel, dims, scale, tm, dv, dvp)
    return pl.pallas_call(
        kern, grid=(batch, nt), in_specs=in_specs, out_specs=out_specs, out_shape=out_shape,
        scratch_shapes=[pltpu.VMEM((tm + SUBLANES, d.conv_dim), F32)],
        compiler_params=_params("arbitrary", "arbitrary"), name="pre_prompt",
    )(x2d, ctab, stab, w["g_pre"], w["w_in"], w["conv_w"], w["w_conv_out"], w["q_g"],
      w["w_uq_a"], w["w_uq_b"], w["kv_g"], w["w_uk_pad"], w["w_uv_t"])


def _pre_sample_kernel(dims, scale,
                       x_ref, state_ref, ctab_ref, stab_ref, gpre_ref, win_ref, convw_ref, wco_ref,
                       qg_ref, wuqa_ref, wuqb_ref, kvg_ref, wukt_ref,
                       q_out, qlat_out, ckv_out, kr_out, mc_out, sg_out, conv_out):
    c = dims.conv_dim
    u, gate_b, g_conv, g_mla, q, ckvn, krr = _pre_common(
        dims, x_ref[...], ctab_ref[...], stab_ref[...], gpre_ref, win_ref, qg_ref, wuqa_ref,
        wuqb_ref, kvg_ref, scale)
    u2 = state_ref[:, 0:c]
    u1 = state_ref[:, c:2 * c]
    cw = convw_ref[...]
    conv = u2 * cw[0:1, :] + u1 * cw[1:2, :] + u * cw[2:3, :]
    y_conv = _bdot(gate_b * conv, wco_ref[...])
    mc_out[...] = jax.nn.sigmoid(g_conv) * y_conv
    sg_out[...] = jax.nn.sigmoid(g_mla)
    conv_out[:, 0:c] = u1
    conv_out[:, c:2 * c] = u
    q_out[...] = q
    ckv_out[...] = ckvn
    kr_out[...] = krr
    kvr = dims.kv_rank
    qb16 = q.astype(BF16)
    for hd in range(dims.n_heads):
        blk = qb16[:, hd * LANES:(hd + 1) * LANES]
        qlat_out[:, hd * kvr:(hd + 1) * kvr] = jnp.dot(
            blk, wukt_ref[hd * LANES:(hd + 1) * LANES, :], preferred_element_type=F32)


def _pre_sample(dims, scale, xs, state2d, ctab, stab, w):
    n = xs.shape[0]
    d = dims
    out_shape = (jax.ShapeDtypeStruct((n, d.hw), F32),
                 jax.ShapeDtypeStruct((n, d.n_heads * d.kv_rank), F32),
                 jax.ShapeDtypeStruct((n, d.kv_rank), F32),
                 jax.ShapeDtypeStruct((n, LANES), F32),
                 jax.ShapeDtypeStruct((n, d.d_model), F32),
                 jax.ShapeDtypeStruct((n, d.d_model), F32),
                 jax.ShapeDtypeStruct((n, (CONV_WIDTH - 1) * d.conv_dim), F32))
    kern = functools.partial(_pre_sample_kernel, dims, scale)
    return pl.pallas_call(kern, out_shape=out_shape, compiler_params=_params(), name="pre_sample")(
        xs, state2d, ctab, stab, w["g_pre"], w["w_in"], w["conv_w"], w["w_conv_out"], w["q_g"],
        w["w_uq_a"], w["w_uq_b"], w["kv_g"], w["w_uk_t_pad"])


def _attn_kernel(tq, tk, hp, dv, dvp, q_ref, k_ref, vt_ref, o_ref, sa_ref, sb_ref):
    qi = pl.program_id(2)

    def scores(j, dst):
        start = pl.multiple_of(j * tk, tk)
        for h in range(hp):
            kh = k_ref[pl.ds(start, tk), h * LANES:(h + 1) * LANES]
            qh = q_ref[:, h * LANES:(h + 1) * LANES]
            dst[h] = lax.dot_general(kh, qh, (((1,), (1,)), ((), ())), preferred_element_type=F32)

    def process(j, src, state, masked):
        start = pl.multiple_of(j * tk, tk)
        new_state = []
        for h in range(hp):
            m_prev, acc = state[h]
            st = src[h]
            if masked:
                key = lax.broadcasted_iota(jnp.int32, (tk, tq), 0) + j * tk
                qry = lax.broadcasted_iota(jnp.int32, (tk, tq), 1) + qi * tq
                st = jnp.where(key <= qry, st, NEG_INF)
            m_new = jnp.maximum(m_prev, jnp.max(st, axis=0, keepdims=True))
            alpha = jnp.exp2(m_prev - m_new)
            p = jnp.exp2(st - m_new).astype(BF16)
            vth = vt_ref[h * dvp:(h + 1) * dvp, pl.ds(start, tk)]
            acc = alpha * acc + jnp.dot(vth, p, preferred_element_type=F32)
            new_state.append((m_new, acc))
        return tuple(new_state)

    n_full = qi * (tq // tk)
    assert tq == tk
    init = tuple((jnp.full((1, tq), NEG_INF, F32), jnp.zeros((dvp, tq), F32)) for _ in range(hp))
    scores(0, sa_ref)

    def pair(t, state):
        scores(2 * t + 1, sb_ref)
        state = process(2 * t, sa_ref, state, False)
        scores(2 * t + 2, sa_ref)
        return process(2 * t + 1, sb_ref, state, False)

    n_pairs = n_full // 2
    state = lax.fori_loop(0, n_pairs, pair, init)
    ja = 2 * n_pairs
    odd = n_full - ja
    scores(ja + odd, sb_ref)
    state = process(ja, sa_ref, state, True)
    state = lax.cond(odd == 1, lambda st: process(ja + 1, sb_ref, st, True), lambda st: st, state)
    ot = jnp.concatenate([acc[0:dv] / acc[dv:dv + 1] for (_, acc) in state], axis=0)
    o_ref[...] = ot.T.astype(o_ref.dtype)


def _prompt_attention(q, k, vt, batch, seq, n_heads, dv):
    tq, tk = ATTN_TQ, ATTN_TK
    assert seq % tq == 0 and tq % tk == 0
    hp = LANES // dv
    assert n_heads % hp == 0
    dvp = vt.shape[1] // n_heads
    nq = seq // tq
    kern = functools.partial(_attn_kernel, tq, tk, hp, dv, dvp)
    return pl.pallas_call(
        kern, grid=(batch, n_heads // hp, nq),
        in_specs=[pl.BlockSpec((tq, hp * LANES), lambda b, g, i: (b * nq + i, g)),
                  pl.BlockSpec((seq, hp * LANES), lambda b, g, i: (b, g)),
                  pl.BlockSpec((None, hp * dvp, seq), lambda b, g, i: (b, g, 0))],
        out_specs=pl.BlockSpec((tq, hp * dv), lambda b, g, i: (b * nq + i, g)),
        out_shape=jax.ShapeDtypeStruct((batch * seq, n_heads * dv), BF16),
        scratch_shapes=[pltpu.VMEM((hp, tk, tq), F32), pltpu.VMEM((hp, tk, tq), F32)],
        compiler_params=_params("arbitrary", "arbitrary", "arbitrary"), name="prompt_attn",
    )(q, k, vt)


def _mix_and_query(x, o, mc, sg, wmla_ref, wmix_ref, gmixpost_ref, gcapre_ref, wcaq_ref, ca_scale):
    y_mla = _bdot(o, wmla_ref[...])
    merged = mc.astype(F32) + sg.astype(F32) * y_mla
    y = _bdot(merged, wmix_ref[...])
    x1 = x + _rms(y, gmixpost_ref[...])
    qc = _bdot(_rms(x1, gcapre_ref[...]), wcaq_ref[...]) * ca_scale
    return x1, qc


def _mlp_tail(x1, oc, wcao_ref, gcapost_ref, gmlppre_ref, wup_ref, wdown_ref, gmlppost_ref):
    ca = _bdot(oc, wcao_ref[...])
    x2 = x1 + _rms(ca, gcapost_ref[...])
    hid = jnp.square(jnp.maximum(_bdot(_rms(x2, gmlppre_ref[...]), wup_ref[...]), 0.0))
    return x2 + _rms(_bdot(hid, wdown_ref[...]), gmlppost_ref[...])


def _post_prompt_kernel(ca_heads, ca_scale,
                        x_ref, o_ref, mc_ref, sg_ref, mk_ref, mv_ref,
                        wmla_ref, wmix_ref, gmixpost_ref, gcapre_ref, wcaq_ref, wcao_ref, gcapost_ref,
                        gmlppre_ref, wup_ref, wdown_ref, gmlppost_ref, y_ref):
    x1, qc = _mix_and_query(x_ref[...], o_ref[...], mc_ref[...], sg_ref[...], wmla_ref, wmix_ref,
                            gmixpost_ref, gcapre_ref, wcaq_ref, ca_scale)
    hd = qc.shape[1] // ca_heads
    outs = []
    for h in range(ca_heads):
        sl = slice(h * hd, (h + 1) * hd)
        p = _softmax_rows(_bdot_nt(qc[:, sl], mk_ref[:, sl]))
        outs.append(_bdot(p, mv_ref[:, sl]))
    oc = jnp.concatenate(outs, axis=1)
    y_ref[...] = _mlp_tail(x1, oc, wcao_ref, gcapost_ref, gmlppre_ref, wup_ref, wdown_ref, gmlppost_ref)


def _post_prompt(x2d, o, mc, sg, mk, mv, w, batch, seq, ca_heads, n_mem):
    tm = ROW_TILE
    nt = seq // tm
    rows, dm = x2d.shape
    dff = w["w_ff_up"].shape[1]
    ca_scale = float((dm // ca_heads) ** -0.5)

    def row_spec(width):
        return pl.BlockSpec((tm, width), lambda i: (i, 0))

    mem_spec = pl.BlockSpec((n_mem, dm), lambda i: (i // nt, 0))
    in_specs = [row_spec(dm), row_spec(o.shape[1]), row_spec(dm), row_spec(dm), mem_spec, mem_spec,
                _const_spec((o.shape[1], dm)), _const_spec((dm, dm)), _const_spec((1, dm)),
                _const_spec((1, dm)), _const_spec((dm, dm)), _const_spec((dm, dm)), _const_spec((1, dm)),
                _const_spec((1, dm)), _const_spec((dm, dff)), _const_spec((dff, dm)), _const_spec((1, dm))]
    kern = functools.partial(_post_prompt_kernel, ca_heads, ca_scale)
    return pl.pallas_call(
        kern, grid=(rows // tm,), in_specs=in_specs, out_specs=row_spec(dm),
        out_shape=jax.ShapeDtypeStruct((rows, dm), F32),
        compiler_params=_params("arbitrary"), name="post_prompt",
    )(x2d, o, mc, sg, mk, mv, w["w_mla"], w["w_mix"], w["g_mix_post"], w["g_ca_pre"], w["w_ca_q"],
      w["w_ca_o"], w["g_ca_post"], w["g_mlp_pre"], w["w_ff_up"], w["w_ff_down"], w["g_mlp_post"])


def _post_sample_a_kernel(n_heads, v_dim, ca_scale,
                          x_ref, olat_ref, mc_ref, sg_ref, wuv_ref, wmla_ref, wmix_ref, gmixpost_ref,
                          gcapre_ref, wcaq_ref, x1_ref, qc_ref):
    n = x_ref.shape[0]
    full = _bdot(olat_ref[...], wuv_ref[...])
    full = full.reshape(n, n_heads, n_heads * v_dim)
    hidx = lax.broadcasted_iota(jnp.int32, full.shape, 1)
    lane_head = lax.broadcasted_iota(jnp.int32, full.shape, 2) // v_dim
    o = jnp.sum(jnp.where(hidx == lane_head, full, 0.0), axis=1)
    x1, qc = _mix_and_query(x_ref[...], o, mc_ref[...], sg_ref[...], wmla_ref, wmix_ref,
                            gmixpost_ref, gcapre_ref, wcaq_ref, ca_scale)
    x1_ref[...] = x1
    qc_ref[...] = qc


def _post_sample_b_kernel(ca_heads, q_ref, mk_ref, mv_ref, o_ref):
    g = q_ref.shape[0]
    dm = q_ref.shape[2]
    hd = dm // ca_heads
    rows = lax.broadcasted_iota(jnp.int32, (SUBLANES, dm), 0)
    lane_head = lax.broadcasted_iota(jnp.int32, (SUBLANES, dm), 1) // hd
    own = rows == lane_head
    for i in range(g):
        q_rep = jnp.where(own, jnp.broadcast_to(q_ref[i], (SUBLANES, dm)), 0.0)
        p = _softmax_rows(_bdot_nt(q_rep, mk_ref[i]))
        full = _bdot(p, mv_ref[i])
        o_ref[i] = jnp.sum(jnp.where(own, full, 0.0), axis=0, keepdims=True)


def _post_sample_c_kernel(x1_ref, oc_ref, wcao_ref, gcapost_ref, gmlppre_ref, wup_ref, wdown_ref,
                          gmlppost_ref, y_ref):
    y_ref[...] = _mlp_tail(x1_ref[...], oc_ref[...], wcao_ref, gcapost_ref, gmlppre_ref, wup_ref,
                           wdown_ref, gmlppost_ref)


def _post_sample(xs, olat, mc, sg, mem_k, mem_v, w, n_heads, v_dim, ca_heads):
    n, dm = xs.shape
    n_mem = mem_k.shape[1]
    ca_scale = float((dm // ca_heads) ** -0.5)
    kern_a = functools.partial(_post_sample_a_kernel, n_heads, v_dim, ca_scale)
    x1, qc = pl.pallas_call(
        kern_a, out_shape=(jax.ShapeDtypeStruct((n, dm), F32), jax.ShapeDtypeStruct((n, dm), F32)),
        compiler_params=_params(), name="post_sample_a",
    )(xs, olat, mc, sg, w["w_uv_flat"], w["w_mla"], w["w_mix"], w["g_mix_post"], w["g_ca_pre"], w["w_ca_q"])

    g = CA_SAMPLES_PER_STEP
    assert n % g == 0
    kern_b = functools.partial(_post_sample_b_kernel, ca_heads)
    oc = pl.pallas_call(
        kern_b, grid=(n // g,),
        in_specs=[pl.BlockSpec((g, 1, dm), lambda i: (i, 0, 0)),
                  pl.BlockSpec((g, n_mem, dm), lambda i: (i, 0, 0)),
                  pl.BlockSpec((g, n_mem, dm), lambda i: (i, 0, 0))],
        out_specs=pl.BlockSpec((g, 1, dm), lambda i: (i, 0, 0)),
        out_shape=jax.ShapeDtypeStruct((n, 1, dm), F32),
        compiler_params=_params("arbitrary"), name="post_sample_b",
    )(qc.reshape(n, 1, dm), mem_k, mem_v)

    return pl.pallas_call(
        _post_sample_c_kernel, out_shape=jax.ShapeDtypeStruct((n, dm), F32),
        compiler_params=_params(), name="post_sample_c",
    )(x1, oc.reshape(n, dm), w["w_ca_o"], w["g_ca_post"], w["g_mlp_pre"], w["w_ff_up"], w["w_ff_down"],
      w["g_mlp_post"])


def _dec_attn_kernel(n_samples, n_pages, n_heads, rope_dim,
                     pt_ref, qlat_ref, qrope_ref, ckvn_ref, krn_ref, cache_ckv, cache_krt,
                     o_ref, cbuf, kbuf, sems):
    ch = DEC_PAGES_PER_STEP
    nch = n_pages // ch
    total = n_samples * nch
    page, kvr = cbuf.shape[2], cbuf.shape[3]

    def copies(g, slot):
        b = g // nch
        c = g % nch
        out = []
        for i in range(ch):
            pg = pt_ref[b, c * ch + i]
            out.append(pltpu.make_async_copy(cache_ckv.at[pg], cbuf.at[slot, i], sems.at[0, slot]))
            out.append(pltpu.make_async_copy(cache_krt.at[pg], kbuf.at[slot, i], sems.at[1, slot]))
        return out

    def start(g):
        for cp in copies(g, g % DEC_NBUF):
            cp.start()

    def wait(g):
        for cp in copies(g, g % DEC_NBUF):
            cp.wait()

    def q_rows(b):
        row0 = pl.multiple_of(b * n_heads, n_heads)
        return qlat_ref[pl.ds(row0, n_heads), :], qrope_ref[pl.ds(row0, n_heads), :][:, :rope_dim]

    def scores(g):
        ql, qr = q_rows(g // nch)
        slot = g % DEC_NBUF
        ckv = cbuf[slot].reshape(ch * page, kvr).astype(BF16)
        qrb = qr.astype(BF16)
        s_rope = jnp.concatenate(
            [jnp.dot(qrb, kbuf[slot, i].astype(BF16), preferred_element_type=F32) for i in range(ch)], axis=1)
        return lax.dot_general(ql.astype(BF16), ckv, (((1,), (1,)), ((), ())),
                               preferred_element_type=F32) + s_rope

    for g0 in range(DEC_NBUF - 1):
        start(g0)
    wait(0)
    s0 = scores(0)

    def body(g, carry):
        s_cur, m, l, acc = carry
        b = g // nch
        c = g % nch

        @pl.when(g + 1 < total)
        def _():
            wait(g + 1)

        @pl.when(g + (DEC_NBUF - 1) < total)
        def _():
            start(g + (DEC_NBUF - 1))

        s_next = scores(jnp.minimum(g + 1, total - 1))

        ql, qr = q_rows(b)
        c_new = ckvn_ref[pl.ds(b, 1), :]
        r_new = krn_ref[pl.ds(b, 1), :][:, :rope_dim]
        s_new = jnp.sum(ql * c_new, axis=-1, keepdims=True) + jnp.sum(qr * r_new, axis=-1, keepdims=True)
        first = c == 0
        m = jnp.where(first, s_new, m)
        l = jnp.where(first, 1.0, l)
        acc = jnp.where(first, jnp.broadcast_to(c_new, acc.shape), acc)

        ckv = cbuf[g % DEC_NBUF].reshape(ch * page, kvr).astype(BF16)
        m_new = jnp.maximum(m, jnp.max(s_cur, axis=-1, keepdims=True))
        alpha = jnp.exp(m - m_new)
        p = jnp.exp(s_cur - m_new)
        l = alpha * l + jnp.sum(p, axis=-1, keepdims=True)
        acc = alpha * acc + jnp.dot(p.astype(BF16), ckv, preferred_element_type=F32)
        o_ref[pl.ds(pl.multiple_of(b * n_heads, n_heads), n_heads), :] = acc / l
        return s_next, m_new, l, acc

    init = (s0, jnp.zeros((n_heads, 1), F32), jnp.zeros((n_heads, 1), F32), jnp.zeros((n_heads, kvr), F32))
    lax.fori_loop(0, total, body, init)


def _decode_attention(page_table, qlat, qrope, ckv_new, kr_new, cache_ckv, cache_krt, n_heads, rope_dim):
    n_samples, n_pages = page_table.shape
    page, kvr = cache_ckv.shape[1], cache_ckv.shape[2]
    ch = DEC_PAGES_PER_STEP
    assert n_pages % ch == 0

    def whole(shape):
        nd = len(shape)
        return pl.BlockSpec(shape, lambda i, pt: (0,) * nd)

    kern = functools.partial(_dec_attn_kernel, n_samples, n_pages, n_heads, rope_dim)
    grid_spec = pltpu.PrefetchScalarGridSpec(
        num_scalar_prefetch=1, grid=(1,),
        in_specs=[whole(qlat.shape), whole(qrope.shape), whole(ckv_new.shape), whole(kr_new.shape),
                  pl.BlockSpec(memory_space=pl.ANY), pl.BlockSpec(memory_space=pl.ANY)],
        out_specs=whole(qlat.shape),
        scratch_shapes=[pltpu.VMEM((DEC_NBUF, ch, page, kvr), F32),
                        pltpu.VMEM((DEC_NBUF, ch, rope_dim, page), F32),
                        pltpu.SemaphoreType.DMA((2, DEC_NBUF))])
    return pl.pallas_call(
        kern, grid_spec=grid_spec, out_shape=jax.ShapeDtypeStruct(qlat.shape, F32),
        compiler_params=_params("arbitrary"), name="decode_attn",
    )(page_table, qlat, qrope, ckv_new, kr_new, cache_ckv, cache_krt)


def _rot_half(w):
    half = w.shape[-1] // 2
    return jnp.concatenate([-w[..., half:], w[..., :half]], axis=-1)


def _rope_tabs(pos, rope_dim, nope_dim):
    inv = 1.0 / (ROPE_BASE ** (jnp.arange(0, rope_dim, 2, dtype=F32) / rope_dim))
    ang = pos.astype(F32)[:, None] * inv[None, :]
    cos, sin = jnp.cos(ang), jnp.sin(ang)
    n = pos.shape[0]
    pad = LANES - rope_dim - nope_dim
    ctab = jnp.concatenate([cos, cos, jnp.ones((n, nope_dim), F32), jnp.zeros((n, pad), F32)], axis=1)
    stab = jnp.concatenate([sin, sin, jnp.zeros((n, LANES - rope_dim), F32)], axis=1)
    return ctab, stab


def kernel(x_prompt, x_sample, mem_prompt, cache_ckv, cache_krope, state_conv, cache_mem_k, cache_mem_v,
           page_table, norm_mix_pre_g, w_in, conv_w, w_conv_out, q_norm_g, w_uq, kv_norm_g, w_uk, w_uv,
           w_mla_out, w_mix_out, norm_mix_post_g, norm_ca_pre_g, mem_norm_g, w_ca_q, w_ca_k, w_ca_v, w_ca_o,
           norm_ca_post_g, norm_mlp_pre_g, w_ff_up, w_ff_down, norm_mlp_post_g):
    depth = w_in.shape[0]
    assert depth == 1, "single-layer step"
    batch, seq, dm = x_prompt.shape
    n_s, t_s, _ = x_sample.shape
    assert t_s == 1
    conv_dim = conv_w.shape[2]
    q_rank, n_heads, qk_dim = w_uq.shape[1:]
    kv_rank, _, nope_dim = w_uk.shape[1:]
    v_dim = w_uv.shape[3]
    rope_dim = qk_dim - nope_dim
    n_mem, ca_heads, ca_hd = cache_mem_k.shape[2:]
    n_pool, page = cache_ckv.shape[1:3]
    past_len = page_table.shape[1] * page
    assert rope_dim + nope_dim <= LANES and v_dim <= LANES
    dims = _PreDims(dm, conv_dim, q_rank, kv_rank, n_heads)
    scale = float(qk_dim ** -0.5)

    w0 = w_in[0]
    o_kr = 3 * conv_dim + q_rank + kv_rank
    w_kr = w0[:, o_kr:o_kr + rope_dim]
    zpad = jnp.zeros((dm, LANES - rope_dim), F32)
    w_in_p = jnp.concatenate([w0[:, :o_kr], w_kr, zpad, _rot_half(w_kr), zpad, w0[:, o_kr + rope_dim:]],
                             axis=1).astype(BF16)
    uq = w_uq[0]
    uq_nope, uq_rope = uq[:, :, :nope_dim], uq[:, :, nope_dim:]
    hz = lambda r, width: jnp.zeros((r, n_heads, width), F32)
    pad_tail = LANES - rope_dim - nope_dim
    w_uq_a = jnp.concatenate([uq_rope, uq_nope, hz(q_rank, pad_tail)], axis=2).reshape(q_rank, n_heads * LANES)
    w_uq_b = jnp.concatenate([_rot_half(uq_rope), hz(q_rank, LANES - rope_dim)], axis=2).reshape(
        q_rank, n_heads * LANES)
    uk = w_uk[0]
    w_uk_pad = jnp.concatenate([hz(kv_rank, rope_dim), uk, hz(kv_rank, pad_tail)], axis=2).reshape(
        kv_rank, n_heads * LANES)
    w_uk_t_pad = jnp.transpose(w_uk_pad.reshape(kv_rank, n_heads, LANES), (1, 2, 0)).reshape(
        n_heads * LANES, kv_rank)
    uv = w_uv[0].reshape(kv_rank, n_heads * v_dim)
    dvp = v_dim + ATTN_V_EXTRA_ROWS
    w_uv_t = jnp.concatenate([jnp.transpose(w_uv[0], (1, 2, 0)),
                              jnp.zeros((n_heads, ATTN_V_EXTRA_ROWS, kv_rank), F32)], axis=1).reshape(
        n_heads * dvp, kv_rank)
    w_mla = w_mla_out[0]
    w = {
        "g_pre": norm_mix_pre_g, "w_in": w_in_p, "conv_w": conv_w[0], "w_conv_out": w_conv_out[0].astype(BF16),
        "q_g": q_norm_g, "w_uq_a": w_uq_a.astype(BF16), "w_uq_b": w_uq_b.astype(BF16), "kv_g": kv_norm_g,
        "w_uk_pad": w_uk_pad.astype(BF16), "w_uk_t_pad": w_uk_t_pad.astype(BF16),
        "w_uv_t": w_uv_t.astype(BF16), "w_uv_flat": uv.astype(BF16),
        "w_mla": w_mla.astype(BF16), "w_mix": w_mix_out[0].astype(BF16),
        "g_mix_post": norm_mix_post_g, "g_ca_pre": norm_ca_pre_g,
        "w_ca_q": w_ca_q[0].reshape(dm, ca_heads * ca_hd).astype(BF16),
        "w_ca_o": w_ca_o[0].reshape(ca_heads * ca_hd, dm).astype(BF16),
        "g_ca_post": norm_ca_post_g, "g_mlp_pre": norm_mlp_pre_g, "w_ff_up": w_ff_up[0].astype(BF16),
        "w_ff_down": w_ff_down[0].astype(BF16), "g_mlp_post": norm_mlp_post_g,
    }

    mk_p, mv_p = _memory_kv(mem_prompt.reshape(batch * n_mem, dm), mem_norm_g,
                            w_ca_k[0].reshape(dm, ca_heads * ca_hd).astype(BF16),
                            w_ca_v[0].reshape(dm, ca_heads * ca_hd).astype(BF16))
    ctab_p, stab_p = _rope_tabs(jnp.arange(seq), rope_dim, nope_dim)
    x2d = x_prompt.reshape(batch * seq, dm)
    q, k, vt, ckv_p, kr_p, mc_p, sg_p, conv_p = _pre_prompt(dims, scale * LOG2_E, x2d, ctab_p, stab_p, w, batch,
                                                           seq, rope_dim, v_dim, dvp)
    o_p = _prompt_attention(q, k, vt, batch, seq, n_heads, v_dim)
    y_p = _post_prompt(x2d, o_p, mc_p, sg_p, mk_p, mv_p, w, batch, seq, ca_heads, n_mem)

    ctab_s, stab_s = _rope_tabs(jnp.full((n_s,), past_len, jnp.int32), rope_dim, nope_dim)
    xs = x_sample.reshape(n_s, dm)
    q_s, qlat_s, ckv_s, kr_s, mc_s, sg_s, conv_s = _pre_sample(
        dims, scale, xs, state_conv.reshape(n_s, (CONV_WIDTH - 1) * conv_dim), ctab_s, stab_s, w)
    olat = _decode_attention(page_table, qlat_s.reshape(n_s * n_heads, kv_rank),
                             q_s.reshape(n_s * n_heads, LANES), ckv_s, kr_s,
                             cache_ckv.reshape(n_pool, page, kv_rank),
                             jnp.swapaxes(cache_krope.reshape(n_pool, page, rope_dim), 1, 2), n_heads, rope_dim)
    y_s = _post_sample(xs, olat, mc_s, sg_s, cache_mem_k.reshape(n_s, n_mem, ca_heads * ca_hd),
                       cache_mem_v.reshape(n_s, n_mem, ca_heads * ca_hd), w, n_heads, v_dim, ca_heads)

    return (y_p.reshape(batch, seq, dm),
            y_s.reshape(n_s, t_s, dm),
            ckv_p.reshape(depth, batch, seq, kv_rank),
            kr_p.reshape(depth, batch, seq, rope_dim),
            conv_p.reshape(depth, batch, CONV_WIDTH - 1, conv_dim),
            mk_p.reshape(depth, batch, n_mem, ca_heads, ca_hd),
            mv_p.reshape(depth, batch, n_mem, ca_heads, ca_hd),
            ckv_s.reshape(depth, n_s, t_s, kv_rank),
            kr_s[:, :rope_dim].reshape(depth, n_s, t_s, rope_dim),
            conv_s.reshape(depth, n_s, CONV_WIDTH - 1, conv_dim))
```

```python
import functools

import numpy as np
import jax
import jax.numpy as jnp
from jax import lax
from jax.experimental import pallas as pl
from jax.experimental.pallas import tpu as pltpu

F32 = jnp.float32
BF16 = jnp.bfloat16

RMS_EPS = 1e-6
NEG_INF = -1e30
ROPE_BASE = 10000.0
CONV_WIDTH = 3
LOG2_E = 1.4426950408889634

LANES = 128
SUBLANES = 8
VMEM_LIMIT_BYTES = 56 * 1024 * 1024

ROW_TILE = 512
ATTN_TQ = 512
ATTN_TK = 256
ATTN_V_EXTRA_ROWS = 16
DEC_PAGES_PER_STEP = 32
DEC_NBUF = 4
CA_SAMPLES_PER_STEP = 4


def _rms(x, g):
    return x * lax.rsqrt(jnp.mean(x * x, axis=-1, keepdims=True) + RMS_EPS) * g


def _bdot(a, b):
    return jnp.dot(a.astype(BF16), b.astype(BF16), preferred_element_type=F32)


def _bdot_nt(a, b):
    return lax.dot_general(a.astype(BF16), b.astype(BF16), (((1,), (1,)), ((), ())),
                           preferred_element_type=F32)


def _softmax_rows(s):
    m = jnp.max(s, axis=-1, keepdims=True)
    p = jnp.exp(s - m)
    return p / jnp.sum(p, axis=-1, keepdims=True)


def _const_spec(shape):
    nd = len(shape)
    return pl.BlockSpec(shape, lambda *_: (0,) * nd, pipeline_mode=pl.Buffered(1))


def _params(*sem):
    return pltpu.CompilerParams(dimension_semantics=tuple(sem) if sem else None,
                                vmem_limit_bytes=VMEM_LIMIT_BYTES)


def _memkv_kernel(mem_ref, g_ref, wk_ref, wv_ref, k_ref, v_ref):
    mn = _rms(mem_ref[...], g_ref[...]).astype(BF16)
    k_ref[...] = jnp.dot(mn, wk_ref[...], preferred_element_type=F32)
    v_ref[...] = jnp.dot(mn, wv_ref[...], preferred_element_type=F32)


def _memory_kv(mem2d, g, wk, wv):
    rows, d = mem2d.shape
    out = jax.ShapeDtypeStruct((rows, wk.shape[1]), F32)
    return pl.pallas_call(_memkv_kernel, out_shape=(out, out), name="mem_kv",
                          compiler_params=_params())(mem2d, g, wk, wv)


class _PreDims:
    def __init__(self, d_model, conv_dim, q_rank, kv_rank, n_heads):
        self.d_model, self.conv_dim, self.q_rank, self.kv_rank = d_model, conv_dim, q_rank, kv_rank
        self.n_heads = n_heads
        c = conv_dim
        self.o_h, self.o_gb, self.o_gc = 0, c, 2 * c
        self.o_cq = 3 * c
        self.o_ckv = self.o_cq + q_rank
        self.o_kra = self.o_ckv + kv_rank
        self.o_krb = self.o_kra + LANES
        self.o_gconv = self.o_krb + LANES
        self.o_gmla = self.o_gconv + d_model
        self.n_in = self.o_gmla + d_model
        self.hw = n_heads * LANES


def _pre_common(dims, x, ctab, stab, gpre_ref, win_ref, qg_ref, wuqa_ref, wuqb_ref, kvg_ref, scale):
    d = dims
    xn = _rms(x, gpre_ref[...]).astype(BF16)

    def proj(lo, hi):
        return jnp.dot(xn, win_ref[:, lo:hi], preferred_element_type=F32)

    h = proj(d.o_h, d.o_gb)
    gate_b = proj(d.o_gb, d.o_gc)
    gate_c = proj(d.o_gc, d.o_cq)
    cq = proj(d.o_cq, d.o_ckv)
    ckv = proj(d.o_ckv, d.o_kra)
    kra = proj(d.o_kra, d.o_krb)
    krb = proj(d.o_krb, d.o_gconv)
    g_conv = proj(d.o_gconv, d.o_gmla)
    g_mla = proj(d.o_gmla, d.n_in)

    u = gate_c * h
    cqn = _rms(cq, qg_ref[...]).astype(BF16)
    qa = jnp.dot(cqn, wuqa_ref[...], preferred_element_type=F32)
    qb = jnp.dot(cqn, wuqb_ref[...], preferred_element_type=F32)
    ct = jnp.concatenate([ctab] * d.n_heads, axis=1)
    st = jnp.concatenate([stab] * d.n_heads, axis=1)
    q = (qa * ct + qb * st) * scale
    ckvn = _rms(ckv, kvg_ref[...])
    krr = kra * ctab + krb * stab
    return u, gate_b, g_conv, g_mla, q, ckvn, krr


def _pre_prompt_kernel(dims, scale, tm, dv, dvp,
                       x_ref, ctab_ref, stab_ref, gpre_ref, win_ref, convw_ref, wco_ref, qg_ref,
                       wuqa_ref, wuqb_ref, kvg_ref, wuk_ref, wuvt_ref,
                       q_out, k_out, vt_out, ckv_out, kr_out, mc_out, sg_out, conv_out, ubuf):
    hist = SUBLANES
    u, gate_b, g_conv, g_mla, q, ckvn, krr = _pre_common(
        dims, x_ref[...], ctab_ref[...], stab_ref[...], gpre_ref, win_ref, qg_ref, wuqa_ref,
        wuqb_ref, kvg_ref, scale)

    @pl.when(pl.program_id(1) == 0)
    def _():
        ubuf[0:hist, :] = jnp.zeros((hist, dims.conv_dim), F32)

    ubuf[hist:hist + tm, :] = u
    u1 = ubuf[hist - 1:hist - 1 + tm, :]
    u2 = ubuf[hist - 2:hist - 2 + tm, :]
    cw = convw_ref[...]
    conv = u2 * cw[0:1, :] + u1 * cw[1:2, :] + u * cw[2:3, :]
    y_conv = _bdot(gate_b * conv, wco_ref[...])
    mc_out[...] = (jax.nn.sigmoid(g_conv) * y_conv).astype(mc_out.dtype)
    sg_out[...] = jax.nn.sigmoid(g_mla).astype(sg_out.dtype)
    conv_out[...] = ubuf[hist + tm - (CONV_WIDTH - 1):hist + tm, :]
    ubuf[0:hist, :] = ubuf[tm:tm + hist, :]

    q_out[...] = q.astype(q_out.dtype)
    ckv_out[...] = ckvn
    kr_out[...] = krr[:, :kr_out.shape[-1]]
    ckvb = ckvn.astype(BF16)
    ka = jnp.dot(ckvb, wuk_ref[...], preferred_element_type=F32)
    k = ka + jnp.concatenate([krr] * dims.n_heads, axis=1)
    k_out[...] = k.astype(k_out.dtype)
    vt = lax.dot_general(wuvt_ref[...], ckvb, (((1,), (1,)), ((), ())), preferred_element_type=F32)
    rid = lax.broadcasted_iota(jnp.int32, (vt.shape[0], 1), 0)
    vt_out[...] = (vt + jnp.where(rid % dvp == dv, 1.0, 0.0)).astype(vt_out.dtype)


def _pre_prompt(dims, scale, x2d, ctab, stab, w, batch, seq, rope_dim, dv, dvp):
    tm = ROW_TILE
    assert seq % tm == 0
    nt = seq // tm
    rows = batch * seq
    d = dims
    hv = w["w_uv_t"].shape[0]

    def row_spec(width):
        return pl.BlockSpec((tm, width), lambda b, t: (b * nt + t, 0))

    def tab_spec():
        return pl.BlockSpec((tm, LANES), lambda b, t: (t, 0))

    in_specs = [row_spec(d.d_model), tab_spec(), tab_spec(),
                _const_spec((1, d.d_model)), _const_spec((d.d_model, d.n_in)),
                _const_spec((CONV_WIDTH, d.conv_dim)), _const_spec((d.conv_dim, d.d_model)),
                _const_spec((1, d.q_rank)), _const_spec((d.q_rank, d.hw)), _const_spec((d.q_rank, d.hw)),
                _const_spec((1, d.kv_rank)), _const_spec((d.kv_rank, d.hw)), _const_spec((hv, d.kv_rank))]
    out_shape = (jax.ShapeDtypeStruct((rows, d.hw), BF16),
                 jax.ShapeDtypeStruct((rows, d.hw), BF16),
                 jax.ShapeDtypeStruct((batch, hv, seq), BF16),
                 jax.ShapeDtypeStruct((rows, d.kv_rank), F32),
                 jax.ShapeDtypeStruct((rows, rope_dim), F32),
                 jax.ShapeDtypeStruct((rows, d.d_model), BF16),
                 jax.ShapeDtypeStruct((rows, d.d_model), BF16),
                 jax.ShapeDtypeStruct((batch, CONV_WIDTH - 1, d.conv_dim), F32))
    out_specs = (row_spec(d.hw), row_spec(d.hw),
                 pl.BlockSpec((None, hv, tm), lambda b, t: (b, 0, t)), row_spec(d.kv_rank),
                 row_spec(rope_dim), row_spec(d.d_model), row_spec(d.d_model),
                 pl.BlockSpec((None, CONV_WIDTH - 1, d.conv_dim), lambda b, t: (b, 0, 0)))
    kern = functools.partial(_pre_prompt_kernel, dims, scale, tm, dv, dvp)
    return pl.pallas_call(
        kern, grid=(batch, nt), in_specs=in_specs, out_specs=out_specs, out_shape=out_shape,
        scratch_shapes=[pltpu.VMEM((tm + SUBLANES, d.conv_dim), F32)],
        compiler_params=_params("arbitrary", "arbitrary"), name="pre_prompt",
    )(x2d, ctab, stab, w["g_pre"], w["w_in"], w["conv_w"], w["w_conv_out"], w["q_g"],
      w["w_uq_a"], w["w_uq_b"], w["kv_g"], w["w_uk_pad"], w["w_uv_t"])


def _pre_sample_kernel(dims, scale,
                       x_ref, state_ref, ctab_ref, stab_ref, gpre_ref, win_ref, convw_ref, wco_ref,
                       qg_ref, wuqa_ref, wuqb_ref, kvg_ref, wukt_ref,
                       q_out, qlat_out, ckv_out, kr_out, mc_out, sg_out, conv_out):
    c = dims.conv_dim
    u, gate_b, g_conv, g_mla, q, ckvn, krr = _pre_common(
        dims, x_ref[...], ctab_ref[...], stab_ref[...], gpre_ref, win_ref, qg_ref, wuqa_ref,
        wuqb_ref, kvg_ref, scale)
    u2 = state_ref[:, 0:c]
    u1 = state_ref[:, c:2 * c]
    cw = convw_ref[...]
    conv = u2 * cw[0:1, :] + u1 * cw[1:2, :] + u * cw[2:3, :]
    y_conv = _bdot(gate_b * conv, wco_ref[...])
    mc_out[...] = jax.nn.sigmoid(g_conv) * y_conv
    sg_out[...] = jax.nn.sigmoid(g_mla)
    conv_out[:, 0:c] = u1
    conv_out[:, c:2 * c] = u
    q_out[...] = q
    ckv_out[...] = ckvn
    kr_out[...] = krr
    kvr = dims.kv_rank
    qb16 = q.astype(BF16)
    for hd in range(dims.n_heads):
        blk = qb16[:, hd * LANES:(hd + 1) * LANES]
        qlat_out[:, hd * kvr:(hd + 1) * kvr] = jnp.dot(
            blk, wukt_ref[hd * LANES:(hd + 1) * LANES, :], preferred_element_type=F32)


def _pre_sample(dims, scale, xs, state2d, ctab, stab, w):
    n = xs.shape[0]
    d = dims
    out_shape = (jax.ShapeDtypeStruct((n, d.hw), F32),
                 jax.ShapeDtypeStruct((n, d.n_heads * d.kv_rank), F32),
                 jax.ShapeDtypeStruct((n, d.kv_rank), F32),
                 jax.ShapeDtypeStruct((n, LANES), F32),
                 jax.ShapeDtypeStruct((n, d.d_model), F32),
                 jax.ShapeDtypeStruct((n, d.d_model), F32),
                 jax.ShapeDtypeStruct((n, (CONV_WIDTH - 1) * d.conv_dim), F32))
    kern = functools.partial(_pre_sample_kernel, dims, scale)
    return pl.pallas_call(kern, out_shape=out_shape, compiler_params=_params(), name="pre_sample")(
        xs, state2d, ctab, stab, w["g_pre"], w["w_in"], w["conv_w"], w["w_conv_out"], w["q_g"],
        w["w_uq_a"], w["w_uq_b"], w["kv_g"], w["w_uk_t_pad"])


def _attn_kernel(tq, tk, hp, dv, dvp, q_ref, k_ref, vt_ref, o_ref, sa_ref, sb_ref):
    qi = pl.program_id(2)

    def scores(j, dst):
        start = pl.multiple_of(j * tk, tk)
        for h in range(hp):
            kh = k_ref[pl.ds(start, tk), h * LANES:(h + 1) * LANES]
            qh = q_ref[:, h * LANES:(h + 1) * LANES]
            dst[h] = lax.dot_general(kh, qh, (((1,), (1,)), ((), ())), preferred_element_type=F32)

    def process(j, src, state, masked):
        start = pl.multiple_of(j * tk, tk)
        new_state = []
        for h in range(hp):
            m_prev, acc = state[h]
            st = src[h]
            if masked:
                key = lax.broadcasted_iota(jnp.int32, (tk, tq), 0) + j * tk
                qry = lax.broadcasted_iota(jnp.int32, (tk, tq), 1) + qi * tq
                st = jnp.where(key <= qry, st, NEG_INF)
            m_new = jnp.maximum(m_prev, jnp.max(st, axis=0, keepdims=True))
            alpha = jnp.exp2(m_prev - m_new)
            p = jnp.exp2(st - m_new).astype(BF16)
            vth = vt_ref[h * dvp:(h + 1) * dvp, pl.ds(start, tk)]
            acc = alpha * acc + jnp.dot(vth, p, preferred_element_type=F32)
            new_state.append((m_new, acc))
        return tuple(new_state)

    assert tq == 2 * tk
    init = tuple((jnp.full((1, tq), NEG_INF, F32), jnp.zeros((dvp, tq), F32)) for _ in range(hp))
    scores(0, sa_ref)

    def pair(t, state):
        scores(2 * t + 1, sb_ref)
        state = process(2 * t, sa_ref, state, False)
        scores(2 * t + 2, sa_ref)
        return process(2 * t + 1, sb_ref, state, False)

    state = lax.fori_loop(0, qi, pair, init)
    scores(2 * qi + 1, sb_ref)
    state = process(2 * qi, sa_ref, state, True)
    state = process(2 * qi + 1, sb_ref, state, True)
    ot = jnp.concatenate([acc[0:dv] / acc[dv:dv + 1] for (_, acc) in state], axis=0)
    o_ref[...] = ot.T.astype(o_ref.dtype)


def _prompt_attention(q, k, vt, batch, seq, n_heads, dv):
    tq, tk = ATTN_TQ, ATTN_TK
    assert seq % tq == 0 and tq % tk == 0
    hp = LANES // dv
    assert n_heads % hp == 0
    dvp = vt.shape[1] // n_heads
    nq = seq // tq
    kern = functools.partial(_attn_kernel, tq, tk, hp, dv, dvp)
    return pl.pallas_call(
        kern, grid=(batch, n_heads // hp, nq),
        in_specs=[pl.BlockSpec((tq, hp * LANES), lambda b, g, i: (b * nq + i, g)),
                  pl.BlockSpec((seq, hp * LANES), lambda b, g, i: (b, g)),
                  pl.BlockSpec((None, hp * dvp, seq), lambda b, g, i: (b, g, 0))],
        out_specs=pl.BlockSpec((tq, hp * dv), lambda b, g, i: (b * nq + i, g)),
        out_shape=jax.ShapeDtypeStruct((batch * seq, n_heads * dv), BF16),
        scratch_shapes=[pltpu.VMEM((hp, tk, tq), F32), pltpu.VMEM((hp, tk, tq), F32)],
        compiler_params=_params("arbitrary", "arbitrary", "arbitrary"), name="prompt_attn",
    )(q, k, vt)


def _mix_and_query(x, o, mc, sg, wmla_ref, wmix_ref, gmixpost_ref, gcapre_ref, wcaq_ref, ca_scale):
    y_mla = _bdot(o, wmla_ref[...])
    merged = mc.astype(F32) + sg.astype(F32) * y_mla
    y = _bdot(merged, wmix_ref[...])
    x1 = x + _rms(y, gmixpost_ref[...])
    qc = _bdot(_rms(x1, gcapre_ref[...]), wcaq_ref[...]) * ca_scale
    return x1, qc


def _mlp_tail(x1, oc, wcao_ref, gcapost_ref, gmlppre_ref, wup_ref, wdown_ref, gmlppost_ref):
    ca = _bdot(oc, wcao_ref[...])
    x2 = x1 + _rms(ca, gcapost_ref[...])
    hid = jnp.square(jnp.maximum(_bdot(_rms(x2, gmlppre_ref[...]), wup_ref[...]), 0.0))
    return x2 + _rms(_bdot(hid, wdown_ref[...]), gmlppost_ref[...])


def _post_prompt_kernel(ca_heads, ca_scale,
                        x_ref, o_ref, mc_ref, sg_ref, mk_ref, mv_ref,
                        wmla_ref, wmix_ref, gmixpost_ref, gcapre_ref, wcaq_ref, wcao_ref, gcapost_ref,
                        gmlppre_ref, wup_ref, wdown_ref, gmlppost_ref, y_ref):
    x1, qc = _mix_and_query(x_ref[...], o_ref[...], mc_ref[...], sg_ref[...], wmla_ref, wmix_ref,
                            gmixpost_ref, gcapre_ref, wcaq_ref, ca_scale)
    hd = qc.shape[1] // ca_heads
    outs = []
    for h in range(ca_heads):
        sl = slice(h * hd, (h + 1) * hd)
        p = _softmax_rows(_bdot_nt(qc[:, sl], mk_ref[:, sl]))
        outs.append(_bdot(p, mv_ref[:, sl]))
    oc = jnp.concatenate(outs, axis=1)
    y_ref[...] = _mlp_tail(x1, oc, wcao_ref, gcapost_ref, gmlppre_ref, wup_ref, wdown_ref, gmlppost_ref)


def _post_prompt(x2d, o, mc, sg, mk, mv, w, batch, seq, ca_heads, n_mem):
    tm = ROW_TILE
    nt = seq // tm
    rows, dm = x2d.shape
    dff = w["w_ff_up"].shape[1]
    ca_scale = float((dm // ca_heads) ** -0.5)

    def row_spec(width):
        return pl.BlockSpec((tm, width), lambda i: (i, 0))

    mem_spec = pl.BlockSpec((n_mem, dm), lambda i: (i // nt, 0))
    in_specs = [row_spec(dm), row_spec(o.shape[1]), row_spec(dm), row_spec(dm), mem_spec, mem_spec,
                _const_spec((o.shape[1], dm)), _const_spec((dm, dm)), _const_spec((1, dm)),
                _const_spec((1, dm)), _const_spec((dm, dm)), _const_spec((dm, dm)), _const_spec((1, dm)),
                _const_spec((1, dm)), _const_spec((dm, dff)), _const_spec((dff, dm)), _const_spec((1, dm))]
    kern = functools.partial(_post_prompt_kernel, ca_heads, ca_scale)
    return pl.pallas_call(
        kern, grid=(rows // tm,), in_specs=in_specs, out_specs=row_spec(dm),
        out_shape=jax.ShapeDtypeStruct((rows, dm), F32),
        compiler_params=_params("arbitrary"), name="post_prompt",
    )(x2d, o, mc, sg, mk, mv, w["w_mla"], w["w_mix"], w["g_mix_post"], w["g_ca_pre"], w["w_ca_q"],
      w["w_ca_o"], w["g_ca_post"], w["g_mlp_pre"], w["w_ff_up"], w["w_ff_down"], w["g_mlp_post"])


def _post_sample_a_kernel(n_heads, v_dim, ca_scale,
                          x_ref, olat_ref, mc_ref, sg_ref, wuv_ref, wmla_ref, wmix_ref, gmixpost_ref,
                          gcapre_ref, wcaq_ref, x1_ref, qc_ref):
    n = x_ref.shape[0]
    full = _bdot(olat_ref[...], wuv_ref[...])
    full = full.reshape(n, n_heads, n_heads * v_dim)
    hidx = lax.broadcasted_iota(jnp.int32, full.shape, 1)
    lane_head = lax.broadcasted_iota(jnp.int32, full.shape, 2) // v_dim
    o = jnp.sum(jnp.where(hidx == lane_head, full, 0.0), axis=1)
    x1, qc = _mix_and_query(x_ref[...], o, mc_ref[...], sg_ref[...], wmla_ref, wmix_ref,
                            gmixpost_ref, gcapre_ref, wcaq_ref, ca_scale)
    x1_ref[...] = x1
    qc_ref[...] = qc


def _post_sample_b_kernel(ca_heads, q_ref, mk_ref, mv_ref, o_ref):
    g = q_ref.shape[0]
    rows = mk_ref.shape[1]
    s8 = 2 * ca_heads
    assert s8 == SUBLANES and mk_ref.shape[2] == LANES
    lane = lax.broadcasted_iota(jnp.int32, (s8, rows), 1)
    sub = lax.broadcasted_iota(jnp.int32, (s8, rows), 0)
    own = (lane % s8) == sub
    low_half = (lax.broadcasted_iota(jnp.int32, (1, rows), 1) % s8) < ca_heads
    for i in range(g):
        q = q_ref[i]
        q8 = jnp.concatenate([q[:, (2 * (j % ca_heads) + j // ca_heads) * LANES:
                                   (2 * (j % ca_heads) + j // ca_heads + 1) * LANES] for j in range(s8)], axis=0)
        kb = mk_ref[i].astype(BF16)
        p_all = lax.dot_general(q8.astype(BF16), kb, (((1,), (1,)), ((), ())), preferred_element_type=F32)
        d = jnp.sum(jnp.where(own, p_all, 0.0), axis=0, keepdims=True)
        d = d + jnp.where(low_half, pltpu.roll(d, rows - ca_heads, 1), pltpu.roll(d, ca_heads, 1))
        dm = jnp.where(own, jnp.broadcast_to(d, (s8, rows)), NEG_INF)
        e = jnp.exp(dm - jnp.max(dm, axis=-1, keepdims=True))
        w = (e / jnp.sum(e, axis=-1, keepdims=True)).astype(BF16)
        o8 = jnp.dot(w, mv_ref[i].astype(BF16), preferred_element_type=F32)
        for j in range(s8):
            blk = 2 * (j % ca_heads) + j // ca_heads
            o_ref[i, :, blk * LANES:(blk + 1) * LANES] = o8[j:j + 1, :]


def _post_sample_c_kernel(x1_ref, oc_ref, wcao_ref, gcapost_ref, gmlppre_ref, wup_ref, wdown_ref,
                          gmlppost_ref, y_ref):
    y_ref[...] = _mlp_tail(x1_ref[...], oc_ref[...], wcao_ref, gcapost_ref, gmlppre_ref, wup_ref,
                           wdown_ref, gmlppost_ref)


def _post_sample(xs, olat, mc, sg, mem_k, mem_v, w, n_heads, v_dim, ca_heads):
    n, dm = xs.shape
    ca_scale = float((dm // ca_heads) ** -0.5)
    kern_a = functools.partial(_post_sample_a_kernel, n_heads, v_dim, ca_scale)
    x1, qc = pl.pallas_call(
        kern_a, out_shape=(jax.ShapeDtypeStruct((n, dm), F32), jax.ShapeDtypeStruct((n, dm), F32)),
        compiler_params=_params(), name="post_sample_a",
    )(xs, olat, mc, sg, w["w_uv_flat"], w["w_mla"], w["w_mix"], w["g_mix_post"], w["g_ca_pre"], w["w_ca_q"])

    g = CA_SAMPLES_PER_STEP
    assert n % g == 0
    kern_b = functools.partial(_post_sample_b_kernel, ca_heads)
    mem_spec = pl.BlockSpec((g,) + mem_k.shape[1:], lambda i: (i, 0, 0))
    oc = pl.pallas_call(
        kern_b, grid=(n // g,),
        in_specs=[pl.BlockSpec((g, 1, dm), lambda i: (i, 0, 0)), mem_spec, mem_spec],
        out_specs=pl.BlockSpec((g, 1, dm), lambda i: (i, 0, 0)),
        out_shape=jax.ShapeDtypeStruct((n, 1, dm), F32),
        compiler_params=_params("arbitrary"), name="post_sample_b",
    )(qc.reshape(n, 1, dm), mem_k, mem_v)

    return pl.pallas_call(
        _post_sample_c_kernel, out_shape=jax.ShapeDtypeStruct((n, dm), F32),
        compiler_params=_params(), name="post_sample_c",
    )(x1, oc.reshape(n, dm), w["w_ca_o"], w["g_ca_post"], w["g_mlp_pre"], w["w_ff_up"], w["w_ff_down"],
      w["g_mlp_post"])


def _dec_attn_kernel(n_samples, n_pages, n_heads, rope_dim,
                     pt_ref, qlat_ref, qrope_ref, ckvn_ref, krn_ref, cache_ckv, cache_krt,
                     o_ref, cbuf, kbuf, sems):
    ch = DEC_PAGES_PER_STEP
    nch = n_pages // ch
    total = n_samples * nch
    page, kvr = cbuf.shape[2], cbuf.shape[3]

    def copies(g, slot):
        b = g // nch
        c = g % nch
        out = []
        for i in range(ch):
            pg = pt_ref[b, c * ch + i]
            out.append(pltpu.make_async_copy(cache_ckv.at[pg], cbuf.at[slot, i], sems.at[0, slot]))
            out.append(pltpu.make_async_copy(cache_krt.at[pg], kbuf.at[slot, i], sems.at[1, slot]))
        return out

    def start(g):
        for cp in copies(g, g % DEC_NBUF):
            cp.start()

    def wait(g):
        for cp in copies(g, g % DEC_NBUF):
            cp.wait()

    def q_rows(b):
        row0 = pl.multiple_of(b * n_heads, n_heads)
        return qlat_ref[pl.ds(row0, n_heads), :], qrope_ref[pl.ds(row0, n_heads), :][:, :rope_dim]

    def scores(g):
        ql, qr = q_rows(g // nch)
        slot = g % DEC_NBUF
        ckv = cbuf[slot].reshape(ch * page, kvr).astype(BF16)
        qrb = qr.astype(BF16)
        s_rope = jnp.concatenate(
            [jnp.dot(qrb, kbuf[slot, i].astype(BF16), preferred_element_type=F32) for i in range(ch)], axis=1)
        return lax.dot_general(ql.astype(BF16), ckv, (((1,), (1,)), ((), ())),
                               preferred_element_type=F32) + s_rope

    for g0 in range(DEC_NBUF - 1):
        start(g0)
    wait(0)
    s0 = scores(0)

    def body(g, carry):
        s_cur, m, l, acc = carry
        b = g // nch
        c = g % nch

        @pl.when(g + 1 < total)
        def _():
            wait(g + 1)

        @pl.when(g + (DEC_NBUF - 1) < total)
        def _():
            start(g + (DEC_NBUF - 1))

        s_next = scores(jnp.minimum(g + 1, total - 1))

        ql, qr = q_rows(b)
        c_new = ckvn_ref[pl.ds(b, 1), :]
        r_new = krn_ref[pl.ds(b, 1), :][:, :rope_dim]
        s_new = jnp.sum(ql * c_new, axis=-1, keepdims=True) + jnp.sum(qr * r_new, axis=-1, keepdims=True)
        first = c == 0
        m = jnp.where(first, s_new, m)
        l = jnp.where(first, 1.0, l)
        acc = jnp.where(first, jnp.broadcast_to(c_new, acc.shape), acc)

        ckv = cbuf[g % DEC_NBUF].reshape(ch * page, kvr).astype(BF16)
        m_new = jnp.maximum(m, jnp.max(s_cur, axis=-1, keepdims=True))
        alpha = jnp.exp(m - m_new)
        p = jnp.exp(s_cur - m_new)
        l = alpha * l + jnp.sum(p, axis=-1, keepdims=True)
        acc = alpha * acc + jnp.dot(p.astype(BF16), ckv, preferred_element_type=F32)
        o_ref[pl.ds(pl.multiple_of(b * n_heads, n_heads), n_heads), :] = acc / l
        return s_next, m_new, l, acc

    init = (s0, jnp.zeros((n_heads, 1), F32), jnp.zeros((n_heads, 1), F32), jnp.zeros((n_heads, kvr), F32))
    lax.fori_loop(0, total, body, init)


def _decode_attention(page_table, qlat, qrope, ckv_new, kr_new, cache_ckv, cache_krt, n_heads, rope_dim):
    n_samples, n_pages = page_table.shape
    page, kvr = cache_ckv.shape[1], cache_ckv.shape[2]
    ch = DEC_PAGES_PER_STEP
    assert n_pages % ch == 0

    def whole(shape):
        nd = len(shape)
        return pl.BlockSpec(shape, lambda i, pt: (0,) * nd)

    kern = functools.partial(_dec_attn_kernel, n_samples, n_pages, n_heads, rope_dim)
    grid_spec = pltpu.PrefetchScalarGridSpec(
        num_scalar_prefetch=1, grid=(1,),
        in_specs=[whole(qlat.shape), whole(qrope.shape), whole(ckv_new.shape), whole(kr_new.shape),
                  pl.BlockSpec(memory_space=pl.ANY), pl.BlockSpec(memory_space=pl.ANY)],
        out_specs=whole(qlat.shape),
        scratch_shapes=[pltpu.VMEM((DEC_NBUF, ch, page, kvr), F32),
                        pltpu.VMEM((DEC_NBUF, ch, rope_dim, page), F32),
                        pltpu.SemaphoreType.DMA((2, DEC_NBUF))])
    return pl.pallas_call(
        kern, grid_spec=grid_spec, out_shape=jax.ShapeDtypeStruct(qlat.shape, F32),
        compiler_params=_params("arbitrary"), name="decode_attn",
    )(page_table, qlat, qrope, ckv_new, kr_new, cache_ckv, cache_krt)


def _rot_half(w):
    half = w.shape[-1] // 2
    return jnp.concatenate([-w[..., half:], w[..., :half]], axis=-1)


def _mem_rows(cache):
    _, n, n_mem, heads, hd = cache.shape
    assert hd == 2 * LANES
    x = cache.reshape(n, n_mem, heads, 2, LANES)
    return jnp.transpose(x, (0, 1, 3, 2, 4)).reshape(n, n_mem * 2 * heads, LANES)


def _rope_tabs(pos, rope_dim, nope_dim):
    inv = 1.0 / (ROPE_BASE ** (jnp.arange(0, rope_dim, 2, dtype=F32) / rope_dim))
    ang = pos.astype(F32)[:, None] * inv[None, :]
    cos, sin = jnp.cos(ang), jnp.sin(ang)
    n = pos.shape[0]
    pad = LANES - rope_dim - nope_dim
    ctab = jnp.concatenate([cos, cos, jnp.ones((n, nope_dim), F32), jnp.zeros((n, pad), F32)], axis=1)
    stab = jnp.concatenate([sin, sin, jnp.zeros((n, LANES - rope_dim), F32)], axis=1)
    return ctab, stab


def kernel(x_prompt, x_sample, mem_prompt, cache_ckv, cache_krope, state_conv, cache_mem_k, cache_mem_v,
           page_table, norm_mix_pre_g, w_in, conv_w, w_conv_out, q_norm_g, w_uq, kv_norm_g, w_uk, w_uv,
           w_mla_out, w_mix_out, norm_mix_post_g, norm_ca_pre_g, mem_norm_g, w_ca_q, w_ca_k, w_ca_v, w_ca_o,
           norm_ca_post_g, norm_mlp_pre_g, w_ff_up, w_ff_down, norm_mlp_post_g):
    depth = w_in.shape[0]
    assert depth == 1, "single-layer step"
    batch, seq, dm = x_prompt.shape
    n_s, t_s, _ = x_sample.shape
    assert t_s == 1
    conv_dim = conv_w.shape[2]
    q_rank, n_heads, qk_dim = w_uq.shape[1:]
    kv_rank, _, nope_dim = w_uk.shape[1:]
    v_dim = w_uv.shape[3]
    rope_dim = qk_dim - nope_dim
    n_mem, ca_heads, ca_hd = cache_mem_k.shape[2:]
    n_pool, page = cache_ckv.shape[1:3]
    past_len = page_table.shape[1] * page
    assert rope_dim + nope_dim <= LANES and v_dim <= LANES
    dims = _PreDims(dm, conv_dim, q_rank, kv_rank, n_heads)
    scale = float(qk_dim ** -0.5)

    w0 = w_in[0]
    o_kr = 3 * conv_dim + q_rank + kv_rank
    w_kr = w0[:, o_kr:o_kr + rope_dim]
    zpad = jnp.zeros((dm, LANES - rope_dim), F32)
    w_in_p = jnp.concatenate([w0[:, :o_kr], w_kr, zpad, _rot_half(w_kr), zpad, w0[:, o_kr + rope_dim:]],
                             axis=1).astype(BF16)
    uq = w_uq[0]
    uq_nope, uq_rope = uq[:, :, :nope_dim], uq[:, :, nope_dim:]
    hz = lambda r, width: jnp.zeros((r, n_heads, width), F32)
    pad_tail = LANES - rope_dim - nope_dim
    w_uq_a = jnp.concatenate([uq_rope, uq_nope, hz(q_rank, pad_tail)], axis=2).reshape(q_rank, n_heads * LANES)
    w_uq_b = jnp.concatenate([_rot_half(uq_rope), hz(q_rank, LANES - rope_dim)], axis=2).reshape(
        q_rank, n_heads * LANES)
    uk = w_uk[0]
    w_uk_pad = jnp.concatenate([hz(kv_rank, rope_dim), uk, hz(kv_rank, pad_tail)], axis=2).reshape(
        kv_rank, n_heads * LANES)
    w_uk_t_pad = jnp.transpose(w_uk_pad.reshape(kv_rank, n_heads, LANES), (1, 2, 0)).reshape(
        n_heads * LANES, kv_rank)
    uv = w_uv[0].reshape(kv_rank, n_heads * v_dim)
    dvp = v_dim + ATTN_V_EXTRA_ROWS
    w_uv_t = jnp.concatenate([jnp.transpose(w_uv[0], (1, 2, 0)),
                              jnp.zeros((n_heads, ATTN_V_EXTRA_ROWS, kv_rank), F32)], axis=1).reshape(
        n_heads * dvp, kv_rank)
    w_mla = w_mla_out[0]
    w = {
        "g_pre": norm_mix_pre_g, "w_in": w_in_p, "conv_w": conv_w[0], "w_conv_out": w_conv_out[0].astype(BF16),
        "q_g": q_norm_g, "w_uq_a": w_uq_a.astype(BF16), "w_uq_b": w_uq_b.astype(BF16), "kv_g": kv_norm_g,
        "w_uk_pad": w_uk_pad.astype(BF16), "w_uk_t_pad": w_uk_t_pad.astype(BF16),
        "w_uv_t": w_uv_t.astype(BF16), "w_uv_flat": uv.astype(BF16),
        "w_mla": w_mla.astype(BF16), "w_mix": w_mix_out[0].astype(BF16),
        "g_mix_post": norm_mix_post_g, "g_ca_pre": norm_ca_pre_g,
        "w_ca_q": w_ca_q[0].reshape(dm, ca_heads * ca_hd).astype(BF16),
        "w_ca_o": w_ca_o[0].reshape(ca_heads * ca_hd, dm).astype(BF16),
        "g_ca_post": norm_ca_post_g, "g_mlp_pre": norm_mlp_pre_g, "w_ff_up": w_ff_up[0].astype(BF16),
        "w_ff_down": w_ff_down[0].astype(BF16), "g_mlp_post": norm_mlp_post_g,
    }

    mk_p, mv_p = _memory_kv(mem_prompt.reshape(batch * n_mem, dm), mem_norm_g,
                            w_ca_k[0].reshape(dm, ca_heads * ca_hd).astype(BF16),
                            w_ca_v[0].reshape(dm, ca_heads * ca_hd).astype(BF16))
    ctab_p, stab_p = _rope_tabs(jnp.arange(seq), rope_dim, nope_dim)
    x2d = x_prompt.reshape(batch * seq, dm)
    q, k, vt, ckv_p, kr_p, mc_p, sg_p, conv_p = _pre_prompt(dims, scale * LOG2_E, x2d, ctab_p, stab_p, w, batch,
                                                           seq, rope_dim, v_dim, dvp)
    o_p = _prompt_attention(q, k, vt, batch, seq, n_heads, v_dim)
    y_p = _post_prompt(x2d, o_p, mc_p, sg_p, mk_p, mv_p, w, batch, seq, ca_heads, n_mem)

    ctab_s, stab_s = _rope_tabs(jnp.full((n_s,), past_len, jnp.int32), rope_dim, nope_dim)
    xs = x_sample.reshape(n_s, dm)
    q_s, qlat_s, ckv_s, kr_s, mc_s, sg_s, conv_s = _pre_sample(
        dims, scale, xs, state_conv.reshape(n_s, (CONV_WIDTH - 1) * conv_dim), ctab_s, stab_s, w)
    olat = _decode_attention(page_table, qlat_s.reshape(n_s * n_heads, kv_rank),
                             q_s.reshape(n_s * n_heads, LANES), ckv_s, kr_s,
                             cache_ckv.reshape(n_pool, page, kv_rank),
                             jnp.swapaxes(cache_krope.reshape(n_pool, page, rope_dim), 1, 2), n_heads, rope_dim)
    y_s = _post_sample(xs, olat, mc_s, sg_s, _mem_rows(cache_mem_k), _mem_rows(cache_mem_v), w, n_heads, v_dim,
                       ca_heads)

    return (y_p.reshape(batch, seq, dm),
            y_s.reshape(n_s, t_s, dm),
            ckv_p.reshape(depth, batch, seq, kv_rank),
            kr_p.reshape(depth, batch, seq, rope_dim),
            conv_p.reshape(depth, batch, CONV_WIDTH - 1, conv_dim),
            mk_p.reshape(depth, batch, n_mem, ca_heads, ca_hd),
            mv_p.reshape(depth, batch, n_mem, ca_heads, ca_hd),
            ckv_s.reshape(depth, n_s, t_s, kv_rank),
            kr_s[:, :rope_dim].reshape(depth, n_s, t_s, rope_dim),
            conv_s.reshape(depth, n_s, CONV_WIDTH - 1, conv_dim))
```

```python
import functools

import numpy as np
import jax
import jax.numpy as jnp
from jax import lax
from jax.experimental import pallas as pl
from jax.experimental.pallas import tpu as pltpu

F32 = jnp.float32
BF16 = jnp.bfloat16

RMS_EPS = 1e-6
NEG_INF = -1e30
ROPE_BASE = 10000.0
CONV_WIDTH = 3
LOG2_E = 1.4426950408889634

LANES = 128
SUBLANES = 8
VMEM_LIMIT_BYTES = 56 * 1024 * 1024

ROW_TILE = 512
ATTN_TQ = 512
ATTN_TK = 256
ATTN_V_EXTRA_ROWS = 16
DEC_PAGES_PER_STEP = 32
DEC_NBUF = 4
CA_SAMPLES_PER_STEP = 4


def _rms(x, g):
    return x * lax.rsqrt(jnp.mean(x * x, axis=-1, keepdims=True) + RMS_EPS) * g


def _bdot(a, b):
    return jnp.dot(a.astype(BF16), b.astype(BF16), preferred_element_type=F32)


def _bdot_nt(a, b):
    return lax.dot_general(a.astype(BF16), b.astype(BF16), (((1,), (1,)), ((), ())),
                           preferred_element_type=F32)


def _softmax_rows(s):
    m = jnp.max(s, axis=-1, keepdims=True)
    p = jnp.exp(s - m)
    return p / jnp.sum(p, axis=-1, keepdims=True)


def _const_spec(shape):
    nd = len(shape)
    return pl.BlockSpec(shape, lambda *_: (0,) * nd, pipeline_mode=pl.Buffered(1))


def _params(*sem):
    return pltpu.CompilerParams(dimension_semantics=tuple(sem) if sem else None,
                                vmem_limit_bytes=VMEM_LIMIT_BYTES)


def _memkv_kernel(mem_ref, g_ref, wk_ref, wv_ref, k_ref, v_ref):
    mn = _rms(mem_ref[...], g_ref[...]).astype(BF16)
    k_ref[...] = jnp.dot(mn, wk_ref[...], preferred_element_type=F32)
    v_ref[...] = jnp.dot(mn, wv_ref[...], preferred_element_type=F32)


def _memory_kv(mem2d, g, wk, wv):
    rows, d = mem2d.shape
    out = jax.ShapeDtypeStruct((rows, wk.shape[1]), F32)
    return pl.pallas_call(_memkv_kernel, out_shape=(out, out), name="mem_kv",
                          compiler_params=_params())(mem2d, g, wk, wv)


class _PreDims:
    def __init__(self, d_model, conv_dim, q_rank, kv_rank, n_heads):
        self.d_model, self.conv_dim, self.q_rank, self.kv_rank = d_model, conv_dim, q_rank, kv_rank
        self.n_heads = n_heads
        c = conv_dim
        self.o_h, self.o_gb, self.o_gc = 0, c, 2 * c
        self.o_cq = 3 * c
        self.o_ckv = self.o_cq + q_rank
        self.n_a = self.o_ckv + kv_rank
        self.hw = n_heads * LANES


def _pre_common(dims, x, ctab, stab, gpre_ref, wa_ref, wkr_ref, wg_ref, qg_ref, wuqa_ref, wuqb_ref, kvg_ref,
                scale):
    d = dims
    xn = _rms(x, gpre_ref[...]).astype(BF16)

    def proj(w_ref, lo, hi):
        return jnp.dot(xn, w_ref[:, lo:hi], preferred_element_type=F32)

    h = proj(wa_ref, d.o_h, d.o_gb)
    gate_b = proj(wa_ref, d.o_gb, d.o_gc)
    gate_c = proj(wa_ref, d.o_gc, d.o_cq)
    cq = proj(wa_ref, d.o_cq, d.o_ckv)
    ckv = proj(wa_ref, d.o_ckv, d.n_a)
    kr2 = proj(wkr_ref, 0, 2 * LANES)
    kra, krb = kr2[:, :LANES], kr2[:, LANES:]
    g_conv = proj(wg_ref, 0, d.d_model)
    g_mla = proj(wg_ref, d.d_model, 2 * d.d_model)

    u = gate_c * h
    cqn = _rms(cq, qg_ref[...]).astype(BF16)
    qa = jnp.dot(cqn, wuqa_ref[...], preferred_element_type=F32)
    qb = jnp.dot(cqn, wuqb_ref[...], preferred_element_type=F32)
    ct = jnp.concatenate([ctab] * d.n_heads, axis=1)
    st = jnp.concatenate([stab] * d.n_heads, axis=1)
    q = (qa * ct + qb * st) * scale
    ckvn = _rms(ckv, kvg_ref[...])
    krr = kra * ctab + krb * stab
    return u, gate_b, g_conv, g_mla, q, ckvn, krr


def _pre_prompt_kernel(dims, scale, tm, dv, dvp,
                       x_ref, ctab_ref, stab_ref, gpre_ref, wa_ref, wkr_ref, wg_ref, convw_ref, wco_ref, qg_ref,
                       wuqa_ref, wuqb_ref, kvg_ref, wuk_ref, wuvt_ref,
                       q_out, k_out, vt_out, ckv_out, kr_out, mc_out, sg_out, conv_out, ubuf):
    hist = SUBLANES
    u, gate_b, g_conv, g_mla, q, ckvn, krr = _pre_common(
        dims, x_ref[...], ctab_ref[...], stab_ref[...], gpre_ref, wa_ref, wkr_ref, wg_ref, qg_ref, wuqa_ref,
        wuqb_ref, kvg_ref, scale)

    @pl.when(pl.program_id(1) == 0)
    def _():
        ubuf[0:hist, :] = jnp.zeros((hist, dims.conv_dim), F32)

    ubuf[hist:hist + tm, :] = u
    u1 = ubuf[hist - 1:hist - 1 + tm, :]
    u2 = ubuf[hist - 2:hist - 2 + tm, :]
    cw = convw_ref[...]
    conv = u2 * cw[0:1, :] + u1 * cw[1:2, :] + u * cw[2:3, :]
    y_conv = _bdot(gate_b * conv, wco_ref[...])
    mc_out[...] = (jax.nn.sigmoid(g_conv) * y_conv).astype(mc_out.dtype)
    sg_out[...] = jax.nn.sigmoid(g_mla).astype(sg_out.dtype)
    conv_out[...] = ubuf[hist + tm - (CONV_WIDTH - 1):hist + tm, :]
    ubuf[0:hist, :] = ubuf[tm:tm + hist, :]

    q_out[...] = q.astype(q_out.dtype)
    ckv_out[...] = ckvn
    kr_out[...] = krr[:, :kr_out.shape[-1]]
    ckvb = ckvn.astype(BF16)
    ka = jnp.dot(ckvb, wuk_ref[...], preferred_element_type=F32)
    k = ka + jnp.concatenate([krr] * dims.n_heads, axis=1)
    k_out[...] = k.astype(k_out.dtype)
    vt = lax.dot_general(wuvt_ref[...], ckvb, (((1,), (1,)), ((), ())), preferred_element_type=F32)
    rid = lax.broadcasted_iota(jnp.int32, (vt.shape[0], 1), 0)
    vt_out[...] = (vt + jnp.where(rid % dvp == dv, 1.0, 0.0)).astype(vt_out.dtype)


def _pre_prompt(dims, scale, x2d, ctab, stab, w, batch, seq, rope_dim, dv, dvp):
    tm = ROW_TILE
    assert seq % tm == 0
    nt = seq // tm
    rows = batch * seq
    d = dims
    hv = w["w_uv_t"].shape[0]

    def row_spec(width):
        return pl.BlockSpec((tm, width), lambda b, t: (b * nt + t, 0))

    def tab_spec():
        return pl.BlockSpec((tm, LANES), lambda b, t: (t, 0))

    in_specs = [row_spec(d.d_model), tab_spec(), tab_spec(),
                _const_spec((1, d.d_model)), _const_spec((d.d_model, d.n_a)),
                _const_spec((d.d_model, 2 * LANES)), _const_spec((d.d_model, 2 * d.d_model)),
                _const_spec((CONV_WIDTH, d.conv_dim)), _const_spec((d.conv_dim, d.d_model)),
                _const_spec((1, d.q_rank)), _const_spec((d.q_rank, d.hw)), _const_spec((d.q_rank, d.hw)),
                _const_spec((1, d.kv_rank)), _const_spec((d.kv_rank, d.hw)), _const_spec((hv, d.kv_rank))]
    out_shape = (jax.ShapeDtypeStruct((rows, d.hw), BF16),
                 jax.ShapeDtypeStruct((rows, d.hw), BF16),
                 jax.ShapeDtypeStruct((batch, hv, seq), BF16),
                 jax.ShapeDtypeStruct((rows, d.kv_rank), F32),
                 jax.ShapeDtypeStruct((rows, rope_dim), F32),
                 jax.ShapeDtypeStruct((rows, d.d_model), BF16),
                 jax.ShapeDtypeStruct((rows, d.d_model), BF16),
                 jax.ShapeDtypeStruct((batch, CONV_WIDTH - 1, d.conv_dim), F32))
    out_specs = (row_spec(d.hw), row_spec(d.hw),
                 pl.BlockSpec((None, hv, tm), lambda b, t: (b, 0, t)), row_spec(d.kv_rank),
                 row_spec(rope_dim), row_spec(d.d_model), row_spec(d.d_model),
                 pl.BlockSpec((None, CONV_WIDTH - 1, d.conv_dim), lambda b, t: (b, 0, 0)))
    kern = functools.partial(_pre_prompt_kernel, dims, scale, tm, dv, dvp)
    return pl.pallas_call(
        kern, grid=(batch, nt), in_specs=in_specs, out_specs=out_specs, out_shape=out_shape,
        scratch_shapes=[pltpu.VMEM((tm + SUBLANES, d.conv_dim), F32)],
        compiler_params=_params("arbitrary", "arbitrary"), name="pre_prompt",
    )(x2d, ctab, stab, w["g_pre"], w["w_in_a"], w["w_in_kr"], w["w_in_g"], w["conv_w"], w["w_conv_out"], w["q_g"],
      w["w_uq_a"], w["w_uq_b"], w["kv_g"], w["w_uk_pad"], w["w_uv_t"])


def _pre_sample_kernel(dims, scale,
                       x_ref, state_ref, ctab_ref, stab_ref, gpre_ref, wa_ref, wkr_ref, wg_ref, convw_ref, wco_ref,
                       qg_ref, wuqa_ref, wuqb_ref, kvg_ref, wukt_ref,
                       q_out, qlat_out, ckv_out, kr_out, mc_out, sg_out, conv_out):
    c = dims.conv_dim
    u, gate_b, g_conv, g_mla, q, ckvn, krr = _pre_common(
        dims, x_ref[...], ctab_ref[...], stab_ref[...], gpre_ref, wa_ref, wkr_ref, wg_ref, qg_ref, wuqa_ref,
        wuqb_ref, kvg_ref, scale)
    u2 = state_ref[:, 0:c]
    u1 = state_ref[:, c:2 * c]
    cw = convw_ref[...]
    conv = u2 * cw[0:1, :] + u1 * cw[1:2, :] + u * cw[2:3, :]
    y_conv = _bdot(gate_b * conv, wco_ref[...])
    mc_out[...] = jax.nn.sigmoid(g_conv) * y_conv
    sg_out[...] = jax.nn.sigmoid(g_mla)
    conv_out[:, 0:c] = u1
    conv_out[:, c:2 * c] = u
    q_out[...] = q
    ckv_out[...] = ckvn
    kr_out[...] = krr
    kvr = dims.kv_rank
    qb16 = q.astype(BF16)
    for hd in range(dims.n_heads):
        blk = qb16[:, hd * LANES:(hd + 1) * LANES]
        qlat_out[:, hd * kvr:(hd + 1) * kvr] = jnp.dot(
            blk, wukt_ref[hd * LANES:(hd + 1) * LANES, :], preferred_element_type=F32)


def _pre_sample(dims, scale, xs, state2d, ctab, stab, w):
    n = xs.shape[0]
    d = dims
    out_shape = (jax.ShapeDtypeStruct((n, d.hw), F32),
                 jax.ShapeDtypeStruct((n, d.n_heads * d.kv_rank), F32),
                 jax.ShapeDtypeStruct((n, d.kv_rank), F32),
                 jax.ShapeDtypeStruct((n, LANES), F32),
                 jax.ShapeDtypeStruct((n, d.d_model), F32),
                 jax.ShapeDtypeStruct((n, d.d_model), F32),
                 jax.ShapeDtypeStruct((n, (CONV_WIDTH - 1) * d.conv_dim), F32))
    kern = functools.partial(_pre_sample_kernel, dims, scale)
    return pl.pallas_call(kern, out_shape=out_shape, compiler_params=_params(), name="pre_sample")(
        xs, state2d, ctab, stab, w["g_pre"], w["w_in_a"], w["w_in_kr"], w["w_in_g"], w["conv_w"], w["w_conv_out"],
        w["q_g"],
        w["w_uq_a"], w["w_uq_b"], w["kv_g"], w["w_uk_t_pad"])


def _attn_kernel(tq, tk, hp, dv, dvp, q_ref, k_ref, vt_ref, o_ref, sa_ref, sb_ref):
    qi = pl.program_id(2)

    def scores(j, dst):
        start = pl.multiple_of(j * tk, tk)
        for h in range(hp):
            kh = k_ref[pl.ds(start, tk), h * LANES:(h + 1) * LANES]
            qh = q_ref[:, h * LANES:(h + 1) * LANES]
            dst[h] = lax.dot_general(kh, qh, (((1,), (1,)), ((), ())), preferred_element_type=F32)

    def process(j, src, state, masked):
        start = pl.multiple_of(j * tk, tk)
        new_state = []
        for h in range(hp):
            m_prev, acc = state[h]
            st = src[h]
            if masked:
                key = lax.broadcasted_iota(jnp.int32, (tk, tq), 0) + j * tk
                qry = lax.broadcasted_iota(jnp.int32, (tk, tq), 1) + qi * tq
                st = jnp.where(key <= qry, st, NEG_INF)
            m_new = jnp.maximum(m_prev, jnp.max(st, axis=0, keepdims=True))
            alpha = jnp.exp2(m_prev - m_new)
            p = jnp.exp2(st - m_new).astype(BF16)
            vth = vt_ref[h * dvp:(h + 1) * dvp, pl.ds(start, tk)]
            acc = alpha * acc + jnp.dot(vth, p, preferred_element_type=F32)
            new_state.append((m_new, acc))
        return tuple(new_state)

    assert tq == 2 * tk
    init = tuple((jnp.full((1, tq), NEG_INF, F32), jnp.zeros((dvp, tq), F32)) for _ in range(hp))
    scores(0, sa_ref)

    def pair(t, state):
        scores(2 * t + 1, sb_ref)
        state = process(2 * t, sa_ref, state, False)
        scores(2 * t + 2, sa_ref)
        return process(2 * t + 1, sb_ref, state, False)

    state = lax.fori_loop(0, qi, pair, init)
    scores(2 * qi + 1, sb_ref)
    state = process(2 * qi, sa_ref, state, True)
    state = process(2 * qi + 1, sb_ref, state, True)
    ot = jnp.concatenate([acc[0:dv] / acc[dv:dv + 1] for (_, acc) in state], axis=0)
    o_ref[...] = ot.T.astype(o_ref.dtype)


def _prompt_attention(q, k, vt, batch, seq, n_heads, dv):
    tq, tk = ATTN_TQ, ATTN_TK
    assert seq % tq == 0 and tq % tk == 0
    hp = LANES // dv
    assert n_heads % hp == 0
    dvp = vt.shape[1] // n_heads
    nq = seq // tq
    kern = functools.partial(_attn_kernel, tq, tk, hp, dv, dvp)
    return pl.pallas_call(
        kern, grid=(batch, n_heads // hp, nq),
        in_specs=[pl.BlockSpec((tq, hp * LANES), lambda b, g, i: (b * nq + i, g)),
                  pl.BlockSpec((seq, hp * LANES), lambda b, g, i: (b, g)),
                  pl.BlockSpec((None, hp * dvp, seq), lambda b, g, i: (b, g, 0))],
        out_specs=pl.BlockSpec((tq, hp * dv), lambda b, g, i: (b * nq + i, g)),
        out_shape=jax.ShapeDtypeStruct((batch * seq, n_heads * dv), BF16),
        scratch_shapes=[pltpu.VMEM((hp, tk, tq), F32), pltpu.VMEM((hp, tk, tq), F32)],
        compiler_params=_params("arbitrary", "arbitrary", "arbitrary"), name="prompt_attn",
    )(q, k, vt)


def _mix_and_query(x, o, mc, sg, wmla_ref, wmix_ref, gmixpost_ref, gcapre_ref, wcaq_ref, ca_scale):
    y_mla = _bdot(o, wmla_ref[...])
    merged = mc.astype(F32) + sg.astype(F32) * y_mla
    y = _bdot(merged, wmix_ref[...])
    x1 = x + _rms(y, gmixpost_ref[...])
    qc = _bdot(_rms(x1, gcapre_ref[...]), wcaq_ref[...]) * ca_scale
    return x1, qc


def _mlp_tail(x1, oc, wcao_ref, gcapost_ref, gmlppre_ref, wup_ref, wdown_ref, gmlppost_ref):
    ca = _bdot(oc, wcao_ref[...])
    x2 = x1 + _rms(ca, gcapost_ref[...])
    hid = jnp.square(jnp.maximum(_bdot(_rms(x2, gmlppre_ref[...]), wup_ref[...]), 0.0))
    return x2 + _rms(_bdot(hid, wdown_ref[...]), gmlppost_ref[...])


def _post_prompt_kernel(ca_heads, ca_scale,
                        x_ref, o_ref, mc_ref, sg_ref, mk_ref, mv_ref,
                        wmla_ref, wmix_ref, gmixpost_ref, gcapre_ref, wcaq_ref, wcao_ref, gcapost_ref,
                        gmlppre_ref, wup_ref, wdown_ref, gmlppost_ref, y_ref):
    x1, qc = _mix_and_query(x_ref[...], o_ref[...], mc_ref[...], sg_ref[...], wmla_ref, wmix_ref,
                            gmixpost_ref, gcapre_ref, wcaq_ref, ca_scale)
    hd = qc.shape[1] // ca_heads
    outs = []
    for h in range(ca_heads):
        sl = slice(h * hd, (h + 1) * hd)
        p = _softmax_rows(_bdot_nt(qc[:, sl], mk_ref[:, sl]))
        outs.append(_bdot(p, mv_ref[:, sl]))
    oc = jnp.concatenate(outs, axis=1)
    y_ref[...] = _mlp_tail(x1, oc, wcao_ref, gcapost_ref, gmlppre_ref, wup_ref, wdown_ref, gmlppost_ref)


def _post_prompt(x2d, o, mc, sg, mk, mv, w, batch, seq, ca_heads, n_mem):
    tm = ROW_TILE
    nt = seq // tm
    rows, dm = x2d.shape
    dff = w["w_ff_up"].shape[1]
    ca_scale = float((dm // ca_heads) ** -0.5)

    def row_spec(width):
        return pl.BlockSpec((tm, width), lambda i: (i, 0))

    mem_spec = pl.BlockSpec((n_mem, dm), lambda i: (i // nt, 0))
    in_specs = [row_spec(dm), row_spec(o.shape[1]), row_spec(dm), row_spec(dm), mem_spec, mem_spec,
                _const_spec((o.shape[1], dm)), _const_spec((dm, dm)), _const_spec((1, dm)),
                _const_spec((1, dm)), _const_spec((dm, dm)), _const_spec((dm, dm)), _const_spec((1, dm)),
                _const_spec((1, dm)), _const_spec((dm, dff)), _const_spec((dff, dm)), _const_spec((1, dm))]
    kern = functools.partial(_post_prompt_kernel, ca_heads, ca_scale)
    return pl.pallas_call(
        kern, grid=(rows // tm,), in_specs=in_specs, out_specs=row_spec(dm),
        out_shape=jax.ShapeDtypeStruct((rows, dm), F32),
        compiler_params=_params("arbitrary"), name="post_prompt",
    )(x2d, o, mc, sg, mk, mv, w["w_mla"], w["w_mix"], w["g_mix_post"], w["g_ca_pre"], w["w_ca_q"],
      w["w_ca_o"], w["g_ca_post"], w["g_mlp_pre"], w["w_ff_up"], w["w_ff_down"], w["g_mlp_post"])


def _post_sample_a_kernel(n_heads, v_dim, ca_scale,
                          x_ref, olat_ref, mc_ref, sg_ref, wuv_ref, wmla_ref, wmix_ref, gmixpost_ref,
                          gcapre_ref, wcaq_ref, x1_ref, qc_ref):
    n = x_ref.shape[0]
    full = _bdot(olat_ref[...], wuv_ref[...])
    full = full.reshape(n, n_heads, n_heads * v_dim)
    hidx = lax.broadcasted_iota(jnp.int32, full.shape, 1)
    lane_head = lax.broadcasted_iota(jnp.int32, full.shape, 2) // v_dim
    o = jnp.sum(jnp.where(hidx == lane_head, full, 0.0), axis=1)
    x1, qc = _mix_and_query(x_ref[...], o, mc_ref[...], sg_ref[...], wmla_ref, wmix_ref,
                            gmixpost_ref, gcapre_ref, wcaq_ref, ca_scale)
    x1_ref[...] = x1
    qc_ref[...] = qc


def _post_sample_b_kernel(ca_heads, q_ref, mk_ref, mv_ref, o_ref):
    g = q_ref.shape[0]
    rows = mk_ref.shape[1]
    s8 = 2 * ca_heads
    assert s8 == SUBLANES and mk_ref.shape[2] == LANES
    lane = lax.broadcasted_iota(jnp.int32, (s8, rows), 1)
    sub = lax.broadcasted_iota(jnp.int32, (s8, rows), 0)
    own = (lane % s8) == sub
    low_half = (lax.broadcasted_iota(jnp.int32, (1, rows), 1) % s8) < ca_heads
    for i in range(g):
        q = q_ref[i]
        q8 = jnp.concatenate([q[:, (2 * (j % ca_heads) + j // ca_heads) * LANES:
                                   (2 * (j % ca_heads) + j // ca_heads + 1) * LANES] for j in range(s8)], axis=0)
        kb = mk_ref[i].astype(BF16)
        p_all = lax.dot_general(q8.astype(BF16), kb, (((1,), (1,)), ((), ())), preferred_element_type=F32)
        d = jnp.sum(jnp.where(own, p_all, 0.0), axis=0, keepdims=True)
        d = d + jnp.where(low_half, pltpu.roll(d, rows - ca_heads, 1), pltpu.roll(d, ca_heads, 1))
        dm = jnp.where(own, jnp.broadcast_to(d, (s8, rows)), NEG_INF)
        e = jnp.exp(dm - jnp.max(dm, axis=-1, keepdims=True))
        w = (e / jnp.sum(e, axis=-1, keepdims=True)).astype(BF16)
        o8 = jnp.dot(w, mv_ref[i].astype(BF16), preferred_element_type=F32)
        for j in range(s8):
            blk = 2 * (j % ca_heads) + j // ca_heads
            o_ref[i, :, blk * LANES:(blk + 1) * LANES] = o8[j:j + 1, :]


def _post_sample_c_kernel(x1_ref, oc_ref, wcao_ref, gcapost_ref, gmlppre_ref, wup_ref, wdown_ref,
                          gmlppost_ref, y_ref):
    y_ref[...] = _mlp_tail(x1_ref[...], oc_ref[...], wcao_ref, gcapost_ref, gmlppre_ref, wup_ref,
                           wdown_ref, gmlppost_ref)


def _post_sample(xs, olat, mc, sg, mem_k, mem_v, w, n_heads, v_dim, ca_heads):
    n, dm = xs.shape
    ca_scale = float((dm // ca_heads) ** -0.5)
    kern_a = functools.partial(_post_sample_a_kernel, n_heads, v_dim, ca_scale)
    x1, qc = pl.pallas_call(
        kern_a, out_shape=(jax.ShapeDtypeStruct((n, dm), F32), jax.ShapeDtypeStruct((n, dm), F32)),
        compiler_params=_params(), name="post_sample_a",
    )(xs, olat, mc, sg, w["w_uv_flat"], w["w_mla"], w["w_mix"], w["g_mix_post"], w["g_ca_pre"], w["w_ca_q"])

    g = CA_SAMPLES_PER_STEP
    assert n % g == 0
    kern_b = functools.partial(_post_sample_b_kernel, ca_heads)
    mem_spec = pl.BlockSpec((g,) + mem_k.shape[1:], lambda i: (i, 0, 0))
    oc = pl.pallas_call(
        kern_b, grid=(n // g,),
        in_specs=[pl.BlockSpec((g, 1, dm), lambda i: (i, 0, 0)), mem_spec, mem_spec],
        out_specs=pl.BlockSpec((g, 1, dm), lambda i: (i, 0, 0)),
        out_shape=jax.ShapeDtypeStruct((n, 1, dm), F32),
        compiler_params=_params("arbitrary"), name="post_sample_b",
    )(qc.reshape(n, 1, dm), mem_k, mem_v)

    return pl.pallas_call(
        _post_sample_c_kernel, out_shape=jax.ShapeDtypeStruct((n, dm), F32),
        compiler_params=_params(), name="post_sample_c",
    )(x1, oc.reshape(n, dm), w["w_ca_o"], w["g_ca_post"], w["g_mlp_pre"], w["w_ff_up"], w["w_ff_down"],
      w["g_mlp_post"])


def _dec_attn_kernel(n_samples, n_pages, n_heads, rope_dim,
                     pt_ref, qlat_ref, qrope_ref, ckvn_ref, krn_ref, cache_ckv, cache_krt,
                     o_ref, cbuf, kbuf, sems):
    ch = DEC_PAGES_PER_STEP
    nch = n_pages // ch
    total = n_samples * nch
    page, kvr = cbuf.shape[2], cbuf.shape[3]

    def copies(g, slot):
        b = g // nch
        c = g % nch
        out = []
        for i in range(ch):
            pg = pt_ref[b, c * ch + i]
            out.append(pltpu.make_async_copy(cache_ckv.at[pg], cbuf.at[slot, i], sems.at[0, slot]))
            out.append(pltpu.make_async_copy(cache_krt.at[pg], kbuf.at[slot, i], sems.at[1, slot]))
        return out

    def start(g):
        for n, cp in enumerate(copies(g, g % DEC_NBUF)):
            cp.start(priority=(n // 2) % 2)

    def wait(g):
        for cp in copies(g, g % DEC_NBUF):
            cp.wait()

    def q_rows(b):
        row0 = pl.multiple_of(b * n_heads, n_heads)
        return qlat_ref[pl.ds(row0, n_heads), :], qrope_ref[pl.ds(row0, n_heads), :][:, :rope_dim]

    def scores(g):
        ql, qr = q_rows(g // nch)
        slot = g % DEC_NBUF
        ckv = cbuf[slot].reshape(ch * page, kvr).astype(BF16)
        qrb = qr.astype(BF16)
        s_rope = jnp.concatenate(
            [jnp.dot(qrb, kbuf[slot, i].astype(BF16), preferred_element_type=F32) for i in range(ch)], axis=1)
        return lax.dot_general(ql.astype(BF16), ckv, (((1,), (1,)), ((), ())),
                               preferred_element_type=F32) + s_rope

    for g0 in range(DEC_NBUF - 1):
        start(g0)
    wait(0)
    s0 = scores(0)

    def body(g, carry):
        s_cur, m, l, acc = carry
        b = g // nch
        c = g % nch

        @pl.when(g + 1 < total)
        def _():
            wait(g + 1)

        @pl.when(g + (DEC_NBUF - 1) < total)
        def _():
            start(g + (DEC_NBUF - 1))

        s_next = scores(jnp.minimum(g + 1, total - 1))

        ql, qr = q_rows(b)
        c_new = ckvn_ref[pl.ds(b, 1), :]
        r_new = krn_ref[pl.ds(b, 1), :][:, :rope_dim]
        s_new = jnp.sum(ql * c_new, axis=-1, keepdims=True) + jnp.sum(qr * r_new, axis=-1, keepdims=True)
        first = c == 0
        m = jnp.where(first, s_new, m)
        l = jnp.where(first, 1.0, l)
        acc = jnp.where(first, jnp.broadcast_to(c_new, acc.shape), acc)

        ckv = cbuf[g % DEC_NBUF].reshape(ch * page, kvr).astype(BF16)
        m_new = jnp.maximum(m, jnp.max(s_cur, axis=-1, keepdims=True))
        alpha = jnp.exp(m - m_new)
        p = jnp.exp(s_cur - m_new)
        l = alpha * l + jnp.sum(p, axis=-1, keepdims=True)
        acc = alpha * acc + jnp.dot(p.astype(BF16), ckv, preferred_element_type=F32)
        o_ref[pl.ds(pl.multiple_of(b * n_heads, n_heads), n_heads), :] = acc / l
        return s_next, m_new, l, acc

    init = (s0, jnp.zeros((n_heads, 1), F32), jnp.zeros((n_heads, 1), F32), jnp.zeros((n_heads, kvr), F32))
    lax.fori_loop(0, total, body, init)


def _decode_attention(page_table, qlat, qrope, ckv_new, kr_new, cache_ckv, cache_krt, n_heads, rope_dim):
    n_samples, n_pages = page_table.shape
    page, kvr = cache_ckv.shape[1], cache_ckv.shape[2]
    ch = DEC_PAGES_PER_STEP
    assert n_pages % ch == 0

    def whole(shape):
        nd = len(shape)
        return pl.BlockSpec(shape, lambda i, pt: (0,) * nd)

    kern = functools.partial(_dec_attn_kernel, n_samples, n_pages, n_heads, rope_dim)
    grid_spec = pltpu.PrefetchScalarGridSpec(
        num_scalar_prefetch=1, grid=(1,),
        in_specs=[whole(qlat.shape), whole(qrope.shape), whole(ckv_new.shape), whole(kr_new.shape),
                  pl.BlockSpec(memory_space=pl.ANY), pl.BlockSpec(memory_space=pl.ANY)],
        out_specs=whole(qlat.shape),
        scratch_shapes=[pltpu.VMEM((DEC_NBUF, ch, page, kvr), F32),
                        pltpu.VMEM((DEC_NBUF, ch, rope_dim, page), F32),
                        pltpu.SemaphoreType.DMA((2, DEC_NBUF))])
    return pl.pallas_call(
        kern, grid_spec=grid_spec, out_shape=jax.ShapeDtypeStruct(qlat.shape, F32),
        compiler_params=_params("arbitrary"), name="decode_attn",
    )(page_table, qlat, qrope, ckv_new, kr_new, cache_ckv, cache_krt)


def _rot_half(w):
    half = w.shape[-1] // 2
    return jnp.concatenate([-w[..., half:], w[..., :half]], axis=-1)


def _mem_rows(cache):
    _, n, n_mem, heads, hd = cache.shape
    assert hd == 2 * LANES
    x = cache.reshape(n, n_mem, heads, 2, LANES)
    return jnp.transpose(x, (0, 1, 3, 2, 4)).reshape(n, n_mem * 2 * heads, LANES)


def _rope_lane_freqs(rope_dim):
    inv = 1.0 / (ROPE_BASE ** (jnp.arange(0, rope_dim, 2, dtype=F32) / rope_dim))
    return jnp.concatenate([inv, inv, jnp.zeros((LANES - rope_dim,), F32)])


def _rope_tabs(pos, rope_dim, nope_dim):
    ang = pos.astype(F32)[:, None] * _rope_lane_freqs(rope_dim)[None, :]
    keep = (jnp.arange(LANES) < rope_dim + nope_dim).astype(F32)
    return jnp.cos(ang) * keep, jnp.sin(ang)


def _rope_tabs_range(seq, rope_dim, nope_dim):
    assert seq % LANES == 0
    freqs = _rope_lane_freqs(rope_dim)
    a = (jnp.arange(seq // LANES, dtype=F32) * LANES)[:, None, None] * freqs
    b = jnp.arange(LANES, dtype=F32)[None, :, None] * freqs
    ca, sa, cb, sb = jnp.cos(a), jnp.sin(a), jnp.cos(b), jnp.sin(b)
    keep = (jnp.arange(LANES) < rope_dim + nope_dim).astype(F32)
    ctab = (ca * cb - sa * sb) * keep
    stab = sa * cb + ca * sb
    return ctab.reshape(seq, LANES), stab.reshape(seq, LANES)


def kernel(x_prompt, x_sample, mem_prompt, cache_ckv, cache_krope, state_conv, cache_mem_k, cache_mem_v,
           page_table, norm_mix_pre_g, w_in, conv_w, w_conv_out, q_norm_g, w_uq, kv_norm_g, w_uk, w_uv,
           w_mla_out, w_mix_out, norm_mix_post_g, norm_ca_pre_g, mem_norm_g, w_ca_q, w_ca_k, w_ca_v, w_ca_o,
           norm_ca_post_g, norm_mlp_pre_g, w_ff_up, w_ff_down, norm_mlp_post_g):
    depth = w_in.shape[0]
    assert depth == 1, "single-layer step"
    batch, seq, dm = x_prompt.shape
    n_s, t_s, _ = x_sample.shape
    assert t_s == 1
    conv_dim = conv_w.shape[2]
    q_rank, n_heads, qk_dim = w_uq.shape[1:]
    kv_rank, _, nope_dim = w_uk.shape[1:]
    v_dim = w_uv.shape[3]
    rope_dim = qk_dim - nope_dim
    n_mem, ca_heads, ca_hd = cache_mem_k.shape[2:]
    n_pool, page = cache_ckv.shape[1:3]
    past_len = page_table.shape[1] * page
    assert rope_dim + nope_dim <= LANES and v_dim <= LANES
    dims = _PreDims(dm, conv_dim, q_rank, kv_rank, n_heads)
    scale = float(qk_dim ** -0.5)

    w0 = w_in[0]
    o_kr = 3 * conv_dim + q_rank + kv_rank
    w_kr = w0[:, o_kr:o_kr + rope_dim]
    zpad = jnp.zeros((dm, LANES - rope_dim), F32)
    w_in_kr = jnp.concatenate([w_kr, zpad, _rot_half(w_kr), zpad], axis=1).astype(BF16)
    uq = w_uq[0]
    uq_nope, uq_rope = uq[:, :, :nope_dim], uq[:, :, nope_dim:]
    hz = lambda r, width: jnp.zeros((r, n_heads, width), F32)
    pad_tail = LANES - rope_dim - nope_dim
    w_uq_a = jnp.concatenate([uq_rope, uq_nope, hz(q_rank, pad_tail)], axis=2).reshape(q_rank, n_heads * LANES)
    w_uq_b = jnp.concatenate([_rot_half(uq_rope), hz(q_rank, LANES - rope_dim)], axis=2).reshape(
        q_rank, n_heads * LANES)
    uk = w_uk[0]
    w_uk_pad = jnp.concatenate([hz(kv_rank, rope_dim), uk, hz(kv_rank, pad_tail)], axis=2).reshape(
        kv_rank, n_heads * LANES)
    w_uk_t_pad = jnp.transpose(w_uk_pad.reshape(kv_rank, n_heads, LANES), (1, 2, 0)).reshape(
        n_heads * LANES, kv_rank)
    uv = w_uv[0].reshape(kv_rank, n_heads * v_dim)
    dvp = v_dim + ATTN_V_EXTRA_ROWS
    w_uv_t = jnp.concatenate([jnp.transpose(w_uv[0], (1, 2, 0)),
                              jnp.zeros((n_heads, ATTN_V_EXTRA_ROWS, kv_rank), F32)], axis=1).reshape(
        n_heads * dvp, kv_rank)
    w_mla = w_mla_out[0]
    w = {
        "g_pre": norm_mix_pre_g, "w_in_a": w0[:, :o_kr].astype(BF16), "w_in_kr": w_in_kr,
        "w_in_g": w0[:, o_kr + rope_dim:].astype(BF16), "conv_w": conv_w[0], "w_conv_out": w_conv_out[0].astype(BF16),
        "q_g": q_norm_g, "w_uq_a": w_uq_a.astype(BF16), "w_uq_b": w_uq_b.astype(BF16), "kv_g": kv_norm_g,
        "w_uk_pad": w_uk_pad.astype(BF16), "w_uk_t_pad": w_uk_t_pad.astype(BF16),
        "w_uv_t": w_uv_t.astype(BF16), "w_uv_flat": uv.astype(BF16),
        "w_mla": w_mla.astype(BF16), "w_mix": w_mix_out[0].astype(BF16),
        "g_mix_post": norm_mix_post_g, "g_ca_pre": norm_ca_pre_g,
        "w_ca_q": w_ca_q[0].reshape(dm, ca_heads * ca_hd).astype(BF16),
        "w_ca_o": w_ca_o[0].reshape(ca_heads * ca_hd, dm).astype(BF16),
        "g_ca_post": norm_ca_post_g, "g_mlp_pre": norm_mlp_pre_g, "w_ff_up": w_ff_up[0].astype(BF16),
        "w_ff_down": w_ff_down[0].astype(BF16), "g_mlp_post": norm_mlp_post_g,
    }

    mk_p, mv_p = _memory_kv(mem_prompt.reshape(batch * n_mem, dm), mem_norm_g,
                            w_ca_k[0].reshape(dm, ca_heads * ca_hd).astype(BF16),
                            w_ca_v[0].reshape(dm, ca_heads * ca_hd).astype(BF16))
    ctab_p, stab_p = _rope_tabs_range(seq, rope_dim, nope_dim)
    x2d = x_prompt.reshape(batch * seq, dm)
    q, k, vt, ckv_p, kr_p, mc_p, sg_p, conv_p = _pre_prompt(dims, scale * LOG2_E, x2d, ctab_p, stab_p, w, batch,
                                                           seq, rope_dim, v_dim, dvp)
    o_p = _prompt_attention(q, k, vt, batch, seq, n_heads, v_dim)
    y_p = _post_prompt(x2d, o_p, mc_p, sg_p, mk_p, mv_p, w, batch, seq, ca_heads, n_mem)

    ctab_s, stab_s = _rope_tabs(jnp.full((n_s,), past_len, jnp.int32), rope_dim, nope_dim)
    xs = x_sample.reshape(n_s, dm)
    q_s, qlat_s, ckv_s, kr_s, mc_s, sg_s, conv_s = _pre_sample(
        dims, scale, xs, state_conv.reshape(n_s, (CONV_WIDTH - 1) * conv_dim), ctab_s, stab_s, w)
    olat = _decode_attention(page_table, qlat_s.reshape(n_s * n_heads, kv_rank),
                             q_s.reshape(n_s * n_heads, LANES), ckv_s, kr_s,
                             cache_ckv.reshape(n_pool, page, kv_rank),
                             jnp.swapaxes(cache_krope.reshape(n_pool, page, rope_dim), 1, 2), n_heads, rope_dim)
    y_s = _post_sample(xs, olat, mc_s, sg_s, _mem_rows(cache_mem_k), _mem_rows(cache_mem_v), w, n_heads, v_dim,
                       ca_heads)

    return (y_p.reshape(batch, seq, dm),
            y_s.reshape(n_s, t_s, dm),
            ckv_p.reshape(depth, batch, seq, kv_rank),
            kr_p.reshape(depth, batch, seq, rope_dim),
            conv_p.reshape(depth, batch, CONV_WIDTH - 1, conv_dim),
            mk_p.reshape(depth, batch, n_mem, ca_heads, ca_hd),
            mv_p.reshape(depth, batch, n_mem, ca_heads, ca_hd),
            ckv_s.reshape(depth, n_s, t_s, kv_rank),
            kr_s[:, :rope_dim].reshape(depth, n_s, t_s, rope_dim),
            conv_s.reshape(depth, n_s, CONV_WIDTH - 1, conv_dim))
```

```python
import functools

import numpy as np
import jax
import jax.numpy as jnp
from jax import lax
from jax.experimental import pallas as pl
from jax.experimental.pallas import tpu as pltpu

F32 = jnp.float32
BF16 = jnp.bfloat16

RMS_EPS = 1e-6
NEG_INF = -1e30
ROPE_BASE = 10000.0
CONV_WIDTH = 3
LOG2_E = 1.4426950408889634

LANES = 128
SUBLANES = 8
VMEM_LIMIT_BYTES = 56 * 1024 * 1024

ROW_TILE = 512
ATTN_TQ = 1024
ATTN_TK = 512
ATTN_V_EXTRA_ROWS = 16
DEC_PAGES_PER_STEP = 32
DEC_NBUF = 4
CA_SAMPLES_PER_STEP = 8


def _rms(x, g):
    return x * lax.rsqrt(jnp.mean(x * x, axis=-1, keepdims=True) + RMS_EPS) * g


def _bdot(a, b):
    return jnp.dot(a.astype(BF16), b.astype(BF16), preferred_element_type=F32)


def _bdot_nt(a, b):
    return lax.dot_general(a.astype(BF16), b.astype(BF16), (((1,), (1,)), ((), ())),
                           preferred_element_type=F32)


def _softmax_rows(s):
    m = jnp.max(s, axis=-1, keepdims=True)
    p = jnp.exp(s - m)
    return p / jnp.sum(p, axis=-1, keepdims=True)


def _const_spec(shape):
    nd = len(shape)
    return pl.BlockSpec(shape, lambda *_: (0,) * nd, pipeline_mode=pl.Buffered(1))


def _params(*sem):
    return pltpu.CompilerParams(dimension_semantics=tuple(sem) if sem else None,
                                vmem_limit_bytes=VMEM_LIMIT_BYTES)


def _memkv_kernel(mem_ref, g_ref, wk_ref, wv_ref, k_ref, v_ref):
    mn = _rms(mem_ref[...], g_ref[...]).astype(BF16)
    k_ref[...] = jnp.dot(mn, wk_ref[...], preferred_element_type=F32)
    v_ref[...] = jnp.dot(mn, wv_ref[...], preferred_element_type=F32)


def _memory_kv(mem2d, g, wk, wv):
    rows, d = mem2d.shape
    out = jax.ShapeDtypeStruct((rows, wk.shape[1]), F32)
    return pl.pallas_call(_memkv_kernel, out_shape=(out, out), name="mem_kv",
                          compiler_params=_params())(mem2d, g, wk, wv)


class _PreDims:
    def __init__(self, d_model, conv_dim, q_rank, kv_rank, n_heads):
        self.d_model, self.conv_dim, self.q_rank, self.kv_rank = d_model, conv_dim, q_rank, kv_rank
        self.n_heads = n_heads
        c = conv_dim
        self.o_h, self.o_gb, self.o_gc = 0, c, 2 * c
        self.o_cq = 3 * c
        self.o_ckv = self.o_cq + q_rank
        self.n_a = self.o_ckv + kv_rank
        self.hw = n_heads * LANES


def _pre_common(dims, x, ctab, stab, gpre_ref, wa_ref, wkr_ref, wg_ref, qg_ref, wuqa_ref, wuqb_ref, kvg_ref,
                scale):
    d = dims
    xn = _rms(x, gpre_ref[...]).astype(BF16)

    def proj(w_ref, lo, hi):
        return jnp.dot(xn, w_ref[:, lo:hi], preferred_element_type=F32)

    h = proj(wa_ref, d.o_h, d.o_gb)
    gate_b = proj(wa_ref, d.o_gb, d.o_gc)
    gate_c = proj(wa_ref, d.o_gc, d.o_cq)
    cq = proj(wa_ref, d.o_cq, d.o_ckv)
    ckv = proj(wa_ref, d.o_ckv, d.n_a)
    kr2 = proj(wkr_ref, 0, 2 * LANES)
    kra, krb = kr2[:, :LANES], kr2[:, LANES:]
    g_conv = proj(wg_ref, 0, d.d_model)
    g_mla = proj(wg_ref, d.d_model, 2 * d.d_model)

    u = gate_c * h
    cqn = _rms(cq, qg_ref[...]).astype(BF16)
    qa = jnp.dot(cqn, wuqa_ref[...], preferred_element_type=F32)
    qbc = jnp.dot(cqn, wuqb_ref[...], preferred_element_type=F32)
    r = wuqb_ref.shape[1] // d.n_heads
    pieces = []
    for hd in range(d.n_heads):
        blk, off = divmod(hd * r, LANES)
        piece = qbc[:, blk * LANES:(blk + 1) * LANES]
        pieces.append(pltpu.roll(piece, LANES - off, 1) if off else piece)
    qb = jnp.concatenate(pieces, axis=1)
    ct = jnp.concatenate([ctab] * d.n_heads, axis=1)
    st = jnp.concatenate([stab] * d.n_heads, axis=1)
    q = (qa * ct + qb * st) * scale
    ckvn = _rms(ckv, kvg_ref[...])
    krr = kra * ctab + krb * stab
    return u, gate_b, g_conv, g_mla, q, ckvn, krr


def _pre_prompt_kernel(dims, scale, tm, dv, dvp,
                       x_ref, ctab_ref, stab_ref, gpre_ref, wa_ref, wkr_ref, wg_ref, convw_ref, wco_ref, qg_ref,
                       wuqa_ref, wuqb_ref, kvg_ref, wuk_ref, wuvt_ref,
                       q_out, k_out, vt_out, ckv_out, kr_out, mc_out, sg_out, conv_out, ubuf):
    hist = SUBLANES
    u, gate_b, g_conv, g_mla, q, ckvn, krr = _pre_common(
        dims, x_ref[...], ctab_ref[...], stab_ref[...], gpre_ref, wa_ref, wkr_ref, wg_ref, qg_ref, wuqa_ref,
        wuqb_ref, kvg_ref, scale)

    @pl.when(pl.program_id(1) == 0)
    def _():
        ubuf[0:hist, :] = jnp.zeros((hist, dims.conv_dim), F32)

    ubuf[hist:hist + tm, :] = u
    u1 = ubuf[hist - 1:hist - 1 + tm, :]
    u2 = ubuf[hist - 2:hist - 2 + tm, :]
    cw = convw_ref[...]
    conv = u2 * cw[0:1, :] + u1 * cw[1:2, :] + u * cw[2:3, :]
    y_conv = _bdot(gate_b * conv, wco_ref[...])
    mc_out[...] = (jax.nn.sigmoid(g_conv) * y_conv).astype(mc_out.dtype)
    sg_out[...] = jax.nn.sigmoid(g_mla).astype(sg_out.dtype)
    conv_out[...] = ubuf[hist + tm - (CONV_WIDTH - 1):hist + tm, :]
    ubuf[0:hist, :] = ubuf[tm:tm + hist, :]

    q_out[...] = q.astype(q_out.dtype)
    ckv_out[...] = ckvn
    kr_out[...] = krr[:, :kr_out.shape[-1]]
    ckvb = ckvn.astype(BF16)
    ka = jnp.dot(ckvb, wuk_ref[...], preferred_element_type=F32)
    k = ka + jnp.concatenate([krr] * dims.n_heads, axis=1)
    k_out[...] = k.astype(k_out.dtype)
    vt = lax.dot_general(wuvt_ref[...], ckvb, (((1,), (1,)), ((), ())), preferred_element_type=F32)
    rid = lax.broadcasted_iota(jnp.int32, (vt.shape[0], 1), 0)
    vt_out[...] = (vt + jnp.where(rid % dvp == dv, 1.0, 0.0)).astype(vt_out.dtype)


def _pre_prompt(dims, scale, x2d, ctab, stab, w, batch, seq, rope_dim, dv, dvp):
    tm = ROW_TILE
    assert seq % tm == 0
    nt = seq // tm
    rows = batch * seq
    d = dims
    hv = w["w_uv_t"].shape[0]

    def row_spec(width):
        return pl.BlockSpec((tm, width), lambda b, t: (b * nt + t, 0))

    def tab_spec():
        return pl.BlockSpec((tm, LANES), lambda b, t: (t, 0))

    in_specs = [row_spec(d.d_model), tab_spec(), tab_spec(),
                _const_spec((1, d.d_model)), _const_spec((d.d_model, d.n_a)),
                _const_spec((d.d_model, 2 * LANES)), _const_spec((d.d_model, 2 * d.d_model)),
                _const_spec((CONV_WIDTH, d.conv_dim)), _const_spec((d.conv_dim, d.d_model)),
                _const_spec((1, d.q_rank)), _const_spec((d.q_rank, d.hw)),
                _const_spec((d.q_rank, d.n_heads * rope_dim)),
                _const_spec((1, d.kv_rank)), _const_spec((d.kv_rank, d.hw)), _const_spec((hv, d.kv_rank))]
    out_shape = (jax.ShapeDtypeStruct((rows, d.hw), BF16),
                 jax.ShapeDtypeStruct((rows, d.hw), BF16),
                 jax.ShapeDtypeStruct((batch, hv, seq), BF16),
                 jax.ShapeDtypeStruct((rows, d.kv_rank), F32),
                 jax.ShapeDtypeStruct((rows, rope_dim), F32),
                 jax.ShapeDtypeStruct((rows, d.d_model), BF16),
                 jax.ShapeDtypeStruct((rows, d.d_model), BF16),
                 jax.ShapeDtypeStruct((batch, CONV_WIDTH - 1, d.conv_dim), F32))
    out_specs = (row_spec(d.hw), row_spec(d.hw),
                 pl.BlockSpec((None, hv, tm), lambda b, t: (b, 0, t)), row_spec(d.kv_rank),
                 row_spec(rope_dim), row_spec(d.d_model), row_spec(d.d_model),
                 pl.BlockSpec((None, CONV_WIDTH - 1, d.conv_dim), lambda b, t: (b, 0, 0)))
    kern = functools.partial(_pre_prompt_kernel, dims, scale, tm, dv, dvp)
    return pl.pallas_call(
        kern, grid=(batch, nt), in_specs=in_specs, out_specs=out_specs, out_shape=out_shape,
        scratch_shapes=[pltpu.VMEM((tm + SUBLANES, d.conv_dim), F32)],
        compiler_params=_params("arbitrary", "arbitrary"), name="pre_prompt",
    )(x2d, ctab, stab, w["g_pre"], w["w_in_a"], w["w_in_kr"], w["w_in_g"], w["conv_w"], w["w_conv_out"], w["q_g"],
      w["w_uq_a"], w["w_uq_b"], w["kv_g"], w["w_uk_pad"], w["w_uv_t"])


def _pre_sample_kernel(dims, scale,
                       x_ref, state_ref, ctab_ref, stab_ref, gpre_ref, wa_ref, wkr_ref, wg_ref, convw_ref, wco_ref,
                       qg_ref, wuqa_ref, wuqb_ref, kvg_ref, wukt_ref,
                       q_out, qlat_out, ckv_out, kr_out, mc_out, sg_out, conv_out):
    c = dims.conv_dim
    u, gate_b, g_conv, g_mla, q, ckvn, krr = _pre_common(
        dims, x_ref[...], ctab_ref[...], stab_ref[...], gpre_ref, wa_ref, wkr_ref, wg_ref, qg_ref, wuqa_ref,
        wuqb_ref, kvg_ref, scale)
    u2 = state_ref[:, 0:c]
    u1 = state_ref[:, c:2 * c]
    cw = convw_ref[...]
    conv = u2 * cw[0:1, :] + u1 * cw[1:2, :] + u * cw[2:3, :]
    y_conv = _bdot(gate_b * conv, wco_ref[...])
    mc_out[...] = jax.nn.sigmoid(g_conv) * y_conv
    sg_out[...] = jax.nn.sigmoid(g_mla)
    conv_out[:, 0:c] = u1
    conv_out[:, c:2 * c] = u
    q_out[...] = q
    ckv_out[...] = ckvn
    kr_out[...] = krr
    kvr = dims.kv_rank
    qb16 = q.astype(BF16)
    for hd in range(dims.n_heads):
        blk = qb16[:, hd * LANES:(hd + 1) * LANES]
        qlat_out[:, hd * kvr:(hd + 1) * kvr] = jnp.dot(
            blk, wukt_ref[hd * LANES:(hd + 1) * LANES, :], preferred_element_type=F32)


def _pre_sample(dims, scale, xs, state2d, ctab, stab, w):
    n = xs.shape[0]
    d = dims
    out_shape = (jax.ShapeDtypeStruct((n, d.hw), F32),
                 jax.ShapeDtypeStruct((n, d.n_heads * d.kv_rank), F32),
                 jax.ShapeDtypeStruct((n, d.kv_rank), F32),
                 jax.ShapeDtypeStruct((n, LANES), F32),
                 jax.ShapeDtypeStruct((n, d.d_model), F32),
                 jax.ShapeDtypeStruct((n, d.d_model), F32),
                 jax.ShapeDtypeStruct((n, (CONV_WIDTH - 1) * d.conv_dim), F32))
    kern = functools.partial(_pre_sample_kernel, dims, scale)
    return pl.pallas_call(kern, out_shape=out_shape, compiler_params=_params(), name="pre_sample")(
        xs, state2d, ctab, stab, w["g_pre"], w["w_in_a"], w["w_in_kr"], w["w_in_g"], w["conv_w"], w["w_conv_out"],
        w["q_g"],
        w["w_uq_a"], w["w_uq_b"], w["kv_g"], w["w_uk_t_pad"])


def _attn_kernel(tq, tk, hp, dv, dvp, q_ref, k_ref, vt_ref, o_ref, sa_ref, sb_ref):
    qi = pl.program_id(2)

    def scores(j, dst):
        start = pl.multiple_of(j * tk, tk)
        for h in range(hp):
            kh = k_ref[pl.ds(start, tk), h * LANES:(h + 1) * LANES]
            qh = q_ref[:, h * LANES:(h + 1) * LANES]
            dst[h] = lax.dot_general(kh, qh, (((1,), (1,)), ((), ())), preferred_element_type=F32)

    def process(j, src, state, masked):
        start = pl.multiple_of(j * tk, tk)
        new_state = []
        for h in range(hp):
            m_prev, acc = state[h]
            st = src[h]
            if masked:
                key = lax.broadcasted_iota(jnp.int32, (tk, tq), 0) + j * tk
                qry = lax.broadcasted_iota(jnp.int32, (tk, tq), 1) + qi * tq
                st = jnp.where(key <= qry, st, NEG_INF)
            m_new = jnp.maximum(m_prev, jnp.max(st, axis=0, keepdims=True))
            alpha = jnp.exp2(m_prev - m_new)
            p = jnp.exp2(st - m_new).astype(BF16)
            vth = vt_ref[h * dvp:(h + 1) * dvp, pl.ds(start, tk)]
            acc = alpha * acc + jnp.dot(vth, p, preferred_element_type=F32)
            new_state.append((m_new, acc))
        return tuple(new_state)

    assert tq == 2 * tk
    init = tuple((jnp.full((1, tq), NEG_INF, F32), jnp.zeros((dvp, tq), F32)) for _ in range(hp))
    scores(0, sa_ref)

    def pair(t, state):
        scores(2 * t + 1, sb_ref)
        state = process(2 * t, sa_ref, state, False)
        scores(2 * t + 2, sa_ref)
        return process(2 * t + 1, sb_ref, state, False)

    state = lax.fori_loop(0, qi, pair, init)
    scores(2 * qi + 1, sb_ref)
    state = process(2 * qi, sa_ref, state, True)
    state = process(2 * qi + 1, sb_ref, state, True)
    ot = jnp.concatenate([acc[0:dv] / acc[dv:dv + 1] for (_, acc) in state], axis=0)
    o_ref[...] = ot.T.astype(o_ref.dtype)


def _prompt_attention(q, k, vt, batch, seq, n_heads, dv):
    tq, tk = ATTN_TQ, ATTN_TK
    assert seq % tq == 0 and tq % tk == 0
    hp = LANES // dv
    assert n_heads % hp == 0
    dvp = vt.shape[1] // n_heads
    nq = seq // tq
    kern = functools.partial(_attn_kernel, tq, tk, hp, dv, dvp)
    return pl.pallas_call(
        kern, grid=(batch, n_heads // hp, nq),
        in_specs=[pl.BlockSpec((tq, hp * LANES), lambda b, g, i: (b * nq + i, g)),
                  pl.BlockSpec((seq, hp * LANES), lambda b, g, i: (b, g)),
                  pl.BlockSpec((None, hp * dvp, seq), lambda b, g, i: (b, g, 0))],
        out_specs=pl.BlockSpec((tq, hp * dv), lambda b, g, i: (b * nq + i, g)),
        out_shape=jax.ShapeDtypeStruct((batch * seq, n_heads * dv), BF16),
        scratch_shapes=[pltpu.VMEM((hp, tk, tq), F32), pltpu.VMEM((hp, tk, tq), F32)],
        compiler_params=_params("arbitrary", "arbitrary", "arbitrary"), name="prompt_attn",
    )(q, k, vt)


def _mix_and_query(x, o, mc, sg, wmla_ref, wmix_ref, gmixpost_ref, gcapre_ref, wcaq_ref, ca_scale):
    y_mla = _bdot(o, wmla_ref[...])
    merged = mc.astype(F32) + sg.astype(F32) * y_mla
    y = _bdot(merged, wmix_ref[...])
    x1 = x + _rms(y, gmixpost_ref[...])
    qc = _bdot(_rms(x1, gcapre_ref[...]), wcaq_ref[...]) * ca_scale
    return x1, qc


def _mlp_tail(x1, oc, wcao_ref, gcapost_ref, gmlppre_ref, wup_ref, wdown_ref, gmlppost_ref):
    ca = _bdot(oc, wcao_ref[...])
    x2 = x1 + _rms(ca, gcapost_ref[...])
    hid = jnp.square(jnp.maximum(_bdot(_rms(x2, gmlppre_ref[...]), wup_ref[...]), 0.0))
    return x2 + _rms(_bdot(hid, wdown_ref[...]), gmlppost_ref[...])


def _post_prompt_kernel(ca_heads, ca_scale,
                        x_ref, o_ref, mc_ref, sg_ref, mk_ref, mv_ref,
                        wmla_ref, wmix_ref, gmixpost_ref, gcapre_ref, wcaq_ref, wcao_ref, gcapost_ref,
                        gmlppre_ref, wup_ref, wdown_ref, gmlppost_ref, y_ref):
    x1, qc = _mix_and_query(x_ref[...], o_ref[...], mc_ref[...], sg_ref[...], wmla_ref, wmix_ref,
                            gmixpost_ref, gcapre_ref, wcaq_ref, ca_scale)
    hd = qc.shape[1] // ca_heads
    outs = []
    for h in range(ca_heads):
        sl = slice(h * hd, (h + 1) * hd)
        p = _softmax_rows(_bdot_nt(qc[:, sl], mk_ref[:, sl]))
        outs.append(_bdot(p, mv_ref[:, sl]))
    oc = jnp.concatenate(outs, axis=1)
    y_ref[...] = _mlp_tail(x1, oc, wcao_ref, gcapost_ref, gmlppre_ref, wup_ref, wdown_ref, gmlppost_ref)


def _post_prompt(x2d, o, mc, sg, mk, mv, w, batch, seq, ca_heads, n_mem):
    tm = ROW_TILE
    nt = seq // tm
    rows, dm = x2d.shape
    dff = w["w_ff_up"].shape[1]
    ca_scale = float((dm // ca_heads) ** -0.5)

    def row_spec(width):
        return pl.BlockSpec((tm, width), lambda i: (i, 0))

    mem_spec = pl.BlockSpec((n_mem, dm), lambda i: (i // nt, 0))
    in_specs = [row_spec(dm), row_spec(o.shape[1]), row_spec(dm), row_spec(dm), mem_spec, mem_spec,
                _const_spec((o.shape[1], dm)), _const_spec((dm, dm)), _const_spec((1, dm)),
                _const_spec((1, dm)), _const_spec((dm, dm)), _const_spec((dm, dm)), _const_spec((1, dm)),
                _const_spec((1, dm)), _const_spec((dm, dff)), _const_spec((dff, dm)), _const_spec((1, dm))]
    kern = functools.partial(_post_prompt_kernel, ca_heads, ca_scale)
    return pl.pallas_call(
        kern, grid=(rows // tm,), in_specs=in_specs, out_specs=row_spec(dm),
        out_shape=jax.ShapeDtypeStruct((rows, dm), F32),
        compiler_params=_params("arbitrary"), name="post_prompt",
    )(x2d, o, mc, sg, mk, mv, w["w_mla"], w["w_mix"], w["g_mix_post"], w["g_ca_pre"], w["w_ca_q"],
      w["w_ca_o"], w["g_ca_post"], w["g_mlp_pre"], w["w_ff_up"], w["w_ff_down"], w["g_mlp_post"])


def _post_sample_a_kernel(n_heads, v_dim, ca_scale,
                          x_ref, olat_ref, mc_ref, sg_ref, wuv_ref, wmla_ref, wmix_ref, gmixpost_ref,
                          gcapre_ref, wcaq_ref, x1_ref, qc_ref):
    n = x_ref.shape[0]
    full = _bdot(olat_ref[...], wuv_ref[...])
    full = full.reshape(n, n_heads, n_heads * v_dim)
    hidx = lax.broadcasted_iota(jnp.int32, full.shape, 1)
    lane_head = lax.broadcasted_iota(jnp.int32, full.shape, 2) // v_dim
    o = jnp.sum(jnp.where(hidx == lane_head, full, 0.0), axis=1)
    x1, qc = _mix_and_query(x_ref[...], o, mc_ref[...], sg_ref[...], wmla_ref, wmix_ref,
                            gmixpost_ref, gcapre_ref, wcaq_ref, ca_scale)
    x1_ref[...] = x1
    qc_ref[...] = qc


def _post_sample_b_kernel(ca_heads, q_ref, mk_ref, mv_ref, o_ref):
    g = q_ref.shape[0]
    rows = mk_ref.shape[1]
    s8 = 2 * ca_heads
    assert s8 == SUBLANES and mk_ref.shape[2] == LANES
    lane = lax.broadcasted_iota(jnp.int32, (s8, rows), 1)
    sub = lax.broadcasted_iota(jnp.int32, (s8, rows), 0)
    own = (lane % s8) == sub
    low_half = (lax.broadcasted_iota(jnp.int32, (1, rows), 1) % s8) < ca_heads
    for i in range(g):
        q = q_ref[i]
        q8 = jnp.concatenate([q[:, (2 * (j % ca_heads) + j // ca_heads) * LANES:
                                   (2 * (j % ca_heads) + j // ca_heads + 1) * LANES] for j in range(s8)], axis=0)
        kb = mk_ref[i].astype(BF16)
        p_all = lax.dot_general(q8.astype(BF16), kb, (((1,), (1,)), ((), ())), preferred_element_type=F32)
        d = jnp.sum(jnp.where(own, p_all, 0.0), axis=0, keepdims=True)
        d = d + jnp.where(low_half, pltpu.roll(d, rows - ca_heads, 1), pltpu.roll(d, ca_heads, 1))
        dm = jnp.where(own, jnp.broadcast_to(d, (s8, rows)), NEG_INF)
        e = jnp.exp(dm - jnp.max(dm, axis=-1, keepdims=True))
        w = (e / jnp.sum(e, axis=-1, keepdims=True)).astype(BF16)
        o8 = jnp.dot(w, mv_ref[i].astype(BF16), preferred_element_type=F32)
        for j in range(s8):
            blk = 2 * (j % ca_heads) + j // ca_heads
            o_ref[i, :, blk * LANES:(blk + 1) * LANES] = o8[j:j + 1, :]


def _post_sample_c_kernel(x1_ref, oc_ref, wcao_ref, gcapost_ref, gmlppre_ref, wup_ref, wdown_ref,
                          gmlppost_ref, y_ref):
    y_ref[...] = _mlp_tail(x1_ref[...], oc_ref[...], wcao_ref, gcapost_ref, gmlppre_ref, wup_ref,
                           wdown_ref, gmlppost_ref)


def _post_sample(xs, olat, mc, sg, mem_k, mem_v, w, n_heads, v_dim, ca_heads):
    n, dm = xs.shape
    ca_scale = float((dm // ca_heads) ** -0.5)
    kern_a = functools.partial(_post_sample_a_kernel, n_heads, v_dim, ca_scale)
    x1, qc = pl.pallas_call(
        kern_a, out_shape=(jax.ShapeDtypeStruct((n, dm), F32), jax.ShapeDtypeStruct((n, dm), F32)),
        compiler_params=_params(), name="post_sample_a",
    )(xs, olat, mc, sg, w["w_uv_flat"], w["w_mla"], w["w_mix"], w["g_mix_post"], w["g_ca_pre"], w["w_ca_q"])

    g = CA_SAMPLES_PER_STEP
    assert n % g == 0
    kern_b = functools.partial(_post_sample_b_kernel, ca_heads)
    mem_spec = pl.BlockSpec((g,) + mem_k.shape[1:], lambda i: (i, 0, 0))
    oc = pl.pallas_call(
        kern_b, grid=(n // g,),
        in_specs=[pl.BlockSpec((g, 1, dm), lambda i: (i, 0, 0)), mem_spec, mem_spec],
        out_specs=pl.BlockSpec((g, 1, dm), lambda i: (i, 0, 0)),
        out_shape=jax.ShapeDtypeStruct((n, 1, dm), F32),
        compiler_params=_params("arbitrary"), name="post_sample_b",
    )(qc.reshape(n, 1, dm), mem_k, mem_v)

    return pl.pallas_call(
        _post_sample_c_kernel, out_shape=jax.ShapeDtypeStruct((n, dm), F32),
        compiler_params=_params(), name="post_sample_c",
    )(x1, oc.reshape(n, dm), w["w_ca_o"], w["g_ca_post"], w["g_mlp_pre"], w["w_ff_up"], w["w_ff_down"],
      w["g_mlp_post"])


def _dec_attn_kernel(n_samples, n_pages, n_heads, rope_dim,
                     pt_ref, qlat_ref, qrope_ref, ckvn_ref, krn_ref, cache_ckv, cache_krt,
                     o_ref, cbuf, kbuf, sems):
    ch = DEC_PAGES_PER_STEP
    nch = n_pages // ch
    total = n_samples * nch
    page, kvr = cbuf.shape[2], cbuf.shape[3]

    def copies(g, slot):
        b = g // nch
        c = g % nch
        out = []
        for i in range(ch):
            pg = pt_ref[b, c * ch + i]
            out.append(pltpu.make_async_copy(cache_ckv.at[pg], cbuf.at[slot, i], sems.at[0, slot]))
            out.append(pltpu.make_async_copy(cache_krt.at[pg], kbuf.at[slot, i], sems.at[1, slot]))
        return out

    def start(g):
        for n, cp in enumerate(copies(g, g % DEC_NBUF)):
            cp.start(priority=(n // 2) % 2)

    def wait(g):
        for cp in copies(g, g % DEC_NBUF):
            cp.wait()

    def q_rows(b):
        row0 = pl.multiple_of(b * n_heads, n_heads)
        return qlat_ref[pl.ds(row0, n_heads), :], qrope_ref[pl.ds(row0, n_heads), :][:, :rope_dim]

    def scores(g):
        ql, qr = q_rows(g // nch)
        slot = g % DEC_NBUF
        ckv = cbuf[slot].reshape(ch * page, kvr).astype(BF16)
        qrb = qr.astype(BF16)
        s_rope = jnp.concatenate(
            [jnp.dot(qrb, kbuf[slot, i].astype(BF16), preferred_element_type=F32) for i in range(ch)], axis=1)
        return lax.dot_general(ql.astype(BF16), ckv, (((1,), (1,)), ((), ())),
                               preferred_element_type=F32) + s_rope

    for g0 in range(DEC_NBUF - 1):
        start(g0)
    wait(0)
    s0 = scores(0)

    def body(g, carry):
        s_cur, m, l, acc = carry
        b = g // nch
        c = g % nch

        @pl.when(g + 1 < total)
        def _():
            wait(g + 1)

        @pl.when(g + (DEC_NBUF - 1) < total)
        def _():
            start(g + (DEC_NBUF - 1))

        s_next = scores(jnp.minimum(g + 1, total - 1))

        ql, qr = q_rows(b)
        c_new = ckvn_ref[pl.ds(b, 1), :]
        r_new = krn_ref[pl.ds(b, 1), :][:, :rope_dim]
        s_new = jnp.sum(ql * c_new, axis=-1, keepdims=True) + jnp.sum(qr * r_new, axis=-1, keepdims=True)
        first = c == 0
        m = jnp.where(first, s_new, m)
        l = jnp.where(first, 1.0, l)
        acc = jnp.where(first, jnp.broadcast_to(c_new, acc.shape), acc)

        ckv = cbuf[g % DEC_NBUF].reshape(ch * page, kvr).astype(BF16)
        m_new = jnp.maximum(m, jnp.max(s_cur, axis=-1, keepdims=True))
        alpha = jnp.exp(m - m_new)
        p = jnp.exp(s_cur - m_new)
        l = alpha * l + jnp.sum(p, axis=-1, keepdims=True)
        acc = alpha * acc + jnp.dot(p.astype(BF16), ckv, preferred_element_type=F32)
        o_ref[pl.ds(pl.multiple_of(b * n_heads, n_heads), n_heads), :] = acc / l
        return s_next, m_new, l, acc

    init = (s0, jnp.zeros((n_heads, 1), F32), jnp.zeros((n_heads, 1), F32), jnp.zeros((n_heads, kvr), F32))
    lax.fori_loop(0, total, body, init)


def _decode_attention(page_table, qlat, qrope, ckv_new, kr_new, cache_ckv, cache_krt, n_heads, rope_dim):
    n_samples, n_pages = page_table.shape
    page, kvr = cache_ckv.shape[1], cache_ckv.shape[2]
    ch = DEC_PAGES_PER_STEP
    assert n_pages % ch == 0

    def whole(shape):
        nd = len(shape)
        return pl.BlockSpec(shape, lambda i, pt: (0,) * nd)

    kern = functools.partial(_dec_attn_kernel, n_samples, n_pages, n_heads, rope_dim)
    grid_spec = pltpu.PrefetchScalarGridSpec(
        num_scalar_prefetch=1, grid=(1,),
        in_specs=[whole(qlat.shape), whole(qrope.shape), whole(ckv_new.shape), whole(kr_new.shape),
                  pl.BlockSpec(memory_space=pl.ANY), pl.BlockSpec(memory_space=pl.ANY)],
        out_specs=whole(qlat.shape),
        scratch_shapes=[pltpu.VMEM((DEC_NBUF, ch, page, kvr), F32),
                        pltpu.VMEM((DEC_NBUF, ch, rope_dim, page), F32),
                        pltpu.SemaphoreType.DMA((2, DEC_NBUF))])
    return pl.pallas_call(
        kern, grid_spec=grid_spec, out_shape=jax.ShapeDtypeStruct(qlat.shape, F32),
        compiler_params=_params("arbitrary"), name="decode_attn",
    )(page_table, qlat, qrope, ckv_new, kr_new, cache_ckv, cache_krt)


def _rot_half(w):
    half = w.shape[-1] // 2
    return jnp.concatenate([-w[..., half:], w[..., :half]], axis=-1)


def _mem_rows(cache):
    _, n, n_mem, heads, hd = cache.shape
    assert hd == 2 * LANES
    x = cache.reshape(n, n_mem, heads, 2, LANES)
    return jnp.transpose(x, (0, 1, 3, 2, 4)).reshape(n, n_mem * 2 * heads, LANES)


def _rope_lane_freqs(rope_dim):
    inv = 1.0 / (ROPE_BASE ** (jnp.arange(0, rope_dim, 2, dtype=F32) / rope_dim))
    return jnp.concatenate([inv, inv, jnp.zeros((LANES - rope_dim,), F32)])


def _rope_tabs(pos, rope_dim, nope_dim):
    ang = pos.astype(F32)[:, None] * _rope_lane_freqs(rope_dim)[None, :]
    keep = (jnp.arange(LANES) < rope_dim + nope_dim).astype(F32)
    return jnp.cos(ang) * keep, jnp.sin(ang)


def _rope_tabs_range(seq, rope_dim, nope_dim):
    assert seq % LANES == 0
    freqs = _rope_lane_freqs(rope_dim)
    a = (jnp.arange(seq // LANES, dtype=F32) * LANES)[:, None, None] * freqs
    b = jnp.arange(LANES, dtype=F32)[None, :, None] * freqs
    ca, sa, cb, sb = jnp.cos(a), jnp.sin(a), jnp.cos(b), jnp.sin(b)
    keep = (jnp.arange(LANES) < rope_dim + nope_dim).astype(F32)
    ctab = (ca * cb - sa * sb) * keep
    stab = sa * cb + ca * sb
    return ctab.reshape(seq, LANES), stab.reshape(seq, LANES)


def kernel(x_prompt, x_sample, mem_prompt, cache_ckv, cache_krope, state_conv, cache_mem_k, cache_mem_v,
           page_table, norm_mix_pre_g, w_in, conv_w, w_conv_out, q_norm_g, w_uq, kv_norm_g, w_uk, w_uv,
           w_mla_out, w_mix_out, norm_mix_post_g, norm_ca_pre_g, mem_norm_g, w_ca_q, w_ca_k, w_ca_v, w_ca_o,
           norm_ca_post_g, norm_mlp_pre_g, w_ff_up, w_ff_down, norm_mlp_post_g):
    depth = w_in.shape[0]
    assert depth == 1, "single-layer step"
    batch, seq, dm = x_prompt.shape
    n_s, t_s, _ = x_sample.shape
    assert t_s == 1
    conv_dim = conv_w.shape[2]
    q_rank, n_heads, qk_dim = w_uq.shape[1:]
    kv_rank, _, nope_dim = w_uk.shape[1:]
    v_dim = w_uv.shape[3]
    rope_dim = qk_dim - nope_dim
    n_mem, ca_heads, ca_hd = cache_mem_k.shape[2:]
    n_pool, page = cache_ckv.shape[1:3]
    past_len = page_table.shape[1] * page
    assert rope_dim + nope_dim <= LANES and v_dim <= LANES
    dims = _PreDims(dm, conv_dim, q_rank, kv_rank, n_heads)
    scale = float(qk_dim ** -0.5)

    w0 = w_in[0]
    o_kr = 3 * conv_dim + q_rank + kv_rank
    w_kr = w0[:, o_kr:o_kr + rope_dim]
    zpad = jnp.zeros((dm, LANES - rope_dim), F32)
    w_in_kr = jnp.concatenate([w_kr, zpad, _rot_half(w_kr), zpad], axis=1).astype(BF16)
    uq = w_uq[0]
    uq_nope, uq_rope = uq[:, :, :nope_dim], uq[:, :, nope_dim:]
    hz = lambda r, width: jnp.zeros((r, n_heads, width), F32)
    pad_tail = LANES - rope_dim - nope_dim
    w_uq_a = jnp.concatenate([uq_rope, uq_nope, hz(q_rank, pad_tail)], axis=2).reshape(q_rank, n_heads * LANES)
    assert LANES % rope_dim == 0
    w_uq_b = _rot_half(uq_rope).reshape(q_rank, n_heads * rope_dim)
    uk = w_uk[0]
    w_uk_pad = jnp.concatenate([hz(kv_rank, rope_dim), uk, hz(kv_rank, pad_tail)], axis=2).reshape(
        kv_rank, n_heads * LANES)
    w_uk_t_pad = jnp.transpose(w_uk_pad.reshape(kv_rank, n_heads, LANES), (1, 2, 0)).reshape(
        n_heads * LANES, kv_rank)
    uv = w_uv[0].reshape(kv_rank, n_heads * v_dim)
    dvp = v_dim + ATTN_V_EXTRA_ROWS
    w_uv_t = jnp.concatenate([jnp.transpose(w_uv[0], (1, 2, 0)),
                              jnp.zeros((n_heads, ATTN_V_EXTRA_ROWS, kv_rank), F32)], axis=1).reshape(
        n_heads * dvp, kv_rank)
    w_mla = w_mla_out[0]
    w = {
        "g_pre": norm_mix_pre_g, "w_in_a": w0[:, :o_kr].astype(BF16), "w_in_kr": w_in_kr,
        "w_in_g": w0[:, o_kr + rope_dim:].astype(BF16), "conv_w": conv_w[0], "w_conv_out": w_conv_out[0].astype(BF16),
        "q_g": q_norm_g, "w_uq_a": w_uq_a.astype(BF16), "w_uq_b": w_uq_b.astype(BF16), "kv_g": kv_norm_g,
        "w_uk_pad": w_uk_pad.astype(BF16), "w_uk_t_pad": w_uk_t_pad.astype(BF16),
        "w_uv_t": w_uv_t.astype(BF16), "w_uv_flat": uv.astype(BF16),
        "w_mla": w_mla.astype(BF16), "w_mix": w_mix_out[0].astype(BF16),
        "g_mix_post": norm_mix_post_g, "g_ca_pre": norm_ca_pre_g,
        "w_ca_q": w_ca_q[0].reshape(dm, ca_heads * ca_hd).astype(BF16),
        "w_ca_o": w_ca_o[0].reshape(ca_heads * ca_hd, dm).astype(BF16),
        "g_ca_post": norm_ca_post_g, "g_mlp_pre": norm_mlp_pre_g, "w_ff_up": w_ff_up[0].astype(BF16),
        "w_ff_down": w_ff_down[0].astype(BF16), "g_mlp_post": norm_mlp_post_g,
    }

    mk_p, mv_p = _memory_kv(mem_prompt.reshape(batch * n_mem, dm), mem_norm_g,
                            w_ca_k[0].reshape(dm, ca_heads * ca_hd).astype(BF16),
                            w_ca_v[0].reshape(dm, ca_heads * ca_hd).astype(BF16))
    ctab_p, stab_p = _rope_tabs_range(seq, rope_dim, nope_dim)
    x2d = x_prompt.reshape(batch * seq, dm)
    q, k, vt, ckv_p, kr_p, mc_p, sg_p, conv_p = _pre_prompt(dims, scale * LOG2_E, x2d, ctab_p, stab_p, w, batch,
                                                           seq, rope_dim, v_dim, dvp)
    o_p = _prompt_attention(q, k, vt, batch, seq, n_heads, v_dim)
    y_p = _post_prompt(x2d, o_p, mc_p, sg_p, mk_p, mv_p, w, batch, seq, ca_heads, n_mem)

    ctab_s, stab_s = _rope_tabs(jnp.full((n_s,), past_len, jnp.int32), rope_dim, nope_dim)
    xs = x_sample.reshape(n_s, dm)
    q_s, qlat_s, ckv_s, kr_s, mc_s, sg_s, conv_s = _pre_sample(
        dims, scale, xs, state_conv.reshape(n_s, (CONV_WIDTH - 1) * conv_dim), ctab_s, stab_s, w)
    olat = _decode_attention(page_table, qlat_s.reshape(n_s * n_heads, kv_rank),
                             q_s.reshape(n_s * n_heads, LANES), ckv_s, kr_s,
                             cache_ckv.reshape(n_pool, page, kv_rank),
                             jnp.swapaxes(cache_krope.reshape(n_pool, page, rope_dim), 1, 2), n_heads, rope_dim)
    y_s = _post_sample(xs, olat, mc_s, sg_s, _mem_rows(cache_mem_k), _mem_rows(cache_mem_v), w, n_heads, v_dim,
                       ca_heads)

    return (y_p.reshape(batch, seq, dm),
            y_s.reshape(n_s, t_s, dm),
            ckv_p.reshape(depth, batch, seq, kv_rank),
            kr_p.reshape(depth, batch, seq, rope_dim),
            conv_p.reshape(depth, batch, CONV_WIDTH - 1, conv_dim),
            mk_p.reshape(depth, batch, n_mem, ca_heads, ca_hd),
            mv_p.reshape(depth, batch, n_mem, ca_heads, ca_hd),
            ckv_s.reshape(depth, n_s, t_s, kv_rank),
            kr_s[:, :rope_dim].reshape(depth, n_s, t_s, rope_dim),
            conv_s.reshape(depth, n_s, CONV_WIDTH - 1, conv_dim))
```

```python
import functools

import numpy as np
import jax
import jax.numpy as jnp
from jax import lax
from jax.experimental import pallas as pl
from jax.experimental.pallas import tpu as pltpu

F32 = jnp.float32
BF16 = jnp.bfloat16

RMS_EPS = 1e-6
NEG_INF = -1e30
ROPE_BASE = 10000.0
CONV_WIDTH = 3
LOG2_E = 1.4426950408889634

LANES = 128
SUBLANES = 8
VMEM_LIMIT_BYTES = 56 * 1024 * 1024

ROW_TILE = 512
ATTN_TQ = 1024
ATTN_TK = 512
ATTN_V_EXTRA_ROWS = 16
DEC_PAGES_PER_STEP = 64
DEC_NBUF = 3
CA_SAMPLES_PER_STEP = 8


def _rms(x, g):
    return x * lax.rsqrt(jnp.mean(x * x, axis=-1, keepdims=True) + RMS_EPS) * g


def _bdot(a, b):
    return jnp.dot(a.astype(BF16), b.astype(BF16), preferred_element_type=F32)


def _bdot_nt(a, b):
    return lax.dot_general(a.astype(BF16), b.astype(BF16), (((1,), (1,)), ((), ())),
                           preferred_element_type=F32)


def _softmax_rows(s):
    m = jnp.max(s, axis=-1, keepdims=True)
    p = jnp.exp(s - m)
    return p / jnp.sum(p, axis=-1, keepdims=True)


def _const_spec(shape):
    nd = len(shape)
    return pl.BlockSpec(shape, lambda *_: (0,) * nd, pipeline_mode=pl.Buffered(1))


def _params(*sem):
    return pltpu.CompilerParams(dimension_semantics=tuple(sem) if sem else None,
                                vmem_limit_bytes=VMEM_LIMIT_BYTES)


def _memkv_kernel(mem_ref, g_ref, wk_ref, wv_ref, k_ref, v_ref):
    mn = _rms(mem_ref[...], g_ref[...]).astype(BF16)
    k_ref[...] = jnp.dot(mn, wk_ref[...], preferred_element_type=F32)
    v_ref[...] = jnp.dot(mn, wv_ref[...], preferred_element_type=F32)


def _memory_kv(mem2d, g, wk, wv):
    rows, d = mem2d.shape
    out = jax.ShapeDtypeStruct((rows, wk.shape[1]), F32)
    return pl.pallas_call(_memkv_kernel, out_shape=(out, out), name="mem_kv",
                          compiler_params=_params())(mem2d, g, wk, wv)


class _PreDims:
    def __init__(self, d_model, conv_dim, q_rank, kv_rank, n_heads):
        self.d_model, self.conv_dim, self.q_rank, self.kv_rank = d_model, conv_dim, q_rank, kv_rank
        self.n_heads = n_heads
        c = conv_dim
        self.o_h, self.o_gb, self.o_gc = 0, c, 2 * c
        self.o_cq = 3 * c
        self.o_ckv = self.o_cq + q_rank
        self.n_a = self.o_ckv + kv_rank
        self.hw = n_heads * LANES


def _pre_common(dims, x, ctab, stab, gpre_ref, wa_ref, wkr_ref, wg_ref, qg_ref, wuqa_ref, wuqb_ref, kvg_ref,
                scale):
    d = dims
    xn = _rms(x, gpre_ref[...]).astype(BF16)

    def proj(w_ref, lo, hi):
        return jnp.dot(xn, w_ref[:, lo:hi], preferred_element_type=F32)

    h = proj(wa_ref, d.o_h, d.o_gb)
    gate_b = proj(wa_ref, d.o_gb, d.o_gc)
    gate_c = proj(wa_ref, d.o_gc, d.o_cq)
    cq = proj(wa_ref, d.o_cq, d.o_ckv)
    ckv = proj(wa_ref, d.o_ckv, d.n_a)
    kr2 = proj(wkr_ref, 0, 2 * LANES)
    kra, krb = kr2[:, :LANES], kr2[:, LANES:]
    g_conv = proj(wg_ref, 0, d.d_model)
    g_mla = proj(wg_ref, d.d_model, 2 * d.d_model)

    u = gate_c * h
    cqn = _rms(cq, qg_ref[...]).astype(BF16)
    qa = jnp.dot(cqn, wuqa_ref[...], preferred_element_type=F32)
    qbc = jnp.dot(cqn, wuqb_ref[...], preferred_element_type=F32)
    r = wuqb_ref.shape[1] // d.n_heads
    pieces = []
    for hd in range(d.n_heads):
        blk, off = divmod(hd * r, LANES)
        piece = qbc[:, blk * LANES:(blk + 1) * LANES]
        pieces.append(pltpu.roll(piece, LANES - off, 1) if off else piece)
    qb = jnp.concatenate(pieces, axis=1)
    ct = jnp.concatenate([ctab] * d.n_heads, axis=1)
    st = jnp.concatenate([stab] * d.n_heads, axis=1)
    q = (qa * ct + qb * st) * scale
    ckvn = _rms(ckv, kvg_ref[...])
    krr = kra * ctab + krb * stab
    return u, gate_b, g_conv, g_mla, q, ckvn, krr


def _pre_prompt_kernel(dims, scale, tm, dv, dvp,
                       x_ref, ctab_ref, stab_ref, gpre_ref, wa_ref, wkr_ref, wg_ref, convw_ref, wco_ref, qg_ref,
                       wuqa_ref, wuqb_ref, kvg_ref, wuk_ref, wuvt_ref,
                       q_out, k_out, vt_out, ckv_out, kr_out, mc_out, sg_out, conv_out, ubuf):
    hist = SUBLANES
    u, gate_b, g_conv, g_mla, q, ckvn, krr = _pre_common(
        dims, x_ref[...], ctab_ref[...], stab_ref[...], gpre_ref, wa_ref, wkr_ref, wg_ref, qg_ref, wuqa_ref,
        wuqb_ref, kvg_ref, scale)

    @pl.when(pl.program_id(1) == 0)
    def _():
        ubuf[0:hist, :] = jnp.zeros((hist, dims.conv_dim), F32)

    ubuf[hist:hist + tm, :] = u
    u1 = ubuf[hist - 1:hist - 1 + tm, :]
    u2 = ubuf[hist - 2:hist - 2 + tm, :]
    cw = convw_ref[...]
    conv = u2 * cw[0:1, :] + u1 * cw[1:2, :] + u * cw[2:3, :]
    y_conv = _bdot(gate_b * conv, wco_ref[...])
    mc_out[...] = (jax.nn.sigmoid(g_conv) * y_conv).astype(mc_out.dtype)
    sg_out[...] = jax.nn.sigmoid(g_mla).astype(sg_out.dtype)
    conv_out[...] = ubuf[hist + tm - (CONV_WIDTH - 1):hist + tm, :]
    ubuf[0:hist, :] = ubuf[tm:tm + hist, :]

    q_out[...] = q.astype(q_out.dtype)
    ckv_out[...] = ckvn
    kr_out[...] = krr[:, :kr_out.shape[-1]]
    ckvb = ckvn.astype(BF16)
    ka = jnp.dot(ckvb, wuk_ref[...], preferred_element_type=F32)
    k = ka + jnp.concatenate([krr] * dims.n_heads, axis=1)
    k_out[...] = k.astype(k_out.dtype)
    vt = lax.dot_general(wuvt_ref[...], ckvb, (((1,), (1,)), ((), ())), preferred_element_type=F32)
    rid = lax.broadcasted_iota(jnp.int32, (vt.shape[0], 1), 0)
    vt_out[...] = (vt + jnp.where(rid % dvp == dv, 1.0, 0.0)).astype(vt_out.dtype)


def _pre_prompt(dims, scale, x2d, ctab, stab, w, batch, seq, rope_dim, dv, dvp):
    tm = ROW_TILE
    assert seq % tm == 0
    nt = seq // tm
    rows = batch * seq
    d = dims
    hv = w["w_uv_t"].shape[0]

    def row_spec(width):
        return pl.BlockSpec((tm, width), lambda b, t: (b * nt + t, 0))

    def tab_spec():
        return pl.BlockSpec((tm, LANES), lambda b, t: (t, 0))

    in_specs = [row_spec(d.d_model), tab_spec(), tab_spec(),
                _const_spec((1, d.d_model)), _const_spec((d.d_model, d.n_a)),
                _const_spec((d.d_model, 2 * LANES)), _const_spec((d.d_model, 2 * d.d_model)),
                _const_spec((CONV_WIDTH, d.conv_dim)), _const_spec((d.conv_dim, d.d_model)),
                _const_spec((1, d.q_rank)), _const_spec((d.q_rank, d.hw)),
                _const_spec((d.q_rank, d.n_heads * rope_dim)),
                _const_spec((1, d.kv_rank)), _const_spec((d.kv_rank, d.hw)), _const_spec((hv, d.kv_rank))]
    out_shape = (jax.ShapeDtypeStruct((rows, d.hw), BF16),
                 jax.ShapeDtypeStruct((rows, d.hw), BF16),
                 jax.ShapeDtypeStruct((batch, hv, seq), BF16),
                 jax.ShapeDtypeStruct((rows, d.kv_rank), F32),
                 jax.ShapeDtypeStruct((rows, rope_dim), F32),
                 jax.ShapeDtypeStruct((rows, d.d_model), BF16),
                 jax.ShapeDtypeStruct((rows, d.d_model), BF16),
                 jax.ShapeDtypeStruct((batch, CONV_WIDTH - 1, d.conv_dim), F32))
    out_specs = (row_spec(d.hw), row_spec(d.hw),
                 pl.BlockSpec((None, hv, tm), lambda b, t: (b, 0, t)), row_spec(d.kv_rank),
                 row_spec(rope_dim), row_spec(d.d_model), row_spec(d.d_model),
                 pl.BlockSpec((None, CONV_WIDTH - 1, d.conv_dim), lambda b, t: (b, 0, 0)))
    kern = functools.partial(_pre_prompt_kernel, dims, scale, tm, dv, dvp)
    return pl.pallas_call(
        kern, grid=(batch, nt), in_specs=in_specs, out_specs=out_specs, out_shape=out_shape,
        scratch_shapes=[pltpu.VMEM((tm + SUBLANES, d.conv_dim), F32)],
        compiler_params=_params("arbitrary", "arbitrary"), name="pre_prompt",
    )(x2d, ctab, stab, w["g_pre"], w["w_in_a"], w["w_in_kr"], w["w_in_g"], w["conv_w"], w["w_conv_out"], w["q_g"],
      w["w_uq_a"], w["w_uq_b"], w["kv_g"], w["w_uk_pad"], w["w_uv_t"])


def _pre_sample_kernel(dims, scale,
                       x_ref, state_ref, ctab_ref, stab_ref, gpre_ref, wa_ref, wkr_ref, wg_ref, convw_ref, wco_ref,
                       qg_ref, wuqa_ref, wuqb_ref, kvg_ref, wukt_ref,
                       q_out, qlat_out, ckv_out, kr_out, mc_out, sg_out, conv_out):
    c = dims.conv_dim
    u, gate_b, g_conv, g_mla, q, ckvn, krr = _pre_common(
        dims, x_ref[...], ctab_ref[...], stab_ref[...], gpre_ref, wa_ref, wkr_ref, wg_ref, qg_ref, wuqa_ref,
        wuqb_ref, kvg_ref, scale)
    u2 = state_ref[:, 0:c]
    u1 = state_ref[:, c:2 * c]
    cw = convw_ref[...]
    conv = u2 * cw[0:1, :] + u1 * cw[1:2, :] + u * cw[2:3, :]
    y_conv = _bdot(gate_b * conv, wco_ref[...])
    mc_out[...] = jax.nn.sigmoid(g_conv) * y_conv
    sg_out[...] = jax.nn.sigmoid(g_mla)
    conv_out[:, 0:c] = u1
    conv_out[:, c:2 * c] = u
    q_out[...] = q
    ckv_out[...] = ckvn
    kr_out[...] = krr
    kvr = dims.kv_rank
    qb16 = q.astype(BF16)
    for hd in range(dims.n_heads):
        blk = qb16[:, hd * LANES:(hd + 1) * LANES]
        qlat_out[:, hd * kvr:(hd + 1) * kvr] = jnp.dot(
            blk, wukt_ref[hd * LANES:(hd + 1) * LANES, :], preferred_element_type=F32)


def _pre_sample(dims, scale, xs, state2d, ctab, stab, w):
    n = xs.shape[0]
    d = dims
    out_shape = (jax.ShapeDtypeStruct((n, d.hw), F32),
                 jax.ShapeDtypeStruct((n, d.n_heads * d.kv_rank), F32),
                 jax.ShapeDtypeStruct((n, d.kv_rank), F32),
                 jax.ShapeDtypeStruct((n, LANES), F32),
                 jax.ShapeDtypeStruct((n, d.d_model), F32),
                 jax.ShapeDtypeStruct((n, d.d_model), F32),
                 jax.ShapeDtypeStruct((n, (CONV_WIDTH - 1) * d.conv_dim), F32))
    kern = functools.partial(_pre_sample_kernel, dims, scale)
    return pl.pallas_call(kern, out_shape=out_shape, compiler_params=_params(), name="pre_sample")(
        xs, state2d, ctab, stab, w["g_pre"], w["w_in_a"], w["w_in_kr"], w["w_in_g"], w["conv_w"], w["w_conv_out"],
        w["q_g"],
        w["w_uq_a"], w["w_uq_b"], w["kv_g"], w["w_uk_t_pad"])


def _attn_kernel(tq, tk, hp, dv, dvp, q_ref, k_ref, vt_ref, o_ref, sa_ref, sb_ref):
    qi = pl.program_id(2)
    assert tq == 2 * tk
    half = tq // 2

    def scores(j, dst, q_lo=0):
        start = pl.multiple_of(j * tk, tk)
        for h in range(hp):
            kh = k_ref[pl.ds(start, tk), h * LANES:(h + 1) * LANES]
            qh = q_ref[q_lo:tq, h * LANES:(h + 1) * LANES]
            dst[h, :, 0:tq - q_lo] = lax.dot_general(kh, qh, (((1,), (1,)), ((), ())),
                                                     preferred_element_type=F32)

    def process(j, src, state, masked, q_lo=0):
        start = pl.multiple_of(j * tk, tk)
        n = tq - q_lo
        new_state = []
        for h in range(hp):
            m_prev, acc = state[h]
            st = src[h, :, 0:n]
            if masked:
                key = lax.broadcasted_iota(jnp.int32, (tk, n), 0) + j * tk
                qry = lax.broadcasted_iota(jnp.int32, (tk, n), 1) + qi * tq + q_lo
                st = jnp.where(key <= qry, st, NEG_INF)
            m_new = jnp.maximum(m_prev[:, q_lo:], jnp.max(st, axis=0, keepdims=True))
            alpha = jnp.exp2(m_prev[:, q_lo:] - m_new)
            p = jnp.exp2(st - m_new).astype(BF16)
            vth = vt_ref[h * dvp:(h + 1) * dvp, pl.ds(start, tk)]
            acc_new = alpha * acc[:, q_lo:] + jnp.dot(vth, p, preferred_element_type=F32)
            if q_lo:
                m_new = jnp.concatenate([m_prev[:, :q_lo], m_new], axis=1)
                acc_new = jnp.concatenate([acc[:, :q_lo], acc_new], axis=1)
            new_state.append((m_new, acc_new))
        return tuple(new_state)

    init = tuple((jnp.full((1, tq), NEG_INF, F32), jnp.zeros((dvp, tq), F32)) for _ in range(hp))
    scores(0, sa_ref)

    def pair(t, state):
        scores(2 * t + 1, sb_ref)
        state = process(2 * t, sa_ref, state, False)
        scores(2 * t + 2, sa_ref)
        return process(2 * t + 1, sb_ref, state, False)

    state = lax.fori_loop(0, qi, pair, init)
    scores(2 * qi + 1, sb_ref, q_lo=half)
    state = process(2 * qi, sa_ref, state, True)
    state = process(2 * qi + 1, sb_ref, state, True, q_lo=half)
    ot = jnp.concatenate([acc[0:dv] / acc[dv:dv + 1] for (_, acc) in state], axis=0)
    o_ref[...] = ot.T.astype(o_ref.dtype)


def _prompt_attention(q, k, vt, batch, seq, n_heads, dv):
    tq, tk = ATTN_TQ, ATTN_TK
    assert seq % tq == 0 and tq % tk == 0
    hp = LANES // dv
    assert n_heads % hp == 0
    dvp = vt.shape[1] // n_heads
    nq = seq // tq
    kern = functools.partial(_attn_kernel, tq, tk, hp, dv, dvp)
    return pl.pallas_call(
        kern, grid=(batch, n_heads // hp, nq),
        in_specs=[pl.BlockSpec((tq, hp * LANES), lambda b, g, i: (b * nq + i, g)),
                  pl.BlockSpec((seq, hp * LANES), lambda b, g, i: (b, g)),
                  pl.BlockSpec((None, hp * dvp, seq), lambda b, g, i: (b, g, 0))],
        out_specs=pl.BlockSpec((tq, hp * dv), lambda b, g, i: (b * nq + i, g)),
        out_shape=jax.ShapeDtypeStruct((batch * seq, n_heads * dv), BF16),
        scratch_shapes=[pltpu.VMEM((hp, tk, tq), F32), pltpu.VMEM((hp, tk, tq), F32)],
        compiler_params=_params("arbitrary", "arbitrary", "arbitrary"), name="prompt_attn",
    )(q, k, vt)


def _mix_and_query(x, o, mc, sg, wmla_ref, wmix_ref, gmixpost_ref, gcapre_ref, wcaq_ref, ca_scale):
    y_mla = _bdot(o, wmla_ref[...])
    merged = mc.astype(F32) + sg.astype(F32) * y_mla
    y = _bdot(merged, wmix_ref[...])
    x1 = x + _rms(y, gmixpost_ref[...])
    qc = _bdot(_rms(x1, gcapre_ref[...]), wcaq_ref[...]) * ca_scale
    return x1, qc


def _mlp_tail(x1, oc, wcao_ref, gcapost_ref, gmlppre_ref, wup_ref, wdown_ref, gmlppost_ref):
    ca = _bdot(oc, wcao_ref[...])
    x2 = x1 + _rms(ca, gcapost_ref[...])
    hid = jnp.square(jnp.maximum(_bdot(_rms(x2, gmlppre_ref[...]), wup_ref[...]), 0.0))
    return x2 + _rms(_bdot(hid, wdown_ref[...]), gmlppost_ref[...])


def _post_prompt_kernel(ca_heads, ca_scale,
                        x_ref, o_ref, mc_ref, sg_ref, mk_ref, mv_ref,
                        wmla_ref, wmix_ref, gmixpost_ref, gcapre_ref, wcaq_ref, wcao_ref, gcapost_ref,
                        gmlppre_ref, wup_ref, wdown_ref, gmlppost_ref, y_ref):
    x1, qc = _mix_and_query(x_ref[...], o_ref[...], mc_ref[...], sg_ref[...], wmla_ref, wmix_ref,
                            gmixpost_ref, gcapre_ref, wcaq_ref, ca_scale)
    hd = qc.shape[1] // ca_heads
    outs = []
    for h in range(ca_heads):
        sl = slice(h * hd, (h + 1) * hd)
        p = _softmax_rows(_bdot_nt(qc[:, sl], mk_ref[:, sl]))
        outs.append(_bdot(p, mv_ref[:, sl]))
    oc = jnp.concatenate(outs, axis=1)
    y_ref[...] = _mlp_tail(x1, oc, wcao_ref, gcapost_ref, gmlppre_ref, wup_ref, wdown_ref, gmlppost_ref)


def _post_prompt(x2d, o, mc, sg, mk, mv, w, batch, seq, ca_heads, n_mem):
    tm = ROW_TILE
    nt = seq // tm
    rows, dm = x2d.shape
    dff = w["w_ff_up"].shape[1]
    ca_scale = float((dm // ca_heads) ** -0.5)

    def row_spec(width):
        return pl.BlockSpec((tm, width), lambda i: (i, 0))

    mem_spec = pl.BlockSpec((n_mem, dm), lambda i: (i // nt, 0))
    in_specs = [row_spec(dm), row_spec(o.shape[1]), row_spec(dm), row_spec(dm), mem_spec, mem_spec,
                _const_spec((o.shape[1], dm)), _const_spec((dm, dm)), _const_spec((1, dm)),
                _const_spec((1, dm)), _const_spec((dm, dm)), _const_spec((dm, dm)), _const_spec((1, dm)),
                _const_spec((1, dm)), _const_spec((dm, dff)), _const_spec((dff, dm)), _const_spec((1, dm))]
    kern = functools.partial(_post_prompt_kernel, ca_heads, ca_scale)
    return pl.pallas_call(
        kern, grid=(rows // tm,), in_specs=in_specs, out_specs=row_spec(dm),
        out_shape=jax.ShapeDtypeStruct((rows, dm), F32),
        compiler_params=_params("arbitrary"), name="post_prompt",
    )(x2d, o, mc, sg, mk, mv, w["w_mla"], w["w_mix"], w["g_mix_post"], w["g_ca_pre"], w["w_ca_q"],
      w["w_ca_o"], w["g_ca_post"], w["g_mlp_pre"], w["w_ff_up"], w["w_ff_down"], w["g_mlp_post"])


def _post_sample_a_kernel(n_heads, v_dim, ca_scale,
                          x_ref, olat_ref, mc_ref, sg_ref, wuv_ref, wmla_ref, wmix_ref, gmixpost_ref,
                          gcapre_ref, wcaq_ref, x1_ref, qc_ref):
    n = x_ref.shape[0]
    full = _bdot(olat_ref[...], wuv_ref[...])
    full = full.reshape(n, n_heads, n_heads * v_dim)
    hidx = lax.broadcasted_iota(jnp.int32, full.shape, 1)
    lane_head = lax.broadcasted_iota(jnp.int32, full.shape, 2) // v_dim
    o = jnp.sum(jnp.where(hidx == lane_head, full, 0.0), axis=1)
    x1, qc = _mix_and_query(x_ref[...], o, mc_ref[...], sg_ref[...], wmla_ref, wmix_ref,
                            gmixpost_ref, gcapre_ref, wcaq_ref, ca_scale)
    x1_ref[...] = x1
    qc_ref[...] = qc


def _post_sample_b_kernel(ca_heads, q_ref, mk_ref, mv_ref, o_ref):
    g = q_ref.shape[0]
    rows = mk_ref.shape[1]
    s8 = 2 * ca_heads
    assert s8 == SUBLANES and mk_ref.shape[2] == LANES
    lane = lax.broadcasted_iota(jnp.int32, (s8, rows), 1)
    sub = lax.broadcasted_iota(jnp.int32, (s8, rows), 0)
    own = (lane % s8) == sub
    low_half = (lax.broadcasted_iota(jnp.int32, (1, rows), 1) % s8) < ca_heads
    for i in range(g):
        q = q_ref[i]
        q8 = jnp.concatenate([q[:, (2 * (j % ca_heads) + j // ca_heads) * LANES:
                                   (2 * (j % ca_heads) + j // ca_heads + 1) * LANES] for j in range(s8)], axis=0)
        kb = mk_ref[i].astype(BF16)
        p_all = lax.dot_general(q8.astype(BF16), kb, (((1,), (1,)), ((), ())), preferred_element_type=F32)
        d = jnp.sum(jnp.where(own, p_all, 0.0), axis=0, keepdims=True)
        d = d + jnp.where(low_half, pltpu.roll(d, rows - ca_heads, 1), pltpu.roll(d, ca_heads, 1))
        dm = jnp.where(own, jnp.broadcast_to(d, (s8, rows)), NEG_INF)
        e = jnp.exp(dm - jnp.max(dm, axis=-1, keepdims=True))
        w = (e / jnp.sum(e, axis=-1, keepdims=True)).astype(BF16)
        o8 = jnp.dot(w, mv_ref[i].astype(BF16), preferred_element_type=F32)
        for j in range(s8):
            blk = 2 * (j % ca_heads) + j // ca_heads
            o_ref[i, :, blk * LANES:(blk + 1) * LANES] = o8[j:j + 1, :]


def _post_sample_c_kernel(x1_ref, oc_ref, wcao_ref, gcapost_ref, gmlppre_ref, wup_ref, wdown_ref,
                          gmlppost_ref, y_ref):
    y_ref[...] = _mlp_tail(x1_ref[...], oc_ref[...], wcao_ref, gcapost_ref, gmlppre_ref, wup_ref,
                           wdown_ref, gmlppost_ref)


def _post_sample(xs, olat, mc, sg, mem_k, mem_v, w, n_heads, v_dim, ca_heads):
    n, dm = xs.shape
    ca_scale = float((dm // ca_heads) ** -0.5)
    kern_a = functools.partial(_post_sample_a_kernel, n_heads, v_dim, ca_scale)
    x1, qc = pl.pallas_call(
        kern_a, out_shape=(jax.ShapeDtypeStruct((n, dm), F32), jax.ShapeDtypeStruct((n, dm), F32)),
        compiler_params=_params(), name="post_sample_a",
    )(xs, olat, mc, sg, w["w_uv_flat"], w["w_mla"], w["w_mix"], w["g_mix_post"], w["g_ca_pre"], w["w_ca_q"])

    g = CA_SAMPLES_PER_STEP
    assert n % g == 0
    kern_b = functools.partial(_post_sample_b_kernel, ca_heads)
    mem_spec = pl.BlockSpec((g,) + mem_k.shape[1:], lambda i: (i, 0, 0))
    oc = pl.pallas_call(
        kern_b, grid=(n // g,),
        in_specs=[pl.BlockSpec((g, 1, dm), lambda i: (i, 0, 0)), mem_spec, mem_spec],
        out_specs=pl.BlockSpec((g, 1, dm), lambda i: (i, 0, 0)),
        out_shape=jax.ShapeDtypeStruct((n, 1, dm), F32),
        compiler_params=_params("arbitrary"), name="post_sample_b",
    )(qc.reshape(n, 1, dm), mem_k, mem_v)

    return pl.pallas_call(
        _post_sample_c_kernel, out_shape=jax.ShapeDtypeStruct((n, dm), F32),
        compiler_params=_params(), name="post_sample_c",
    )(x1, oc.reshape(n, dm), w["w_ca_o"], w["g_ca_post"], w["g_mlp_pre"], w["w_ff_up"], w["w_ff_down"],
      w["g_mlp_post"])


def _dec_attn_kernel(n_samples, n_pages, n_heads, rope_dim,
                     pt_ref, qlat_ref, qrope_ref, ckvn_ref, krn_ref, cache_ckv, cache_krt,
                     o_ref, cbuf, kbuf, sems):
    ch = DEC_PAGES_PER_STEP
    nch = n_pages // ch
    total = n_samples * nch
    page, kvr = cbuf.shape[2], cbuf.shape[3]

    def copies(g, slot):
        b = g // nch
        c = g % nch
        out = []
        for i in range(ch):
            pg = pt_ref[b, c * ch + i]
            out.append(pltpu.make_async_copy(cache_ckv.at[pg], cbuf.at[slot, i], sems.at[0, slot]))
            out.append(pltpu.make_async_copy(cache_krt.at[pg], kbuf.at[slot, i], sems.at[1, slot]))
        return out

    def start(g):
        for n, cp in enumerate(copies(g, g % DEC_NBUF)):
            cp.start(priority=(n // 2) % 2)

    def wait(g):
        for cp in copies(g, g % DEC_NBUF):
            cp.wait()

    def q_rows(b):
        row0 = pl.multiple_of(b * n_heads, n_heads)
        return qlat_ref[pl.ds(row0, n_heads), :], qrope_ref[pl.ds(row0, n_heads), :][:, :rope_dim]

    def scores(g):
        ql, qr = q_rows(g // nch)
        slot = g % DEC_NBUF
        ckv = cbuf[slot].reshape(ch * page, kvr).astype(BF16)
        qrb = qr.astype(BF16)
        s_rope = jnp.concatenate(
            [jnp.dot(qrb, kbuf[slot, i].astype(BF16), preferred_element_type=F32) for i in range(ch)], axis=1)
        return lax.dot_general(ql.astype(BF16), ckv, (((1,), (1,)), ((), ())),
                               preferred_element_type=F32) + s_rope

    for g0 in range(DEC_NBUF - 1):
        start(g0)
    wait(0)
    s0 = scores(0)

    def body(g, carry):
        s_cur, m, l, acc = carry
        b = g // nch
        c = g % nch

        @pl.when(g + 1 < total)
        def _():
            wait(g + 1)

        @pl.when(g + (DEC_NBUF - 1) < total)
        def _():
            start(g + (DEC_NBUF - 1))

        s_next = scores(jnp.minimum(g + 1, total - 1))

        ql, qr = q_rows(b)
        c_new = ckvn_ref[pl.ds(b, 1), :]
        r_new = krn_ref[pl.ds(b, 1), :][:, :rope_dim]
        s_new = jnp.sum(ql * c_new, axis=-1, keepdims=True) + jnp.sum(qr * r_new, axis=-1, keepdims=True)
        first = c == 0
        m = jnp.where(first, s_new, m)
        l = jnp.where(first, 1.0, l)
        acc = jnp.where(first, jnp.broadcast_to(c_new, acc.shape), acc)

        ckv = cbuf[g % DEC_NBUF].reshape(ch * page, kvr).astype(BF16)
        m_new = jnp.maximum(m, jnp.max(s_cur, axis=-1, keepdims=True))
        alpha = jnp.exp(m - m_new)
        p = jnp.exp(s_cur - m_new)
        l = alpha * l + jnp.sum(p, axis=-1, keepdims=True)
        acc = alpha * acc + jnp.dot(p.astype(BF16), ckv, preferred_element_type=F32)
        o_ref[pl.ds(pl.multiple_of(b * n_heads, n_heads), n_heads), :] = acc / l
        return s_next, m_new, l, acc

    init = (s0, jnp.zeros((n_heads, 1), F32), jnp.zeros((n_heads, 1), F32), jnp.zeros((n_heads, kvr), F32))
    lax.fori_loop(0, total, body, init)


def _decode_attention(page_table, qlat, qrope, ckv_new, kr_new, cache_ckv, cache_krt, n_heads, rope_dim):
    n_samples, n_pages = page_table.shape
    page, kvr = cache_ckv.shape[1], cache_ckv.shape[2]
    ch = DEC_PAGES_PER_STEP
    assert n_pages % ch == 0

    def whole(shape):
        nd = len(shape)
        return pl.BlockSpec(shape, lambda i, pt: (0,) * nd)

    kern = functools.partial(_dec_attn_kernel, n_samples, n_pages, n_heads, rope_dim)
    grid_spec = pltpu.PrefetchScalarGridSpec(
        num_scalar_prefetch=1, grid=(1,),
        in_specs=[whole(qlat.shape), whole(qrope.shape), whole(ckv_new.shape), whole(kr_new.shape),
                  pl.BlockSpec(memory_space=pl.ANY), pl.BlockSpec(memory_space=pl.ANY)],
        out_specs=whole(qlat.shape),
        scratch_shapes=[pltpu.VMEM((DEC_NBUF, ch, page, kvr), F32),
                        pltpu.VMEM((DEC_NBUF, ch, rope_dim, page), F32),
                        pltpu.SemaphoreType.DMA((2, DEC_NBUF))])
    return pl.pallas_call(
        kern, grid_spec=grid_spec, out_shape=jax.ShapeDtypeStruct(qlat.shape, F32),
        compiler_params=_params("arbitrary"), name="decode_attn",
    )(page_table, qlat, qrope, ckv_new, kr_new, cache_ckv, cache_krt)


def _rot_half(w):
    half = w.shape[-1] // 2
    return jnp.concatenate([-w[..., half:], w[..., :half]], axis=-1)


def _mem_rows(cache):
    _, n, n_mem, heads, hd = cache.shape
    assert hd == 2 * LANES
    x = cache.reshape(n, n_mem, heads, 2, LANES)
    return jnp.transpose(x, (0, 1, 3, 2, 4)).reshape(n, n_mem * 2 * heads, LANES)


def _rope_lane_freqs(rope_dim):
    inv = 1.0 / (ROPE_BASE ** (jnp.arange(0, rope_dim, 2, dtype=F32) / rope_dim))
    return jnp.concatenate([inv, inv, jnp.zeros((LANES - rope_dim,), F32)])


def _rope_tabs(pos, rope_dim, nope_dim):
    ang = pos.astype(F32)[:, None] * _rope_lane_freqs(rope_dim)[None, :]
    keep = (jnp.arange(LANES) < rope_dim + nope_dim).astype(F32)
    return jnp.cos(ang) * keep, jnp.sin(ang)


def _rope_tabs_range(seq, rope_dim, nope_dim):
    assert seq % LANES == 0
    freqs = _rope_lane_freqs(rope_dim)
    a = (jnp.arange(seq // LANES, dtype=F32) * LANES)[:, None, None] * freqs
    b = jnp.arange(LANES, dtype=F32)[None, :, None] * freqs
    ca, sa, cb, sb = jnp.cos(a), jnp.sin(a), jnp.cos(b), jnp.sin(b)
    keep = (jnp.arange(LANES) < rope_dim + nope_dim).astype(F32)
    ctab = (ca * cb - sa * sb) * keep
    stab = sa * cb + ca * sb
    return ctab.reshape(seq, LANES), stab.reshape(seq, LANES)


def kernel(x_prompt, x_sample, mem_prompt, cache_ckv, cache_krope, state_conv, cache_mem_k, cache_mem_v,
           page_table, norm_mix_pre_g, w_in, conv_w, w_conv_out, q_norm_g, w_uq, kv_norm_g, w_uk, w_uv,
           w_mla_out, w_mix_out, norm_mix_post_g, norm_ca_pre_g, mem_norm_g, w_ca_q, w_ca_k, w_ca_v, w_ca_o,
           norm_ca_post_g, norm_mlp_pre_g, w_ff_up, w_ff_down, norm_mlp_post_g):
    depth = w_in.shape[0]
    assert depth == 1, "single-layer step"
    batch, seq, dm = x_prompt.shape
    n_s, t_s, _ = x_sample.shape
    assert t_s == 1
    conv_dim = conv_w.shape[2]
    q_rank, n_heads, qk_dim = w_uq.shape[1:]
    kv_rank, _, nope_dim = w_uk.shape[1:]
    v_dim = w_uv.shape[3]
    rope_dim = qk_dim - nope_dim
    n_mem, ca_heads, ca_hd = cache_mem_k.shape[2:]
    n_pool, page = cache_ckv.shape[1:3]
    past_len = page_table.shape[1] * page
    assert rope_dim + nope_dim <= LANES and v_dim <= LANES
    dims = _PreDims(dm, conv_dim, q_rank, kv_rank, n_heads)
    scale = float(qk_dim ** -0.5)

    w0 = w_in[0]
    o_kr = 3 * conv_dim + q_rank + kv_rank
    w_kr = w0[:, o_kr:o_kr + rope_dim]
    zpad = jnp.zeros((dm, LANES - rope_dim), F32)
    w_in_kr = jnp.concatenate([w_kr, zpad, _rot_half(w_kr), zpad], axis=1).astype(BF16)
    uq = w_uq[0]
    uq_nope, uq_rope = uq[:, :, :nope_dim], uq[:, :, nope_dim:]
    hz = lambda r, width: jnp.zeros((r, n_heads, width), F32)
    pad_tail = LANES - rope_dim - nope_dim
    w_uq_a = jnp.concatenate([uq_rope, uq_nope, hz(q_rank, pad_tail)], axis=2).reshape(q_rank, n_heads * LANES)
    assert LANES % rope_dim == 0
    w_uq_b = _rot_half(uq_rope).reshape(q_rank, n_heads * rope_dim)
    uk = w_uk[0]
    w_uk_pad = jnp.concatenate([hz(kv_rank, rope_dim), uk, hz(kv_rank, pad_tail)], axis=2).reshape(
        kv_rank, n_heads * LANES)
    w_uk_t_pad = jnp.transpose(w_uk_pad.reshape(kv_rank, n_heads, LANES), (1, 2, 0)).reshape(
        n_heads * LANES, kv_rank)
    uv = w_uv[0].reshape(kv_rank, n_heads * v_dim)
    dvp = v_dim + ATTN_V_EXTRA_ROWS
    w_uv_t = jnp.concatenate([jnp.transpose(w_uv[0], (1, 2, 0)),
                              jnp.zeros((n_heads, ATTN_V_EXTRA_ROWS, kv_rank), F32)], axis=1).reshape(
        n_heads * dvp, kv_rank)
    w_mla = w_mla_out[0]
    w = {
        "g_pre": norm_mix_pre_g, "w_in_a": w0[:, :o_kr].astype(BF16), "w_in_kr": w_in_kr,
        "w_in_g": w0[:, o_kr + rope_dim:].astype(BF16), "conv_w": conv_w[0], "w_conv_out": w_conv_out[0].astype(BF16),
        "q_g": q_norm_g, "w_uq_a": w_uq_a.astype(BF16), "w_uq_b": w_uq_b.astype(BF16), "kv_g": kv_norm_g,
        "w_uk_pad": w_uk_pad.astype(BF16), "w_uk_t_pad": w_uk_t_pad.astype(BF16),
        "w_uv_t": w_uv_t.astype(BF16), "w_uv_flat": uv.astype(BF16),
        "w_mla": w_mla.astype(BF16), "w_mix": w_mix_out[0].astype(BF16),
        "g_mix_post": norm_mix_post_g, "g_ca_pre": norm_ca_pre_g,
        "w_ca_q": w_ca_q[0].reshape(dm, ca_heads * ca_hd).astype(BF16),
        "w_ca_o": w_ca_o[0].reshape(ca_heads * ca_hd, dm).astype(BF16),
        "g_ca_post": norm_ca_post_g, "g_mlp_pre": norm_mlp_pre_g, "w_ff_up": w_ff_up[0].astype(BF16),
        "w_ff_down": w_ff_down[0].astype(BF16), "g_mlp_post": norm_mlp_post_g,
    }

    mk_p, mv_p = _memory_kv(mem_prompt.reshape(batch * n_mem, dm), mem_norm_g,
                            w_ca_k[0].reshape(dm, ca_heads * ca_hd).astype(BF16),
                            w_ca_v[0].reshape(dm, ca_heads * ca_hd).astype(BF16))
    ctab_p, stab_p = _rope_tabs_range(seq, rope_dim, nope_dim)
    x2d = x_prompt.reshape(batch * seq, dm)
    q, k, vt, ckv_p, kr_p, mc_p, sg_p, conv_p = _pre_prompt(dims, scale * LOG2_E, x2d, ctab_p, stab_p, w, batch,
                                                           seq, rope_dim, v_dim, dvp)
    o_p = _prompt_attention(q, k, vt, batch, seq, n_heads, v_dim)
    y_p = _post_prompt(x2d, o_p, mc_p, sg_p, mk_p, mv_p, w, batch, seq, ca_heads, n_mem)

    ctab_s, stab_s = _rope_tabs(jnp.full((n_s,), past_len, jnp.int32), rope_dim, nope_dim)
    xs = x_sample.reshape(n_s, dm)
    q_s, qlat_s, ckv_s, kr_s, mc_s, sg_s, conv_s = _pre_sample(
        dims, scale, xs, state_conv.reshape(n_s, (CONV_WIDTH - 1) * conv_dim), ctab_s, stab_s, w)
    olat = _decode_attention(page_table, qlat_s.reshape(n_s * n_heads, kv_rank),
                             q_s.reshape(n_s * n_heads, LANES), ckv_s, kr_s,
                             cache_ckv.reshape(n_pool, page, kv_rank),
                             jnp.swapaxes(cache_krope.reshape(n_pool, page, rope_dim), 1, 2), n_heads, rope_dim)
    y_s = _post_sample(xs, olat, mc_s, sg_s, _mem_rows(cache_mem_k), _mem_rows(cache_mem_v), w, n_heads, v_dim,
                       ca_heads)

    return (y_p.reshape(batch, seq, dm),
            y_s.reshape(n_s, t_s, dm),
            ckv_p.reshape(depth, batch, seq, kv_rank),
            kr_p.reshape(depth, batch, seq, rope_dim),
            conv_p.reshape(depth, batch, CONV_WIDTH - 1, conv_dim),
            mk_p.reshape(depth, batch, n_mem, ca_heads, ca_hd),
            mv_p.reshape(depth, batch, n_mem, ca_heads, ca_hd),
            ckv_s.reshape(depth, n_s, t_s, kv_rank),
            kr_s[:, :rope_dim].reshape(depth, n_s, t_s, rope_dim),
            conv_s.reshape(depth, n_s, CONV_WIDTH - 1, conv_dim))
```

```python
import functools

import numpy as np
import jax
import jax.numpy as jnp
from jax import lax
from jax.experimental import pallas as pl
from jax.experimental.pallas import tpu as pltpu

F32 = jnp.float32
BF16 = jnp.bfloat16

RMS_EPS = 1e-6
NEG_INF = -1e30
ROPE_BASE = 10000.0
CONV_WIDTH = 3
LOG2_E = 1.4426950408889634

LANES = 128
SUBLANES = 8
VMEM_LIMIT_BYTES = 56 * 1024 * 1024

ROW_TILE = 512
ATTN_TQ = 1024
ATTN_TK = 512
ATTN_V_EXTRA_ROWS = 16
DEC_PAGES_PER_STEP = 32
DEC_NBUF = 4
CA_SAMPLES_PER_STEP = 8


def _rms(x, g):
    return x * lax.rsqrt(jnp.mean(x * x, axis=-1, keepdims=True) + RMS_EPS) * g


def _bdot(a, b):
    return jnp.dot(a.astype(BF16), b.astype(BF16), preferred_element_type=F32)


def _bdot_nt(a, b):
    return lax.dot_general(a.astype(BF16), b.astype(BF16), (((1,), (1,)), ((), ())),
                           preferred_element_type=F32)


def _softmax_rows(s):
    m = jnp.max(s, axis=-1, keepdims=True)
    p = jnp.exp(s - m)
    return p / jnp.sum(p, axis=-1, keepdims=True)


def _const_spec(shape):
    nd = len(shape)
    return pl.BlockSpec(shape, lambda *_: (0,) * nd, pipeline_mode=pl.Buffered(1))


def _params(*sem):
    return pltpu.CompilerParams(dimension_semantics=tuple(sem) if sem else None,
                                vmem_limit_bytes=VMEM_LIMIT_BYTES)


def _memkv_kernel(mem_ref, g_ref, wk_ref, wv_ref, k_ref, v_ref):
    mn = _rms(mem_ref[...], g_ref[...]).astype(BF16)
    k_ref[...] = jnp.dot(mn, wk_ref[...], preferred_element_type=F32)
    v_ref[...] = jnp.dot(mn, wv_ref[...], preferred_element_type=F32)


def _memory_kv(mem2d, g, wk, wv):
    rows, d = mem2d.shape
    out = jax.ShapeDtypeStruct((rows, wk.shape[1]), F32)
    return pl.pallas_call(_memkv_kernel, out_shape=(out, out), name="mem_kv",
                          compiler_params=_params())(mem2d, g, wk, wv)


class _PreDims:
    def __init__(self, d_model, conv_dim, q_rank, kv_rank, n_heads):
        self.d_model, self.conv_dim, self.q_rank, self.kv_rank = d_model, conv_dim, q_rank, kv_rank
        self.n_heads = n_heads
        c = conv_dim
        self.o_h, self.o_gb, self.o_gc = 0, c, 2 * c
        self.o_cq = 3 * c
        self.o_ckv = self.o_cq + q_rank
        self.n_a = self.o_ckv + kv_rank
        self.hw = n_heads * LANES


def _pre_common(dims, x, ctab, stab, gpre_ref, wa_ref, wkr_ref, wg_ref, qg_ref, wuqa_ref, wuqb_ref, kvg_ref,
                scale):
    d = dims
    xn = _rms(x, gpre_ref[...]).astype(BF16)

    def proj(w_ref, lo, hi):
        return jnp.dot(xn, w_ref[:, lo:hi], preferred_element_type=F32)

    h = proj(wa_ref, d.o_h, d.o_gb)
    gate_b = proj(wa_ref, d.o_gb, d.o_gc)
    gate_c = proj(wa_ref, d.o_gc, d.o_cq)
    cq = proj(wa_ref, d.o_cq, d.o_ckv)
    ckv = proj(wa_ref, d.o_ckv, d.n_a)
    kr2 = proj(wkr_ref, 0, 2 * LANES)
    kra, krb = kr2[:, :LANES], kr2[:, LANES:]
    g_conv = proj(wg_ref, 0, d.d_model)
    g_mla = proj(wg_ref, d.d_model, 2 * d.d_model)

    u = gate_c * h
    cqn = _rms(cq, qg_ref[...]).astype(BF16)
    qa = jnp.dot(cqn, wuqa_ref[...], preferred_element_type=F32)
    qbc = jnp.dot(cqn, wuqb_ref[...], preferred_element_type=F32)
    r = wuqb_ref.shape[1] // d.n_heads
    pieces = []
    for hd in range(d.n_heads):
        blk, off = divmod(hd * r, LANES)
        piece = qbc[:, blk * LANES:(blk + 1) * LANES]
        pieces.append(pltpu.roll(piece, LANES - off, 1) if off else piece)
    qb = jnp.concatenate(pieces, axis=1)
    ct = jnp.concatenate([ctab] * d.n_heads, axis=1)
    st = jnp.concatenate([stab] * d.n_heads, axis=1)
    q = (qa * ct + qb * st) * scale
    ckvn = _rms(ckv, kvg_ref[...])
    krr = kra * ctab + krb * stab
    return u, gate_b, g_conv, g_mla, q, ckvn, krr


def _pre_prompt_kernel(dims, scale, tm, dv, dvp,
                       x_ref, ctab_ref, stab_ref, gpre_ref, wa_ref, wkr_ref, wg_ref, convw_ref, wco_ref, qg_ref,
                       wuqa_ref, wuqb_ref, kvg_ref, wuk_ref, wuvt_ref,
                       q_out, k_out, vt_out, ckv_out, kr_out, mc_out, sg_out, conv_out, ubuf):
    hist = SUBLANES

    @pl.when(pl.program_id(1) == 0)
    def _():
        ubuf[0:hist, :] = jnp.zeros((hist, dims.conv_dim), F32)

    u, gate_b, g_conv, g_mla, q, ckvn, krr = _pre_common(
        dims, x_ref[...], ctab_ref[...], stab_ref[...], gpre_ref, wa_ref, wkr_ref, wg_ref, qg_ref, wuqa_ref,
        wuqb_ref, kvg_ref, scale)

    ubuf[hist:hist + tm, :] = u
    u1 = ubuf[hist - 1:hist - 1 + tm, :]
    u2 = ubuf[hist - 2:hist - 2 + tm, :]
    cw = convw_ref[...]
    conv = u2 * cw[0:1, :] + u1 * cw[1:2, :] + u * cw[2:3, :]
    y_conv = _bdot(gate_b * conv, wco_ref[...])
    mc_out[...] = (jax.nn.sigmoid(g_conv) * y_conv).astype(mc_out.dtype)
    sg_out[...] = jax.nn.sigmoid(g_mla).astype(sg_out.dtype)
    conv_out[...] = ubuf[hist + tm - (CONV_WIDTH - 1):hist + tm, :]
    ubuf[0:hist, :] = ubuf[tm:tm + hist, :]

    q_out[...] = q.astype(q_out.dtype)
    ckv_out[...] = ckvn
    kr_out[...] = krr[:, :kr_out.shape[-1]]
    ckvb = ckvn.astype(BF16)
    ka = jnp.dot(ckvb, wuk_ref[...], preferred_element_type=F32)
    k = ka + jnp.concatenate([krr] * dims.n_heads, axis=1)
    k_out[...] = k.astype(k_out.dtype)
    vt = lax.dot_general(wuvt_ref[...], ckvb, (((1,), (1,)), ((), ())), preferred_element_type=F32)
    rid = lax.broadcasted_iota(jnp.int32, (vt.shape[0], 1), 0)
    vt_out[...] = (vt + jnp.where(rid % dvp == dv, 1.0, 0.0)).astype(vt_out.dtype)


def _pre_prompt(dims, scale, x2d, ctab, stab, w, batch, seq, rope_dim, dv, dvp):
    tm = ROW_TILE
    assert seq % tm == 0
    nt = seq // tm
    rows = batch * seq
    d = dims
    hv = w["w_uv_t"].shape[0]

    def row_spec(width):
        return pl.BlockSpec((tm, width), lambda b, t: (b * nt + t, 0))

    def tab_spec():
        return pl.BlockSpec((tm, LANES), lambda b, t: (t, 0))

    in_specs = [row_spec(d.d_model), tab_spec(), tab_spec(),
                _const_spec((1, d.d_model)), _const_spec((d.d_model, d.n_a)),
                _const_spec((d.d_model, 2 * LANES)), _const_spec((d.d_model, 2 * d.d_model)),
                _const_spec((CONV_WIDTH, d.conv_dim)), _const_spec((d.conv_dim, d.d_model)),
                _const_spec((1, d.q_rank)), _const_spec((d.q_rank, d.hw)),
                _const_spec((d.q_rank, d.n_heads * rope_dim)),
                _const_spec((1, d.kv_rank)), _const_spec((d.kv_rank, d.hw)), _const_spec((hv, d.kv_rank))]
    out_shape = (jax.ShapeDtypeStruct((rows, d.hw), BF16),
                 jax.ShapeDtypeStruct((rows, d.hw), BF16),
                 jax.ShapeDtypeStruct((batch, hv, seq), BF16),
                 jax.ShapeDtypeStruct((rows, d.kv_rank), F32),
                 jax.ShapeDtypeStruct((rows, rope_dim), F32),
                 jax.ShapeDtypeStruct((rows, d.d_model), BF16),
                 jax.ShapeDtypeStruct((rows, d.d_model), BF16),
                 jax.ShapeDtypeStruct((batch, CONV_WIDTH - 1, d.conv_dim), F32))
    out_specs = (row_spec(d.hw), row_spec(d.hw),
                 pl.BlockSpec((None, hv, tm), lambda b, t: (b, 0, t)), row_spec(d.kv_rank),
                 row_spec(rope_dim), row_spec(d.d_model), row_spec(d.d_model),
                 pl.BlockSpec((None, CONV_WIDTH - 1, d.conv_dim), lambda b, t: (b, 0, 0)))
    kern = functools.partial(_pre_prompt_kernel, dims, scale, tm, dv, dvp)
    return pl.pallas_call(
        kern, grid=(batch, nt), in_specs=in_specs, out_specs=out_specs, out_shape=out_shape,
        scratch_shapes=[pltpu.VMEM((tm + SUBLANES, d.conv_dim), F32)],
        compiler_params=_params("arbitrary", "arbitrary"), name="pre_prompt",
    )(x2d, ctab, stab, w["g_pre"], w["w_in_a"], w["w_in_kr"], w["w_in_g"], w["conv_w"], w["w_conv_out"], w["q_g"],
      w["w_uq_a"], w["w_uq_b"], w["kv_g"], w["w_uk_pad"], w["w_uv_t"])


def _pre_sample_kernel(dims, scale,
                       x_ref, state_ref, ctab_ref, stab_ref, gpre_ref, wa_ref, wkr_ref, wg_ref, convw_ref, wco_ref,
                       qg_ref, wuqa_ref, wuqb_ref, kvg_ref, wukt_ref,
                       q_out, qlat_out, ckv_out, kr_out, mc_out, sg_out, conv_out):
    c = dims.conv_dim
    u, gate_b, g_conv, g_mla, q, ckvn, krr = _pre_common(
        dims, x_ref[...], ctab_ref[...], stab_ref[...], gpre_ref, wa_ref, wkr_ref, wg_ref, qg_ref, wuqa_ref,
        wuqb_ref, kvg_ref, scale)
    u2 = state_ref[:, 0:c]
    u1 = state_ref[:, c:2 * c]
    cw = convw_ref[...]
    conv = u2 * cw[0:1, :] + u1 * cw[1:2, :] + u * cw[2:3, :]
    y_conv = _bdot(gate_b * conv, wco_ref[...])
    mc_out[...] = jax.nn.sigmoid(g_conv) * y_conv
    sg_out[...] = jax.nn.sigmoid(g_mla)
    conv_out[:, 0:c] = u1
    conv_out[:, c:2 * c] = u
    q_out[...] = q
    ckv_out[...] = ckvn
    kr_out[...] = krr
    kvr = dims.kv_rank
    qb16 = q.astype(BF16)
    for hd in range(dims.n_heads):
        blk = qb16[:, hd * LANES:(hd + 1) * LANES]
        qlat_out[:, hd * kvr:(hd + 1) * kvr] = jnp.dot(
            blk, wukt_ref[hd * LANES:(hd + 1) * LANES, :], preferred_element_type=F32)


def _pre_sample(dims, scale, xs, state2d, ctab, stab, w):
    n = xs.shape[0]
    d = dims
    out_shape = (jax.ShapeDtypeStruct((n, d.hw), F32),
                 jax.ShapeDtypeStruct((n, d.n_heads * d.kv_rank), F32),
                 jax.ShapeDtypeStruct((n, d.kv_rank), F32),
                 jax.ShapeDtypeStruct((n, LANES), F32),
                 jax.ShapeDtypeStruct((n, d.d_model), F32),
                 jax.ShapeDtypeStruct((n, d.d_model), F32),
                 jax.ShapeDtypeStruct((n, (CONV_WIDTH - 1) * d.conv_dim), F32))
    kern = functools.partial(_pre_sample_kernel, dims, scale)
    return pl.pallas_call(kern, out_shape=out_shape, compiler_params=_params(), name="pre_sample")(
        xs, state2d, ctab, stab, w["g_pre"], w["w_in_a"], w["w_in_kr"], w["w_in_g"], w["conv_w"], w["w_conv_out"],
        w["q_g"],
        w["w_uq_a"], w["w_uq_b"], w["kv_g"], w["w_uk_t_pad"])


def _attn_kernel(tq, tk, hp, dv, dvp, nq, q_ref, k_ref, vt_ref, o_ref, sa_ref, sb_ref):
    assert tq == 2 * tk
    half = tq // 2

    def tile(qi, carry):
        q0 = pl.multiple_of(qi * tq, tq)

        def scores(j, dst, q_lo=0):
            start = pl.multiple_of(j * tk, tk)
            for h in range(hp):
                kh = k_ref[pl.ds(start, tk), h * LANES:(h + 1) * LANES]
                qh = q_ref[pl.ds(pl.multiple_of(q0 + q_lo, half), tq - q_lo), h * LANES:(h + 1) * LANES]
                dst[h, :, 0:tq - q_lo] = lax.dot_general(kh, qh, (((1,), (1,)), ((), ())),
                                                         preferred_element_type=F32)

        def process(j, src, state, masked, q_lo=0):
            start = pl.multiple_of(j * tk, tk)
            n = tq - q_lo
            new_state = []
            for h in range(hp):
                m_prev, acc = state[h]
                st = src[h, :, 0:n]
                if masked:
                    key = lax.broadcasted_iota(jnp.int32, (tk, n), 0) + j * tk
                    qry = lax.broadcasted_iota(jnp.int32, (tk, n), 1) + qi * tq + q_lo
                    st = jnp.where(key <= qry, st, NEG_INF)
                m_new = jnp.maximum(m_prev[:, q_lo:], jnp.max(st, axis=0, keepdims=True))
                alpha = jnp.exp2(m_prev[:, q_lo:] - m_new)
                p = jnp.exp2(st - m_new).astype(BF16)
                vth = vt_ref[h * dvp:(h + 1) * dvp, pl.ds(start, tk)]
                acc_new = alpha * acc[:, q_lo:] + jnp.dot(vth, p, preferred_element_type=F32)
                if q_lo:
                    m_new = jnp.concatenate([m_prev[:, :q_lo], m_new], axis=1)
                    acc_new = jnp.concatenate([acc[:, :q_lo], acc_new], axis=1)
                new_state.append((m_new, acc_new))
            return tuple(new_state)

        init = tuple((jnp.full((1, tq), NEG_INF, F32), jnp.zeros((dvp, tq), F32)) for _ in range(hp))
        scores(0, sa_ref)

        def pair(t, state):
            scores(2 * t + 1, sb_ref)
            state = process(2 * t, sa_ref, state, False)
            scores(2 * t + 2, sa_ref)
            return process(2 * t + 1, sb_ref, state, False)

        state = lax.fori_loop(0, qi, pair, init)
        scores(2 * qi + 1, sb_ref, q_lo=half)
        state = process(2 * qi, sa_ref, state, True)
        state = process(2 * qi + 1, sb_ref, state, True, q_lo=half)
        ot = jnp.concatenate([acc[0:dv] / acc[dv:dv + 1] for (_, acc) in state], axis=0)
        o_ref[pl.ds(q0, tq), :] = ot.T.astype(o_ref.dtype)
        return carry

    lax.fori_loop(0, nq, tile, 0)


def _prompt_attention(q, k, vt, batch, seq, n_heads, dv):
    tq, tk = ATTN_TQ, ATTN_TK
    assert seq % tq == 0 and tq % tk == 0
    hp = LANES // dv
    assert n_heads % hp == 0
    dvp = vt.shape[1] // n_heads
    nq = seq // tq
    kern = functools.partial(_attn_kernel, tq, tk, hp, dv, dvp, nq)
    seq_spec = pl.BlockSpec((seq, hp * LANES), lambda b, g: (b, g))
    return pl.pallas_call(
        kern, grid=(batch, n_heads // hp),
        in_specs=[seq_spec, seq_spec, pl.BlockSpec((None, hp * dvp, seq), lambda b, g: (b, g, 0))],
        out_specs=pl.BlockSpec((seq, hp * dv), lambda b, g: (b, g)),
        out_shape=jax.ShapeDtypeStruct((batch * seq, n_heads * dv), BF16),
        scratch_shapes=[pltpu.VMEM((hp, tk, tq), F32), pltpu.VMEM((hp, tk, tq), F32)],
        compiler_params=_params("arbitrary", "arbitrary"), name="prompt_attn",
    )(q, k, vt)


def _mix_and_query(x, o, mc, sg, wmla_ref, wmix_ref, gmixpost_ref, gcapre_ref, wcaq_ref, ca_scale):
    y_mla = _bdot(o, wmla_ref[...])
    merged = mc.astype(F32) + sg.astype(F32) * y_mla
    y = _bdot(merged, wmix_ref[...])
    x1 = x + _rms(y, gmixpost_ref[...])
    qc = _bdot(_rms(x1, gcapre_ref[...]), wcaq_ref[...]) * ca_scale
    return x1, qc


def _mlp_tail(x1, oc, wcao_ref, gcapost_ref, gmlppre_ref, wup_ref, wdown_ref, gmlppost_ref):
    ca = _bdot(oc, wcao_ref[...])
    x2 = x1 + _rms(ca, gcapost_ref[...])
    hid = jnp.square(jnp.maximum(_bdot(_rms(x2, gmlppre_ref[...]), wup_ref[...]), 0.0))
    return x2 + _rms(_bdot(hid, wdown_ref[...]), gmlppost_ref[...])


def _post_prompt_kernel(ca_heads, ca_scale,
                        x_ref, o_ref, mc_ref, sg_ref, mk_ref, mv_ref,
                        wmla_ref, wmix_ref, gmixpost_ref, gcapre_ref, wcaq_ref, wcao_ref, gcapost_ref,
                        gmlppre_ref, wup_ref, wdown_ref, gmlppost_ref, y_ref):
    x1, qc = _mix_and_query(x_ref[...], o_ref[...], mc_ref[...], sg_ref[...], wmla_ref, wmix_ref,
                            gmixpost_ref, gcapre_ref, wcaq_ref, ca_scale)
    hd = qc.shape[1] // ca_heads
    outs = []
    for h in range(ca_heads):
        sl = slice(h * hd, (h + 1) * hd)
        p = _softmax_rows(_bdot_nt(qc[:, sl], mk_ref[:, sl]))
        outs.append(_bdot(p, mv_ref[:, sl]))
    oc = jnp.concatenate(outs, axis=1)
    y_ref[...] = _mlp_tail(x1, oc, wcao_ref, gcapost_ref, gmlppre_ref, wup_ref, wdown_ref, gmlppost_ref)


def _post_prompt(x2d, o, mc, sg, mk, mv, w, batch, seq, ca_heads, n_mem):
    tm = ROW_TILE
    nt = seq // tm
    rows, dm = x2d.shape
    dff = w["w_ff_up"].shape[1]
    ca_scale = float((dm // ca_heads) ** -0.5)

    def row_spec(width):
        return pl.BlockSpec((tm, width), lambda i: (i, 0))

    mem_spec = pl.BlockSpec((n_mem, dm), lambda i: (i // nt, 0))
    in_specs = [row_spec(dm), row_spec(o.shape[1]), row_spec(dm), row_spec(dm), mem_spec, mem_spec,
                _const_spec((o.shape[1], dm)), _const_spec((dm, dm)), _const_spec((1, dm)),
                _const_spec((1, dm)), _const_spec((dm, dm)), _const_spec((dm, dm)), _const_spec((1, dm)),
                _const_spec((1, dm)), _const_spec((dm, dff)), _const_spec((dff, dm)), _const_spec((1, dm))]
    kern = functools.partial(_post_prompt_kernel, ca_heads, ca_scale)
    return pl.pallas_call(
        kern, grid=(rows // tm,), in_specs=in_specs, out_specs=row_spec(dm),
        out_shape=jax.ShapeDtypeStruct((rows, dm), F32),
        compiler_params=_params("arbitrary"), name="post_prompt",
    )(x2d, o, mc, sg, mk, mv, w["w_mla"], w["w_mix"], w["g_mix_post"], w["g_ca_pre"], w["w_ca_q"],
      w["w_ca_o"], w["g_ca_post"], w["g_mlp_pre"], w["w_ff_up"], w["w_ff_down"], w["g_mlp_post"])


def _post_sample_a_kernel(n_heads, v_dim, ca_scale,
                          x_ref, olat_ref, mc_ref, sg_ref, wuv_ref, wmla_ref, wmix_ref, gmixpost_ref,
                          gcapre_ref, wcaq_ref, x1_ref, qc_ref):
    n = x_ref.shape[0]
    full = _bdot(olat_ref[...], wuv_ref[...])
    full = full.reshape(n, n_heads, n_heads * v_dim)
    hidx = lax.broadcasted_iota(jnp.int32, full.shape, 1)
    lane_head = lax.broadcasted_iota(jnp.int32, full.shape, 2) // v_dim
    o = jnp.sum(jnp.where(hidx == lane_head, full, 0.0), axis=1)
    x1, qc = _mix_and_query(x_ref[...], o, mc_ref[...], sg_ref[...], wmla_ref, wmix_ref,
                            gmixpost_ref, gcapre_ref, wcaq_ref, ca_scale)
    x1_ref[...] = x1
    qc_ref[...] = qc


def _post_sample_b_kernel(ca_heads, q_ref, mk_ref, mv_ref, o_ref):
    g = q_ref.shape[0]
    rows = mk_ref.shape[1]
    s8 = 2 * ca_heads
    assert s8 == SUBLANES and mk_ref.shape[2] == LANES
    lane = lax.broadcasted_iota(jnp.int32, (s8, rows), 1)
    sub = lax.broadcasted_iota(jnp.int32, (s8, rows), 0)
    own = (lane % s8) == sub
    low_half = (lax.broadcasted_iota(jnp.int32, (1, rows), 1) % s8) < ca_heads
    for i in range(g):
        q = q_ref[i]
        q8 = jnp.concatenate([q[:, (2 * (j % ca_heads) + j // ca_heads) * LANES:
                                   (2 * (j % ca_heads) + j // ca_heads + 1) * LANES] for j in range(s8)], axis=0)
        kb = mk_ref[i].astype(BF16)
        p_all = lax.dot_general(q8.astype(BF16), kb, (((1,), (1,)), ((), ())), preferred_element_type=F32)
        d = jnp.sum(jnp.where(own, p_all, 0.0), axis=0, keepdims=True)
        d = d + jnp.where(low_half, pltpu.roll(d, rows - ca_heads, 1), pltpu.roll(d, ca_heads, 1))
        dm = jnp.where(own, jnp.broadcast_to(d, (s8, rows)), NEG_INF)
        e = jnp.exp(dm - jnp.max(dm, axis=-1, keepdims=True))
        w = (e / jnp.sum(e, axis=-1, keepdims=True)).astype(BF16)
        o8 = jnp.dot(w, mv_ref[i].astype(BF16), preferred_element_type=F32)
        for j in range(s8):
            blk = 2 * (j % ca_heads) + j // ca_heads
            o_ref[i, :, blk * LANES:(blk + 1) * LANES] = o8[j:j + 1, :]


def _post_sample_c_kernel(x1_ref, oc_ref, wcao_ref, gcapost_ref, gmlppre_ref, wup_ref, wdown_ref,
                          gmlppost_ref, y_ref):
    y_ref[...] = _mlp_tail(x1_ref[...], oc_ref[...], wcao_ref, gcapost_ref, gmlppre_ref, wup_ref,
                           wdown_ref, gmlppost_ref)


def _post_sample(xs, olat, mc, sg, mem_k, mem_v, w, n_heads, v_dim, ca_heads):
    n, dm = xs.shape
    ca_scale = float((dm // ca_heads) ** -0.5)
    kern_a = functools.partial(_post_sample_a_kernel, n_heads, v_dim, ca_scale)
    x1, qc = pl.pallas_call(
        kern_a, out_shape=(jax.ShapeDtypeStruct((n, dm), F32), jax.ShapeDtypeStruct((n, dm), F32)),
        compiler_params=_params(), name="post_sample_a",
    )(xs, olat, mc, sg, w["w_uv_flat"], w["w_mla"], w["w_mix"], w["g_mix_post"], w["g_ca_pre"], w["w_ca_q"])

    g = CA_SAMPLES_PER_STEP
    assert n % g == 0
    kern_b = functools.partial(_post_sample_b_kernel, ca_heads)
    mem_spec = pl.BlockSpec((g,) + mem_k.shape[1:], lambda i: (i, 0, 0))
    oc = pl.pallas_call(
        kern_b, grid=(n // g,),
        in_specs=[pl.BlockSpec((g, 1, dm), lambda i: (i, 0, 0)), mem_spec, mem_spec],
        out_specs=pl.BlockSpec((g, 1, dm), lambda i: (i, 0, 0)),
        out_shape=jax.ShapeDtypeStruct((n, 1, dm), F32),
        compiler_params=_params("arbitrary"), name="post_sample_b",
    )(qc.reshape(n, 1, dm), mem_k, mem_v)

    return pl.pallas_call(
        _post_sample_c_kernel, out_shape=jax.ShapeDtypeStruct((n, dm), F32),
        compiler_params=_params(), name="post_sample_c",
    )(x1, oc.reshape(n, dm), w["w_ca_o"], w["g_ca_post"], w["g_mlp_pre"], w["w_ff_up"], w["w_ff_down"],
      w["g_mlp_post"])


def _dec_attn_kernel(n_samples, n_pages, n_heads, rope_dim,
                     pt_ref, qlat_ref, qrope_ref, ckvn_ref, krn_ref, cache_ckv, cache_krt,
                     o_ref, cbuf, kbuf, sems):
    ch = DEC_PAGES_PER_STEP
    nch = n_pages // ch
    total = n_samples * nch
    page, kvr = cbuf.shape[2], cbuf.shape[3]

    def copies(g, slot):
        b = g // nch
        c = g % nch
        out = []
        for i in range(ch):
            pg = pt_ref[b, c * ch + i]
            out.append(pltpu.make_async_copy(cache_ckv.at[pg], cbuf.at[slot, i], sems.at[0, slot]))
            out.append(pltpu.make_async_copy(cache_krt.at[pg], kbuf.at[slot, i], sems.at[1, slot]))
        return out

    def start(g):
        for n, cp in enumerate(copies(g, g % DEC_NBUF)):
            cp.start(priority=(n // 2) % 2)

    def wait(g):
        for cp in copies(g, g % DEC_NBUF):
            cp.wait()

    def q_rows(b):
        row0 = pl.multiple_of(b * n_heads, n_heads)
        return qlat_ref[pl.ds(row0, n_heads), :], qrope_ref[pl.ds(row0, n_heads), :][:, :rope_dim]

    def scores(g):
        ql, qr = q_rows(g // nch)
        slot = g % DEC_NBUF
        ckv = cbuf[slot].reshape(ch * page, kvr).astype(BF16)
        qrb = qr.astype(BF16)
        s_rope = jnp.concatenate(
            [jnp.dot(qrb, kbuf[slot, i].astype(BF16), preferred_element_type=F32) for i in range(ch)], axis=1)
        return lax.dot_general(ql.astype(BF16), ckv, (((1,), (1,)), ((), ())),
                               preferred_element_type=F32) + s_rope

    for g0 in range(DEC_NBUF - 1):
        start(g0)
    wait(0)
    s0 = scores(0)

    def body(g, carry):
        s_cur, m, l, acc = carry
        b = g // nch
        c = g % nch

        @pl.when(g + 1 < total)
        def _():
            wait(g + 1)

        @pl.when(g + (DEC_NBUF - 1) < total)
        def _():
            start(g + (DEC_NBUF - 1))

        s_next = scores(jnp.minimum(g + 1, total - 1))

        ql, qr = q_rows(b)
        c_new = ckvn_ref[pl.ds(b, 1), :]
        r_new = krn_ref[pl.ds(b, 1), :][:, :rope_dim]
        s_new = jnp.sum(ql * c_new, axis=-1, keepdims=True) + jnp.sum(qr * r_new, axis=-1, keepdims=True)
        first = c == 0
        m = jnp.where(first, s_new, m)
        l = jnp.where(first, 1.0, l)
        acc = jnp.where(first, jnp.broadcast_to(c_new, acc.shape), acc)

        ckv = cbuf[g % DEC_NBUF].reshape(ch * page, kvr).astype(BF16)
        m_new = jnp.maximum(m, jnp.max(s_cur, axis=-1, keepdims=True))
        alpha = jnp.exp(m - m_new)
        p = jnp.exp(s_cur - m_new)
        l = alpha * l + jnp.sum(p, axis=-1, keepdims=True)
        acc = alpha * acc + jnp.dot(p.astype(BF16), ckv, preferred_element_type=F32)
        o_ref[pl.ds(pl.multiple_of(b * n_heads, n_heads), n_heads), :] = acc / l
        return s_next, m_new, l, acc

    init = (s0, jnp.zeros((n_heads, 1), F32), jnp.zeros((n_heads, 1), F32), jnp.zeros((n_heads, kvr), F32))
    lax.fori_loop(0, total, body, init)


def _decode_attention(page_table, qlat, qrope, ckv_new, kr_new, cache_ckv, cache_krt, n_heads, rope_dim):
    n_samples, n_pages = page_table.shape
    page, kvr = cache_ckv.shape[1], cache_ckv.shape[2]
    ch = DEC_PAGES_PER_STEP
    assert n_pages % ch == 0

    def whole(shape):
        nd = len(shape)
        return pl.BlockSpec(shape, lambda i, pt: (0,) * nd)

    kern = functools.partial(_dec_attn_kernel, n_samples, n_pages, n_heads, rope_dim)
    grid_spec = pltpu.PrefetchScalarGridSpec(
        num_scalar_prefetch=1, grid=(1,),
        in_specs=[whole(qlat.shape), whole(qrope.shape), whole(ckv_new.shape), whole(kr_new.shape),
                  pl.BlockSpec(memory_space=pl.ANY), pl.BlockSpec(memory_space=pl.ANY)],
        out_specs=whole(qlat.shape),
        scratch_shapes=[pltpu.VMEM((DEC_NBUF, ch, page, kvr), F32),
                        pltpu.VMEM((DEC_NBUF, ch, rope_dim, page), F32),
                        pltpu.SemaphoreType.DMA((2, DEC_NBUF))])
    return pl.pallas_call(
        kern, grid_spec=grid_spec, out_shape=jax.ShapeDtypeStruct(qlat.shape, F32),
        compiler_params=_params("arbitrary"), name="decode_attn",
    )(page_table, qlat, qrope, ckv_new, kr_new, cache_ckv, cache_krt)


def _rot_half(w):
    half = w.shape[-1] // 2
    return jnp.concatenate([-w[..., half:], w[..., :half]], axis=-1)


def _mem_rows(cache):
    _, n, n_mem, heads, hd = cache.shape
    assert hd == 2 * LANES
    x = cache.reshape(n, n_mem, heads, 2, LANES)
    return jnp.transpose(x, (0, 1, 3, 2, 4)).reshape(n, n_mem * 2 * heads, LANES)


def _rope_lane_freqs(rope_dim):
    inv = 1.0 / (ROPE_BASE ** (jnp.arange(0, rope_dim, 2, dtype=F32) / rope_dim))
    return jnp.concatenate([inv, inv, jnp.zeros((LANES - rope_dim,), F32)])


def _rope_tabs(pos, rope_dim, nope_dim):
    ang = pos.astype(F32)[:, None] * _rope_lane_freqs(rope_dim)[None, :]
    keep = (jnp.arange(LANES) < rope_dim + nope_dim).astype(F32)
    return jnp.cos(ang) * keep, jnp.sin(ang)


def _rope_tabs_range(seq, rope_dim, nope_dim):
    assert seq % LANES == 0
    freqs = _rope_lane_freqs(rope_dim)
    a = (jnp.arange(seq // LANES, dtype=F32) * LANES)[:, None, None] * freqs
    b = jnp.arange(LANES, dtype=F32)[None, :, None] * freqs
    ca, sa, cb, sb = jnp.cos(a), jnp.sin(a), jnp.cos(b), jnp.sin(b)
    keep = (jnp.arange(LANES) < rope_dim + nope_dim).astype(F32)
    ctab = (ca * cb - sa * sb) * keep
    stab = sa * cb + ca * sb
    return ctab.reshape(seq, LANES), stab.reshape(seq, LANES)


def kernel(x_prompt, x_sample, mem_prompt, cache_ckv, cache_krope, state_conv, cache_mem_k, cache_mem_v,
           page_table, norm_mix_pre_g, w_in, conv_w, w_conv_out, q_norm_g, w_uq, kv_norm_g, w_uk, w_uv,
           w_mla_out, w_mix_out, norm_mix_post_g, norm_ca_pre_g, mem_norm_g, w_ca_q, w_ca_k, w_ca_v, w_ca_o,
           norm_ca_post_g, norm_mlp_pre_g, w_ff_up, w_ff_down, norm_mlp_post_g):
    depth = w_in.shape[0]
    assert depth == 1, "single-layer step"
    batch, seq, dm = x_prompt.shape
    n_s, t_s, _ = x_sample.shape
    assert t_s == 1
    conv_dim = conv_w.shape[2]
    q_rank, n_heads, qk_dim = w_uq.shape[1:]
    kv_rank, _, nope_dim = w_uk.shape[1:]
    v_dim = w_uv.shape[3]
    rope_dim = qk_dim - nope_dim
    n_mem, ca_heads, ca_hd = cache_mem_k.shape[2:]
    n_pool, page = cache_ckv.shape[1:3]
    past_len = page_table.shape[1] * page
    assert rope_dim + nope_dim <= LANES and v_dim <= LANES
    dims = _PreDims(dm, conv_dim, q_rank, kv_rank, n_heads)
    scale = float(qk_dim ** -0.5)

    w0 = w_in[0]
    o_kr = 3 * conv_dim + q_rank + kv_rank
    w_kr = w0[:, o_kr:o_kr + rope_dim]
    zpad = jnp.zeros((dm, LANES - rope_dim), F32)
    w_in_kr = jnp.concatenate([w_kr, zpad, _rot_half(w_kr), zpad], axis=1).astype(BF16)
    uq = w_uq[0]
    uq_nope, uq_rope = uq[:, :, :nope_dim], uq[:, :, nope_dim:]
    hz = lambda r, width: jnp.zeros((r, n_heads, width), F32)
    pad_tail = LANES - rope_dim - nope_dim
    w_uq_a = jnp.concatenate([uq_rope, uq_nope, hz(q_rank, pad_tail)], axis=2).reshape(q_rank, n_heads * LANES)
    assert LANES % rope_dim == 0
    w_uq_b = _rot_half(uq_rope).reshape(q_rank, n_heads * rope_dim)
    uk = w_uk[0]
    w_uk_pad = jnp.concatenate([hz(kv_rank, rope_dim), uk, hz(kv_rank, pad_tail)], axis=2).reshape(
        kv_rank, n_heads * LANES)
    w_uk_t_pad = jnp.transpose(w_uk_pad.reshape(kv_rank, n_heads, LANES), (1, 2, 0)).reshape(
        n_heads * LANES, kv_rank)
    uv = w_uv[0].reshape(kv_rank, n_heads * v_dim)
    dvp = v_dim + ATTN_V_EXTRA_ROWS
    w_uv_t = jnp.concatenate([jnp.transpose(w_uv[0], (1, 2, 0)),
                              jnp.zeros((n_heads, ATTN_V_EXTRA_ROWS, kv_rank), F32)], axis=1).reshape(
        n_heads * dvp, kv_rank)
    w_mla = w_mla_out[0]
    w = {
        "g_pre": norm_mix_pre_g, "w_in_a": w0[:, :o_kr].astype(BF16), "w_in_kr": w_in_kr,
        "w_in_g": w0[:, o_kr + rope_dim:].astype(BF16), "conv_w": conv_w[0], "w_conv_out": w_conv_out[0].astype(BF16),
        "q_g": q_norm_g, "w_uq_a": w_uq_a.astype(BF16), "w_uq_b": w_uq_b.astype(BF16), "kv_g": kv_norm_g,
        "w_uk_pad": w_uk_pad.astype(BF16), "w_uk_t_pad": w_uk_t_pad.astype(BF16),
        "w_uv_t": w_uv_t.astype(BF16), "w_uv_flat": uv.astype(BF16),
        "w_mla": w_mla.astype(BF16), "w_mix": w_mix_out[0].astype(BF16),
        "g_mix_post": norm_mix_post_g, "g_ca_pre": norm_ca_pre_g,
        "w_ca_q": w_ca_q[0].reshape(dm, ca_heads * ca_hd).astype(BF16),
        "w_ca_o": w_ca_o[0].reshape(ca_heads * ca_hd, dm).astype(BF16),
        "g_ca_post": norm_ca_post_g, "g_mlp_pre": norm_mlp_pre_g, "w_ff_up": w_ff_up[0].astype(BF16),
        "w_ff_down": w_ff_down[0].astype(BF16), "g_mlp_post": norm_mlp_post_g,
    }

    mk_p, mv_p = _memory_kv(mem_prompt.reshape(batch * n_mem, dm), mem_norm_g,
                            w_ca_k[0].reshape(dm, ca_heads * ca_hd).astype(BF16),
                            w_ca_v[0].reshape(dm, ca_heads * ca_hd).astype(BF16))
    ctab_p, stab_p = _rope_tabs_range(seq, rope_dim, nope_dim)
    x2d = x_prompt.reshape(batch * seq, dm)
    q, k, vt, ckv_p, kr_p, mc_p, sg_p, conv_p = _pre_prompt(dims, scale * LOG2_E, x2d, ctab_p, stab_p, w, batch,
                                                           seq, rope_dim, v_dim, dvp)
    o_p = _prompt_attention(q, k, vt, batch, seq, n_heads, v_dim)
    y_p = _post_prompt(x2d, o_p, mc_p, sg_p, mk_p, mv_p, w, batch, seq, ca_heads, n_mem)

    ctab_s, stab_s = _rope_tabs(jnp.full((n_s,), past_len, jnp.int32), rope_dim, nope_dim)
    xs = x_sample.reshape(n_s, dm)
    q_s, qlat_s, ckv_s, kr_s, mc_s, sg_s, conv_s = _pre_sample(
        dims, scale, xs, state_conv.reshape(n_s, (CONV_WIDTH - 1) * conv_dim), ctab_s, stab_s, w)
    olat = _decode_attention(page_table, qlat_s.reshape(n_s * n_heads, kv_rank),
                             q_s.reshape(n_s * n_heads, LANES), ckv_s, kr_s,
                             cache_ckv.reshape(n_pool, page, kv_rank),
                             jnp.swapaxes(cache_krope.reshape(n_pool, page, rope_dim), 1, 2), n_heads, rope_dim)
    y_s = _post_sample(xs, olat, mc_s, sg_s, _mem_rows(cache_mem_k), _mem_rows(cache_mem_v), w, n_heads, v_dim,
                       ca_heads)

    return (y_p.reshape(batch, seq, dm),
            y_s.reshape(n_s, t_s, dm),
            ckv_p.reshape(depth, batch, seq, kv_rank),
            kr_p.reshape(depth, batch, seq, rope_dim),
            conv_p.reshape(depth, batch, CONV_WIDTH - 1, conv_dim),
            mk_p.reshape(depth, batch, n_mem, ca_heads, ca_hd),
            mv_p.reshape(depth, batch, n_mem, ca_heads, ca_hd),
            ckv_s.reshape(depth, n_s, t_s, kv_rank),
            kr_s[:, :rope_dim].reshape(depth, n_s, t_s, rope_dim),
            conv_s.reshape(depth, n_s, CONV_WIDTH - 1, conv_dim))
```

```python
import functools

import jax
import jax.numpy as jnp
from jax import lax
from jax.experimental import pallas as pl
from jax.experimental.pallas import tpu as pltpu

F32 = jnp.float32
BF16 = jnp.bfloat16

RMS_EPS = 1e-6
NEG_INF = -1e30
ROPE_BASE = 10000.0
CONV_WIDTH = 3
LOG2_E = 1.4426950408889634

LANES = 128
SUBLANES = 8
VMEM_LIMIT_BYTES = 56 * 1024 * 1024

ROW_TILE = 512
ATTN_TQ = 1024
ATTN_TK = 512
ATTN_V_EXTRA_ROWS = 16
DEC_PAGES_PER_STEP = 32
DEC_NBUF = 5
CA_SAMPLES_PER_STEP = 8


def _rms(x, g):
    return x * lax.rsqrt(jnp.mean(x * x, axis=-1, keepdims=True) + RMS_EPS) * g


def _bdot(a, b):
    return jnp.dot(a.astype(BF16), b.astype(BF16), preferred_element_type=F32)


def _bdot_nt(a, b):
    return lax.dot_general(a.astype(BF16), b.astype(BF16), (((1,), (1,)), ((), ())),
                           preferred_element_type=F32)


def _softmax_rows(s):
    m = jnp.max(s, axis=-1, keepdims=True)
    p = jnp.exp(s - m)
    return p / jnp.sum(p, axis=-1, keepdims=True)


def _const_spec(shape):
    nd = len(shape)
    return pl.BlockSpec(shape, lambda *_: (0,) * nd, pipeline_mode=pl.Buffered(1))


def _params(*sem):
    return pltpu.CompilerParams(dimension_semantics=tuple(sem) if sem else None,
                                vmem_limit_bytes=VMEM_LIMIT_BYTES)


def _memkv_kernel(mem_ref, g_ref, wk_ref, wv_ref, k_ref, v_ref):
    mn = _rms(mem_ref[...], g_ref[...]).astype(BF16)
    k_ref[...] = jnp.dot(mn, wk_ref[...], preferred_element_type=F32)
    v_ref[...] = jnp.dot(mn, wv_ref[...], preferred_element_type=F32)


def _memory_kv(mem2d, g, wk, wv):
    rows, d = mem2d.shape
    out = jax.ShapeDtypeStruct((rows, wk.shape[1]), F32)
    return pl.pallas_call(_memkv_kernel, out_shape=(out, out), name="mem_kv",
                          compiler_params=_params())(mem2d, g, wk, wv)


class _PreDims:
    def __init__(self, d_model, conv_dim, q_rank, kv_rank, n_heads):
        self.d_model, self.conv_dim, self.q_rank, self.kv_rank = d_model, conv_dim, q_rank, kv_rank
        self.n_heads = n_heads
        c = conv_dim
        self.o_h, self.o_gb, self.o_gc = 0, c, 2 * c
        self.o_cq = 3 * c
        self.o_ckv = self.o_cq + q_rank
        self.n_a = self.o_ckv + kv_rank
        self.hw = n_heads * LANES


def _pre_common(dims, x, ctab, stab, gpre_ref, wa_ref, wkr_ref, wg_ref, qg_ref, wuqa_ref, wuqb_ref, kvg_ref,
                scale):
    d = dims
    xn = _rms(x, gpre_ref[...]).astype(BF16)

    def proj(w_ref, lo, hi):
        return jnp.dot(xn, w_ref[:, lo:hi], preferred_element_type=F32)

    h = proj(wa_ref, d.o_h, d.o_gb)
    gate_b = proj(wa_ref, d.o_gb, d.o_gc)
    gate_c = proj(wa_ref, d.o_gc, d.o_cq)
    cq = proj(wa_ref, d.o_cq, d.o_ckv)
    ckv = proj(wa_ref, d.o_ckv, d.n_a)
    kr2 = proj(wkr_ref, 0, 2 * LANES)
    kra, krb = kr2[:, :LANES], kr2[:, LANES:]
    g_conv = proj(wg_ref, 0, d.d_model)
    g_mla = proj(wg_ref, d.d_model, 2 * d.d_model)

    u = gate_c * h
    cqn = _rms(cq, qg_ref[...]).astype(BF16)
    qa = jnp.dot(cqn, wuqa_ref[...], preferred_element_type=F32)
    qbc = jnp.dot(cqn, wuqb_ref[...], preferred_element_type=F32)
    r = wuqb_ref.shape[1] // d.n_heads
    pieces = []
    for hd in range(d.n_heads):
        blk, off = divmod(hd * r, LANES)
        piece = qbc[:, blk * LANES:(blk + 1) * LANES]
        pieces.append(pltpu.roll(piece, LANES - off, 1) if off else piece)
    qb = jnp.concatenate(pieces, axis=1)
    ct = jnp.concatenate([ctab] * d.n_heads, axis=1)
    st = jnp.concatenate([stab] * d.n_heads, axis=1)
    q = (qa * ct + qb * st) * scale
    ckvn = _rms(ckv, kvg_ref[...])
    krr = kra * ctab + krb * stab
    return u, gate_b, g_conv, g_mla, q, ckvn, krr


def _pre_prompt_kernel(dims, scale, tm, dv, dvp,
                       x_ref, ctab_ref, stab_ref, gpre_ref, wa_ref, wkr_ref, wg_ref, convw_ref, wco_ref, qg_ref,
                       wuqa_ref, wuqb_ref, kvg_ref, wuk_ref, wuvt_ref,
                       q_out, k_out, vt_out, ckv_out, kr_out, mc_out, sg_out, conv_out, ubuf):
    hist = SUBLANES

    @pl.when(pl.program_id(1) == 0)
    def _():
        ubuf[0:hist, :] = jnp.zeros((hist, dims.conv_dim), F32)

    u, gate_b, g_conv, g_mla, q, ckvn, krr = _pre_common(
        dims, x_ref[...], ctab_ref[...], stab_ref[...], gpre_ref, wa_ref, wkr_ref, wg_ref, qg_ref, wuqa_ref,
        wuqb_ref, kvg_ref, scale)

    ubuf[hist:hist + tm, :] = u
    u1 = ubuf[hist - 1:hist - 1 + tm, :]
    u2 = ubuf[hist - 2:hist - 2 + tm, :]
    cw = convw_ref[...]
    conv = u2 * cw[0:1, :] + u1 * cw[1:2, :] + u * cw[2:3, :]
    y_conv = _bdot(gate_b * conv, wco_ref[...])
    mc_out[...] = (jax.nn.sigmoid(g_conv) * y_conv).astype(mc_out.dtype)
    sg_out[...] = jax.nn.sigmoid(g_mla).astype(sg_out.dtype)
    conv_out[...] = ubuf[hist + tm - (CONV_WIDTH - 1):hist + tm, :]
    ubuf[0:hist, :] = ubuf[tm:tm + hist, :]

    q_out[...] = q.astype(q_out.dtype)
    ckv_out[...] = ckvn
    kr_out[...] = krr[:, :kr_out.shape[-1]]
    ckvb = ckvn.astype(BF16)
    ka = jnp.dot(ckvb, wuk_ref[...], preferred_element_type=F32)
    k = ka + jnp.concatenate([krr] * dims.n_heads, axis=1)
    k_out[...] = k.astype(k_out.dtype)
    vt = lax.dot_general(wuvt_ref[...], ckvb, (((1,), (1,)), ((), ())), preferred_element_type=F32)
    rid = lax.broadcasted_iota(jnp.int32, (vt.shape[0], 1), 0)
    vt_out[...] = (vt + jnp.where(rid % dvp == dv, 1.0, 0.0)).astype(vt_out.dtype)


def _pre_prompt(dims, scale, x2d, ctab, stab, w, batch, seq, rope_dim, dv, dvp):
    tm = ROW_TILE
    assert seq % tm == 0
    nt = seq // tm
    rows = batch * seq
    d = dims
    hv = w["w_uv_t"].shape[0]

    def row_spec(width):
        return pl.BlockSpec((tm, width), lambda b, t: (b * nt + t, 0))

    def tab_spec():
        return pl.BlockSpec((tm, LANES), lambda b, t: (t, 0))

    in_specs = [row_spec(d.d_model), tab_spec(), tab_spec(),
                _const_spec((1, d.d_model)), _const_spec((d.d_model, d.n_a)),
                _const_spec((d.d_model, 2 * LANES)), _const_spec((d.d_model, 2 * d.d_model)),
                _const_spec((CONV_WIDTH, d.conv_dim)), _const_spec((d.conv_dim, d.d_model)),
                _const_spec((1, d.q_rank)), _const_spec((d.q_rank, d.hw)),
                _const_spec((d.q_rank, d.n_heads * rope_dim)),
                _const_spec((1, d.kv_rank)), _const_spec((d.kv_rank, d.hw)), _const_spec((hv, d.kv_rank))]
    out_shape = (jax.ShapeDtypeStruct((rows, d.hw), BF16),
                 jax.ShapeDtypeStruct((rows, d.hw), BF16),
                 jax.ShapeDtypeStruct((batch, hv, seq), BF16),
                 jax.ShapeDtypeStruct((rows, d.kv_rank), F32),
                 jax.ShapeDtypeStruct((rows, rope_dim), F32),
                 jax.ShapeDtypeStruct((rows, d.d_model), BF16),
                 jax.ShapeDtypeStruct((rows, d.d_model), BF16),
                 jax.ShapeDtypeStruct((batch, CONV_WIDTH - 1, d.conv_dim), F32))
    out_specs = (row_spec(d.hw), row_spec(d.hw),
                 pl.BlockSpec((None, hv, tm), lambda b, t: (b, 0, t)), row_spec(d.kv_rank),
                 row_spec(rope_dim), row_spec(d.d_model), row_spec(d.d_model),
                 pl.BlockSpec((None, CONV_WIDTH - 1, d.conv_dim), lambda b, t: (b, 0, 0)))
    kern = functools.partial(_pre_prompt_kernel, dims, scale, tm, dv, dvp)
    return pl.pallas_call(
        kern, grid=(batch, nt), in_specs=in_specs, out_specs=out_specs, out_shape=out_shape,
        scratch_shapes=[pltpu.VMEM((tm + SUBLANES, d.conv_dim), F32)],
        compiler_params=_params("arbitrary", "arbitrary"), name="pre_prompt",
    )(x2d, ctab, stab, w["g_pre"], w["w_in_a"], w["w_in_kr"], w["w_in_g"], w["conv_w"], w["w_conv_out"], w["q_g"],
      w["w_uq_a"], w["w_uq_b"], w["kv_g"], w["w_uk_pad"], w["w_uv_t"])


def _pre_sample_kernel(dims, scale,
                       x_ref, state_ref, ctab_ref, stab_ref, gpre_ref, wa_ref, wkr_ref, wg_ref, convw_ref, wco_ref,
                       qg_ref, wuqa_ref, wuqb_ref, kvg_ref, wukt_ref,
                       q_out, qlat_out, ckv_out, kr_out, mc_out, sg_out, conv_out):
    c = dims.conv_dim
    u, gate_b, g_conv, g_mla, q, ckvn, krr = _pre_common(
        dims, x_ref[...], ctab_ref[...], stab_ref[...], gpre_ref, wa_ref, wkr_ref, wg_ref, qg_ref, wuqa_ref,
        wuqb_ref, kvg_ref, scale)
    u2 = state_ref[:, 0:c]
    u1 = state_ref[:, c:2 * c]
    cw = convw_ref[...]
    conv = u2 * cw[0:1, :] + u1 * cw[1:2, :] + u * cw[2:3, :]
    y_conv = _bdot(gate_b * conv, wco_ref[...])
    mc_out[...] = jax.nn.sigmoid(g_conv) * y_conv
    sg_out[...] = jax.nn.sigmoid(g_mla)
    conv_out[:, 0:c] = u1
    conv_out[:, c:2 * c] = u
    q_out[...] = q
    ckv_out[...] = ckvn
    kr_out[...] = krr
    kvr = dims.kv_rank
    qb16 = q.astype(BF16)
    for hd in range(dims.n_heads):
        blk = qb16[:, hd * LANES:(hd + 1) * LANES]
        qlat_out[:, hd * kvr:(hd + 1) * kvr] = jnp.dot(
            blk, wukt_ref[hd * LANES:(hd + 1) * LANES, :], preferred_element_type=F32)


def _pre_sample(dims, scale, xs, state2d, ctab, stab, w):
    n = xs.shape[0]
    d = dims
    out_shape = (jax.ShapeDtypeStruct((n, d.hw), F32),
                 jax.ShapeDtypeStruct((n, d.n_heads * d.kv_rank), F32),
                 jax.ShapeDtypeStruct((n, d.kv_rank), F32),
                 jax.ShapeDtypeStruct((n, LANES), F32),
                 jax.ShapeDtypeStruct((n, d.d_model), F32),
                 jax.ShapeDtypeStruct((n, d.d_model), F32),
                 jax.ShapeDtypeStruct((n, (CONV_WIDTH - 1) * d.conv_dim), F32))
    kern = functools.partial(_pre_sample_kernel, dims, scale)
    return pl.pallas_call(kern, out_shape=out_shape, compiler_params=_params(), name="pre_sample")(
        xs, state2d, ctab, stab, w["g_pre"], w["w_in_a"], w["w_in_kr"], w["w_in_g"], w["conv_w"], w["w_conv_out"],
        w["q_g"],
        w["w_uq_a"], w["w_uq_b"], w["kv_g"], w["w_uk_t_pad"])


def _attn_kernel(tq, tk, hp, dv, dvp, nq, q_ref, k_ref, vt_ref, o_ref, sa_ref, sb_ref):
    assert tq == 2 * tk
    half = tq // 2

    def tile(qi, carry):
        q0 = pl.multiple_of(qi * tq, tq)

        def scores(j, dst, q_lo=0):
            start = pl.multiple_of(j * tk, tk)
            for h in range(hp):
                kh = k_ref[pl.ds(start, tk), h * LANES:(h + 1) * LANES]
                qh = q_ref[pl.ds(pl.multiple_of(q0 + q_lo, half), tq - q_lo), h * LANES:(h + 1) * LANES]
                dst[h, :, 0:tq - q_lo] = lax.dot_general(kh, qh, (((1,), (1,)), ((), ())),
                                                         preferred_element_type=F32)

        def process(j, src, state, masked, q_lo=0):
            start = pl.multiple_of(j * tk, tk)
            n = tq - q_lo
            new_state = []
            for h in range(hp):
                m_prev, acc = state[h]
                st = src[h, :, 0:n]
                if masked:
                    key = lax.broadcasted_iota(jnp.int32, (tk, n), 0) + j * tk
                    qry = lax.broadcasted_iota(jnp.int32, (tk, n), 1) + qi * tq + q_lo
                    st = jnp.where(key <= qry, st, NEG_INF)
                m_new = jnp.maximum(m_prev[:, q_lo:], jnp.max(st, axis=0, keepdims=True))
                alpha = jnp.exp2(m_prev[:, q_lo:] - m_new)
                p = jnp.exp2(st - m_new).astype(BF16)
                vth = vt_ref[h * dvp:(h + 1) * dvp, pl.ds(start, tk)]
                acc_new = alpha * acc[:, q_lo:] + jnp.dot(vth, p, preferred_element_type=F32)
                if q_lo:
                    m_new = jnp.concatenate([m_prev[:, :q_lo], m_new], axis=1)
                    acc_new = jnp.concatenate([acc[:, :q_lo], acc_new], axis=1)
                new_state.append((m_new, acc_new))
            return tuple(new_state)

        init = tuple((jnp.full((1, tq), NEG_INF, F32), jnp.zeros((dvp, tq), F32)) for _ in range(hp))
        scores(0, sa_ref)

        def pair(t, state):
            scores(2 * t + 1, sb_ref)
            state = process(2 * t, sa_ref, state, False)
            scores(2 * t + 2, sa_ref)
            return process(2 * t + 1, sb_ref, state, False)

        state = lax.fori_loop(0, qi, pair, init)
        scores(2 * qi + 1, sb_ref, q_lo=half)
        state = process(2 * qi, sa_ref, state, True)
        state = process(2 * qi + 1, sb_ref, state, True, q_lo=half)
        ot = jnp.concatenate([acc[0:dv] / acc[dv:dv + 1] for (_, acc) in state], axis=0)
        o_ref[pl.ds(q0, tq), :] = ot.T.astype(o_ref.dtype)
        return carry

    lax.fori_loop(0, nq, tile, 0)


def _prompt_attention(q, k, vt, batch, seq, n_heads, dv):
    tq, tk = ATTN_TQ, ATTN_TK
    assert seq % tq == 0 and tq % tk == 0
    hp = LANES // dv
    assert n_heads % hp == 0
    dvp = vt.shape[1] // n_heads
    nq = seq // tq
    kern = functools.partial(_attn_kernel, tq, tk, hp, dv, dvp, nq)
    seq_spec = pl.BlockSpec((seq, hp * LANES), lambda b, g: (b, g))
    return pl.pallas_call(
        kern, grid=(batch, n_heads // hp),
        in_specs=[seq_spec, seq_spec, pl.BlockSpec((None, hp * dvp, seq), lambda b, g: (b, g, 0))],
        out_specs=pl.BlockSpec((seq, hp * dv), lambda b, g: (b, g)),
        out_shape=jax.ShapeDtypeStruct((batch * seq, n_heads * dv), BF16),
        scratch_shapes=[pltpu.VMEM((hp, tk, tq), F32), pltpu.VMEM((hp, tk, tq), F32)],
        compiler_params=_params("arbitrary", "arbitrary"), name="prompt_attn",
    )(q, k, vt)


def _mix_and_query(x, o, mc, sg, wmla_ref, wmix_ref, gmixpost_ref, gcapre_ref, wcaq_ref, ca_scale):
    y_mla = _bdot(o, wmla_ref[...])
    merged = mc.astype(F32) + sg.astype(F32) * y_mla
    y = _bdot(merged, wmix_ref[...])
    x1 = x + _rms(y, gmixpost_ref[...])
    qc = _bdot(_rms(x1, gcapre_ref[...]), wcaq_ref[...]) * ca_scale
    return x1, qc


def _mlp_tail(x1, oc, wcao_ref, gcapost_ref, gmlppre_ref, wup_ref, wdown_ref, gmlppost_ref):
    ca = _bdot(oc, wcao_ref[...])
    x2 = x1 + _rms(ca, gcapost_ref[...])
    hid = jnp.square(jnp.maximum(_bdot(_rms(x2, gmlppre_ref[...]), wup_ref[...]), 0.0))
    return x2 + _rms(_bdot(hid, wdown_ref[...]), gmlppost_ref[...])


def _post_prompt_kernel(ca_heads, ca_scale,
                        x_ref, o_ref, mc_ref, sg_ref, mk_ref, mv_ref,
                        wmla_ref, wmix_ref, gmixpost_ref, gcapre_ref, wcaq_ref, wcao_ref, gcapost_ref,
                        gmlppre_ref, wup_ref, wdown_ref, gmlppost_ref, y_ref):
    x1, qc = _mix_and_query(x_ref[...], o_ref[...], mc_ref[...], sg_ref[...], wmla_ref, wmix_ref,
                            gmixpost_ref, gcapre_ref, wcaq_ref, ca_scale)
    hd = qc.shape[1] // ca_heads
    outs = []
    for h in range(ca_heads):
        sl = slice(h * hd, (h + 1) * hd)
        p = _softmax_rows(_bdot_nt(qc[:, sl], mk_ref[:, sl]))
        outs.append(_bdot(p, mv_ref[:, sl]))
    oc = jnp.concatenate(outs, axis=1)
    y_ref[...] = _mlp_tail(x1, oc, wcao_ref, gcapost_ref, gmlppre_ref, wup_ref, wdown_ref, gmlppost_ref)


def _post_prompt(x2d, o, mc, sg, mk, mv, w, batch, seq, ca_heads, n_mem):
    tm = ROW_TILE
    nt = seq // tm
    rows, dm = x2d.shape
    dff = w["w_ff_up"].shape[1]
    ca_scale = float((dm // ca_heads) ** -0.5)

    def row_spec(width):
        return pl.BlockSpec((tm, width), lambda i: (i, 0))

    mem_spec = pl.BlockSpec((n_mem, dm), lambda i: (i // nt, 0))
    in_specs = [row_spec(dm), row_spec(o.shape[1]), row_spec(dm), row_spec(dm), mem_spec, mem_spec,
                _const_spec((o.shape[1], dm)), _const_spec((dm, dm)), _const_spec((1, dm)),
                _const_spec((1, dm)), _const_spec((dm, dm)), _const_spec((dm, dm)), _const_spec((1, dm)),
                _const_spec((1, dm)), _const_spec((dm, dff)), _const_spec((dff, dm)), _const_spec((1, dm))]
    kern = functools.partial(_post_prompt_kernel, ca_heads, ca_scale)
    return pl.pallas_call(
        kern, grid=(rows // tm,), in_specs=in_specs, out_specs=row_spec(dm),
        out_shape=jax.ShapeDtypeStruct((rows, dm), F32),
        compiler_params=_params("arbitrary"), name="post_prompt",
    )(x2d, o, mc, sg, mk, mv, w["w_mla"], w["w_mix"], w["g_mix_post"], w["g_ca_pre"], w["w_ca_q"],
      w["w_ca_o"], w["g_ca_post"], w["g_mlp_pre"], w["w_ff_up"], w["w_ff_down"], w["g_mlp_post"])


def _post_sample_a_kernel(n_heads, v_dim, ca_scale,
                          x_ref, olat_ref, mc_ref, sg_ref, wuv_ref, wmla_ref, wmix_ref, gmixpost_ref,
                          gcapre_ref, wcaq_ref, x1_ref, qc_ref):
    n = x_ref.shape[0]
    full = _bdot(olat_ref[...], wuv_ref[...])
    full = full.reshape(n, n_heads, n_heads * v_dim)
    hidx = lax.broadcasted_iota(jnp.int32, full.shape, 1)
    lane_head = lax.broadcasted_iota(jnp.int32, full.shape, 2) // v_dim
    o = jnp.sum(jnp.where(hidx == lane_head, full, 0.0), axis=1)
    x1, qc = _mix_and_query(x_ref[...], o, mc_ref[...], sg_ref[...], wmla_ref, wmix_ref,
                            gmixpost_ref, gcapre_ref, wcaq_ref, ca_scale)
    x1_ref[...] = x1
    qc_ref[...] = qc


def _post_sample_b_kernel(ca_heads, q_ref, mk_ref, mv_ref, o_ref):
    g = q_ref.shape[0]
    rows = mk_ref.shape[1]
    s8 = 2 * ca_heads
    assert s8 == SUBLANES and mk_ref.shape[2] == LANES
    lane = lax.broadcasted_iota(jnp.int32, (s8, rows), 1)
    sub = lax.broadcasted_iota(jnp.int32, (s8, rows), 0)
    own = (lane % s8) == sub
    low_half = (lax.broadcasted_iota(jnp.int32, (1, rows), 1) % s8) < ca_heads
    for i in range(g):
        q = q_ref[i]
        q8 = jnp.concatenate([q[:, (2 * (j % ca_heads) + j // ca_heads) * LANES:
                                   (2 * (j % ca_heads) + j // ca_heads + 1) * LANES] for j in range(s8)], axis=0)
        kb = mk_ref[i].astype(BF16)
        p_all = lax.dot_general(q8.astype(BF16), kb, (((1,), (1,)), ((), ())), preferred_element_type=F32)
        d = jnp.sum(jnp.where(own, p_all, 0.0), axis=0, keepdims=True)
        d = d + jnp.where(low_half, pltpu.roll(d, rows - ca_heads, 1), pltpu.roll(d, ca_heads, 1))
        dm = jnp.where(own, jnp.broadcast_to(d, (s8, rows)), NEG_INF)
        e = jnp.exp(dm - jnp.max(dm, axis=-1, keepdims=True))
        w = (e / jnp.sum(e, axis=-1, keepdims=True)).astype(BF16)
        o8 = jnp.dot(w, mv_ref[i].astype(BF16), preferred_element_type=F32)
        for j in range(s8):
            blk = 2 * (j % ca_heads) + j // ca_heads
            o_ref[i, :, blk * LANES:(blk + 1) * LANES] = o8[j:j + 1, :]


def _post_sample_c_kernel(x1_ref, oc_ref, wcao_ref, gcapost_ref, gmlppre_ref, wup_ref, wdown_ref,
                          gmlppost_ref, y_ref):
    y_ref[...] = _mlp_tail(x1_ref[...], oc_ref[...], wcao_ref, gcapost_ref, gmlppre_ref, wup_ref,
                           wdown_ref, gmlppost_ref)


def _post_sample(xs, olat, mc, sg, mem_k, mem_v, w, n_heads, v_dim, ca_heads):
    n, dm = xs.shape
    ca_scale = float((dm // ca_heads) ** -0.5)
    kern_a = functools.partial(_post_sample_a_kernel, n_heads, v_dim, ca_scale)
    x1, qc = pl.pallas_call(
        kern_a, out_shape=(jax.ShapeDtypeStruct((n, dm), F32), jax.ShapeDtypeStruct((n, dm), F32)),
        compiler_params=_params(), name="post_sample_a",
    )(xs, olat, mc, sg, w["w_uv_flat"], w["w_mla"], w["w_mix"], w["g_mix_post"], w["g_ca_pre"], w["w_ca_q"])

    g = CA_SAMPLES_PER_STEP
    assert n % g == 0
    kern_b = functools.partial(_post_sample_b_kernel, ca_heads)
    mem_spec = pl.BlockSpec((g,) + mem_k.shape[1:], lambda i: (i, 0, 0))
    oc = pl.pallas_call(
        kern_b, grid=(n // g,),
        in_specs=[pl.BlockSpec((g, 1, dm), lambda i: (i, 0, 0)), mem_spec, mem_spec],
        out_specs=pl.BlockSpec((g, 1, dm), lambda i: (i, 0, 0)),
        out_shape=jax.ShapeDtypeStruct((n, 1, dm), F32),
        compiler_params=_params("arbitrary"), name="post_sample_b",
    )(qc.reshape(n, 1, dm), mem_k, mem_v)

    return pl.pallas_call(
        _post_sample_c_kernel, out_shape=jax.ShapeDtypeStruct((n, dm), F32),
        compiler_params=_params(), name="post_sample_c",
    )(x1, oc.reshape(n, dm), w["w_ca_o"], w["g_ca_post"], w["g_mlp_pre"], w["w_ff_up"], w["w_ff_down"],
      w["g_mlp_post"])


def _dec_attn_kernel(n_samples, n_pages, n_heads, rope_dim,
                     pt_ref, qlat_ref, qrope_ref, ckvn_ref, krn_ref, cache_ckv, cache_krt,
                     o_ref, cbuf, kbuf, sems):
    ch = DEC_PAGES_PER_STEP
    nch = n_pages // ch
    total = n_samples * nch
    page, kvr = cbuf.shape[2], cbuf.shape[3]

    def copies(g, slot):
        b = g // nch
        c = g % nch
        out = []
        for i in range(ch):
            pg = pt_ref[b, c * ch + i]
            out.append(pltpu.make_async_copy(cache_ckv.at[pg], cbuf.at[slot, i], sems.at[0, slot]))
            out.append(pltpu.make_async_copy(cache_krt.at[pg], kbuf.at[slot, i], sems.at[1, slot]))
        return out

    def start(g):
        for n, cp in enumerate(copies(g, g % DEC_NBUF)):
            cp.start(priority=(n // 2) % 2)

    def wait(g):
        for cp in copies(g, g % DEC_NBUF):
            cp.wait()

    def q_rows(b):
        row0 = pl.multiple_of(b * n_heads, n_heads)
        return qlat_ref[pl.ds(row0, n_heads), :], qrope_ref[pl.ds(row0, n_heads), :][:, :rope_dim]

    def scores(g):
        ql, qr = q_rows(g // nch)
        slot = g % DEC_NBUF
        ckv = cbuf[slot].reshape(ch * page, kvr).astype(BF16)
        qrb = qr.astype(BF16)
        s_rope = jnp.concatenate(
            [jnp.dot(qrb, kbuf[slot, i].astype(BF16), preferred_element_type=F32) for i in range(ch)], axis=1)
        return lax.dot_general(ql.astype(BF16), ckv, (((1,), (1,)), ((), ())),
                               preferred_element_type=F32) + s_rope

    for g0 in range(DEC_NBUF - 1):
        start(g0)
    wait(0)
    s0 = scores(0)

    def body(g, carry):
        s_cur, m, l, acc = carry
        b = g // nch
        c = g % nch

        @pl.when(g + 1 < total)
        def _():
            wait(g + 1)

        @pl.when(g + (DEC_NBUF - 1) < total)
        def _():
            start(g + (DEC_NBUF - 1))

        s_next = scores(jnp.minimum(g + 1, total - 1))

        ql, qr = q_rows(b)
        c_new = ckvn_ref[pl.ds(b, 1), :]
        r_new = krn_ref[pl.ds(b, 1), :][:, :rope_dim]
        s_new = jnp.sum(ql * c_new, axis=-1, keepdims=True) + jnp.sum(qr * r_new, axis=-1, keepdims=True)
        first = c == 0
        m = jnp.where(first, s_new, m)
        l = jnp.where(first, 1.0, l)
        acc = jnp.where(first, jnp.broadcast_to(c_new, acc.shape), acc)

        ckv = cbuf[g % DEC_NBUF].reshape(ch * page, kvr).astype(BF16)
        m_new = jnp.maximum(m, jnp.max(s_cur, axis=-1, keepdims=True))
        alpha = jnp.exp(m - m_new)
        p = jnp.exp(s_cur - m_new)
        l = alpha * l + jnp.sum(p, axis=-1, keepdims=True)
        acc = alpha * acc + jnp.dot(p.astype(BF16), ckv, preferred_element_type=F32)
        o_ref[pl.ds(pl.multiple_of(b * n_heads, n_heads), n_heads), :] = acc / l
        return s_next, m_new, l, acc

    init = (s0, jnp.zeros((n_heads, 1), F32), jnp.zeros((n_heads, 1), F32), jnp.zeros((n_heads, kvr), F32))
    lax.fori_loop(0, total, body, init)


def _decode_attention(page_table, qlat, qrope, ckv_new, kr_new, cache_ckv, cache_krt, n_heads, rope_dim):
    n_samples, n_pages = page_table.shape
    page, kvr = cache_ckv.shape[1], cache_ckv.shape[2]
    ch = DEC_PAGES_PER_STEP
    assert n_pages % ch == 0

    def whole(shape):
        nd = len(shape)
        return pl.BlockSpec(shape, lambda i, pt: (0,) * nd)

    kern = functools.partial(_dec_attn_kernel, n_samples, n_pages, n_heads, rope_dim)
    grid_spec = pltpu.PrefetchScalarGridSpec(
        num_scalar_prefetch=1, grid=(1,),
        in_specs=[whole(qlat.shape), whole(qrope.shape), whole(ckv_new.shape), whole(kr_new.shape),
                  pl.BlockSpec(memory_space=pl.ANY), pl.BlockSpec(memory_space=pl.ANY)],
        out_specs=whole(qlat.shape),
        scratch_shapes=[pltpu.VMEM((DEC_NBUF, ch, page, kvr), F32),
                        pltpu.VMEM((DEC_NBUF, ch, rope_dim, page), F32),
                        pltpu.SemaphoreType.DMA((2, DEC_NBUF))])
    return pl.pallas_call(
        kern, grid_spec=grid_spec, out_shape=jax.ShapeDtypeStruct(qlat.shape, F32),
        compiler_params=_params("arbitrary"), name="decode_attn",
    )(page_table, qlat, qrope, ckv_new, kr_new, cache_ckv, cache_krt)


def _rot_half(w):
    half = w.shape[-1] // 2
    return jnp.concatenate([-w[..., half:], w[..., :half]], axis=-1)


def _mem_rows(cache):
    _, n, n_mem, heads, hd = cache.shape
    assert hd == 2 * LANES
    x = cache.reshape(n, n_mem, heads, 2, LANES)
    return jnp.transpose(x, (0, 1, 3, 2, 4)).reshape(n, n_mem * 2 * heads, LANES)


def _rope_lane_freqs(rope_dim):
    inv = 1.0 / (ROPE_BASE ** (jnp.arange(0, rope_dim, 2, dtype=F32) / rope_dim))
    return jnp.concatenate([inv, inv, jnp.zeros((LANES - rope_dim,), F32)])


def _rope_tabs(pos, rope_dim, nope_dim):
    ang = pos.astype(F32)[:, None] * _rope_lane_freqs(rope_dim)[None, :]
    keep = (jnp.arange(LANES) < rope_dim + nope_dim).astype(F32)
    return jnp.cos(ang) * keep, jnp.sin(ang)


def _rope_tabs_range(seq, rope_dim, nope_dim):
    assert seq % LANES == 0
    freqs = _rope_lane_freqs(rope_dim)
    a = (jnp.arange(seq // LANES, dtype=F32) * LANES)[:, None, None] * freqs
    b = jnp.arange(LANES, dtype=F32)[None, :, None] * freqs
    ca, sa, cb, sb = jnp.cos(a), jnp.sin(a), jnp.cos(b), jnp.sin(b)
    keep = (jnp.arange(LANES) < rope_dim + nope_dim).astype(F32)
    ctab = (ca * cb - sa * sb) * keep
    stab = sa * cb + ca * sb
    return ctab.reshape(seq, LANES), stab.reshape(seq, LANES)


def kernel(x_prompt, x_sample, mem_prompt, cache_ckv, cache_krope, state_conv, cache_mem_k, cache_mem_v,
           page_table, norm_mix_pre_g, w_in, conv_w, w_conv_out, q_norm_g, w_uq, kv_norm_g, w_uk, w_uv,
           w_mla_out, w_mix_out, norm_mix_post_g, norm_ca_pre_g, mem_norm_g, w_ca_q, w_ca_k, w_ca_v, w_ca_o,
           norm_ca_post_g, norm_mlp_pre_g, w_ff_up, w_ff_down, norm_mlp_post_g):
    depth = w_in.shape[0]
    assert depth == 1, "single-layer step"
    batch, seq, dm = x_prompt.shape
    n_s, t_s, _ = x_sample.shape
    assert t_s == 1
    conv_dim = conv_w.shape[2]
    q_rank, n_heads, qk_dim = w_uq.shape[1:]
    kv_rank, _, nope_dim = w_uk.shape[1:]
    v_dim = w_uv.shape[3]
    rope_dim = qk_dim - nope_dim
    n_mem, ca_heads, ca_hd = cache_mem_k.shape[2:]
    n_pool, page = cache_ckv.shape[1:3]
    past_len = page_table.shape[1] * page
    assert rope_dim + nope_dim <= LANES and v_dim <= LANES
    dims = _PreDims(dm, conv_dim, q_rank, kv_rank, n_heads)
    scale = float(qk_dim ** -0.5)

    w0 = w_in[0]
    o_kr = 3 * conv_dim + q_rank + kv_rank
    w_kr = w0[:, o_kr:o_kr + rope_dim]
    zpad = jnp.zeros((dm, LANES - rope_dim), F32)
    w_in_kr = jnp.concatenate([w_kr, zpad, _rot_half(w_kr), zpad], axis=1).astype(BF16)
    uq = w_uq[0]
    uq_nope, uq_rope = uq[:, :, :nope_dim], uq[:, :, nope_dim:]
    hz = lambda r, width: jnp.zeros((r, n_heads, width), F32)
    pad_tail = LANES - rope_dim - nope_dim
    w_uq_a = jnp.concatenate([uq_rope, uq_nope, hz(q_rank, pad_tail)], axis=2).reshape(q_rank, n_heads * LANES)
    assert LANES % rope_dim == 0
    w_uq_b = _rot_half(uq_rope).reshape(q_rank, n_heads * rope_dim)
    uk = w_uk[0]
    w_uk_pad = jnp.concatenate([hz(kv_rank, rope_dim), uk, hz(kv_rank, pad_tail)], axis=2).reshape(
        kv_rank, n_heads * LANES)
    w_uk_t_pad = jnp.transpose(w_uk_pad.reshape(kv_rank, n_heads, LANES), (1, 2, 0)).reshape(
        n_heads * LANES, kv_rank)
    uv = w_uv[0].reshape(kv_rank, n_heads * v_dim)
    dvp = v_dim + ATTN_V_EXTRA_ROWS
    w_uv_t = jnp.concatenate([jnp.transpose(w_uv[0], (1, 2, 0)),
                              jnp.zeros((n_heads, ATTN_V_EXTRA_ROWS, kv_rank), F32)], axis=1).reshape(
        n_heads * dvp, kv_rank)
    w_mla = w_mla_out[0]
    w = {
        "g_pre": norm_mix_pre_g, "w_in_a": w0[:, :o_kr].astype(BF16), "w_in_kr": w_in_kr,
        "w_in_g": w0[:, o_kr + rope_dim:].astype(BF16), "conv_w": conv_w[0], "w_conv_out": w_conv_out[0].astype(BF16),
        "q_g": q_norm_g, "w_uq_a": w_uq_a.astype(BF16), "w_uq_b": w_uq_b.astype(BF16), "kv_g": kv_norm_g,
        "w_uk_pad": w_uk_pad.astype(BF16), "w_uk_t_pad": w_uk_t_pad.astype(BF16),
        "w_uv_t": w_uv_t.astype(BF16), "w_uv_flat": uv.astype(BF16),
        "w_mla": w_mla.astype(BF16), "w_mix": w_mix_out[0].astype(BF16),
        "g_mix_post": norm_mix_post_g, "g_ca_pre": norm_ca_pre_g,
        "w_ca_q": w_ca_q[0].reshape(dm, ca_heads * ca_hd).astype(BF16),
        "w_ca_o": w_ca_o[0].reshape(ca_heads * ca_hd, dm).astype(BF16),
        "g_ca_post": norm_ca_post_g, "g_mlp_pre": norm_mlp_pre_g, "w_ff_up": w_ff_up[0].astype(BF16),
        "w_ff_down": w_ff_down[0].astype(BF16), "g_mlp_post": norm_mlp_post_g,
    }

    mk_p, mv_p = _memory_kv(mem_prompt.reshape(batch * n_mem, dm), mem_norm_g,
                            w_ca_k[0].reshape(dm, ca_heads * ca_hd).astype(BF16),
                            w_ca_v[0].reshape(dm, ca_heads * ca_hd).astype(BF16))
    ctab_p, stab_p = _rope_tabs_range(seq, rope_dim, nope_dim)
    x2d = x_prompt.reshape(batch * seq, dm)
    q, k, vt, ckv_p, kr_p, mc_p, sg_p, conv_p = _pre_prompt(dims, scale * LOG2_E, x2d, ctab_p, stab_p, w, batch,
                                                           seq, rope_dim, v_dim, dvp)
    o_p = _prompt_attention(q, k, vt, batch, seq, n_heads, v_dim)
    y_p = _post_prompt(x2d, o_p, mc_p, sg_p, mk_p, mv_p, w, batch, seq, ca_heads, n_mem)

    ctab_s, stab_s = _rope_tabs(jnp.full((n_s,), past_len, jnp.int32), rope_dim, nope_dim)
    xs = x_sample.reshape(n_s, dm)
    q_s, qlat_s, ckv_s, kr_s, mc_s, sg_s, conv_s = _pre_sample(
        dims, scale, xs, state_conv.reshape(n_s, (CONV_WIDTH - 1) * conv_dim), ctab_s, stab_s, w)
    olat = _decode_attention(page_table, qlat_s.reshape(n_s * n_heads, kv_rank),
                             q_s.reshape(n_s * n_heads, LANES), ckv_s, kr_s,
                             cache_ckv.reshape(n_pool, page, kv_rank),
                             jnp.swapaxes(cache_krope.reshape(n_pool, page, rope_dim), 1, 2), n_heads, rope_dim)
    y_s = _post_sample(xs, olat, mc_s, sg_s, _mem_rows(cache_mem_k), _mem_rows(cache_mem_v), w, n_heads, v_dim,
                       ca_heads)

    return (y_p.reshape(batch, seq, dm),
            y_s.reshape(n_s, t_s, dm),
            ckv_p.reshape(depth, batch, seq, kv_rank),
            kr_p.reshape(depth, batch, seq, rope_dim),
            conv_p.reshape(depth, batch, CONV_WIDTH - 1, conv_dim),
            mk_p.reshape(depth, batch, n_mem, ca_heads, ca_hd),
            mv_p.reshape(depth, batch, n_mem, ca_heads, ca_hd),
            ckv_s.reshape(depth, n_s, t_s, kv_rank),
            kr_s[:, :rope_dim].reshape(depth, n_s, t_s, rope_dim),
            conv_s.reshape(depth, n_s, CONV_WIDTH - 1, conv_dim))
```

```python
import functools

import jax
import jax.numpy as jnp
from jax import lax
from jax.experimental import pallas as pl
from jax.experimental.pallas import tpu as pltpu

F32 = jnp.float32
BF16 = jnp.bfloat16

RMS_EPS = 1e-6
NEG_INF = -1e30
ROPE_BASE = 10000.0
CONV_WIDTH = 3
LOG2_E = 1.4426950408889634

LANES = 128
SUBLANES = 8
MXU_WIDTH = 256
VMEM_LIMIT_BYTES = 56 * 1024 * 1024

ROW_TILE = 512
ATTN_TQ = 1024
ATTN_TK = 512
ATTN_V_EXTRA_ROWS = 16
DEC_PAGES_PER_STEP = 32
DEC_NBUF = 5
CA_SAMPLES_PER_STEP = 8


def _rms(x, g):
    return x * lax.rsqrt(jnp.mean(x * x, axis=-1, keepdims=True) + RMS_EPS) * g


def _bdot(a, b):
    return jnp.dot(a.astype(BF16), b.astype(BF16), preferred_element_type=F32)


def _bdot_nt(a, b):
    return lax.dot_general(a.astype(BF16), b.astype(BF16), (((1,), (1,)), ((), ())),
                           preferred_element_type=F32)


def _softmax_rows(s):
    m = jnp.max(s, axis=-1, keepdims=True)
    p = jnp.exp(s - m)
    return p / jnp.sum(p, axis=-1, keepdims=True)


def _const_spec(shape):
    nd = len(shape)
    return pl.BlockSpec(shape, lambda *_: (0,) * nd, pipeline_mode=pl.Buffered(1))


def _params(*sem):
    return pltpu.CompilerParams(dimension_semantics=tuple(sem) if sem else None,
                                vmem_limit_bytes=VMEM_LIMIT_BYTES)


def _memkv_kernel(mem_ref, g_ref, wk_ref, wv_ref, k_ref, v_ref):
    mn = _rms(mem_ref[...], g_ref[...]).astype(BF16)
    k_ref[...] = jnp.dot(mn, wk_ref[...], preferred_element_type=F32)
    v_ref[...] = jnp.dot(mn, wv_ref[...], preferred_element_type=F32)


def _memory_kv(mem2d, g, wk, wv):
    rows, d = mem2d.shape
    out = jax.ShapeDtypeStruct((rows, wk.shape[1]), F32)
    return pl.pallas_call(_memkv_kernel, out_shape=(out, out), name="mem_kv",
                          compiler_params=_params())(mem2d, g, wk, wv)


class _PreDims:
    def __init__(self, d_model, conv_dim, q_rank, kv_rank, n_heads):
        self.d_model, self.conv_dim, self.q_rank, self.kv_rank = d_model, conv_dim, q_rank, kv_rank
        self.n_heads = n_heads
        c = conv_dim
        self.o_h, self.o_gb, self.o_gc = 0, c, 2 * c
        self.o_cq = 3 * c
        self.o_ckv = self.o_cq + q_rank
        self.n_a = self.o_ckv + kv_rank
        self.hw = n_heads * LANES


def _pre_common(dims, x, ctab, stab, gpre_ref, wa_ref, wkr_ref, wg_ref, qg_ref, wuqa_ref, wuqb_ref, kvg_ref,
                scale):
    d = dims
    xn = _rms(x, gpre_ref[...]).astype(BF16)

    def proj(w_ref, lo, hi):
        return jnp.dot(xn, w_ref[:, lo:hi], preferred_element_type=F32)

    h = proj(wa_ref, d.o_h, d.o_gb)
    gate_b = proj(wa_ref, d.o_gb, d.o_gc)
    gate_c = proj(wa_ref, d.o_gc, d.o_cq)
    cq = proj(wa_ref, d.o_cq, d.o_ckv)
    ckv = proj(wa_ref, d.o_ckv, d.n_a)
    kr2 = proj(wkr_ref, 0, 2 * LANES)
    kra, krb = kr2[:, :LANES], kr2[:, LANES:]
    g_conv = proj(wg_ref, 0, d.d_model)
    g_mla = proj(wg_ref, d.d_model, 2 * d.d_model)

    u = gate_c * h
    cqn = _rms(cq, qg_ref[...]).astype(BF16)
    qa = jnp.dot(cqn, wuqa_ref[...], preferred_element_type=F32)
    qbc = jnp.dot(cqn, wuqb_ref[...], preferred_element_type=F32)
    r = wuqb_ref.shape[1] // d.n_heads
    pieces = []
    for hd in range(d.n_heads):
        blk, off = divmod(hd * r, LANES)
        piece = qbc[:, blk * LANES:(blk + 1) * LANES]
        pieces.append(pltpu.roll(piece, LANES - off, 1) if off else piece)
    qb = jnp.concatenate(pieces, axis=1)
    ct = jnp.concatenate([ctab] * d.n_heads, axis=1)
    st = jnp.concatenate([stab] * d.n_heads, axis=1)
    q = (qa * ct + qb * st) * scale
    ckvn = _rms(ckv, kvg_ref[...])
    krr = kra * ctab + krb * stab
    return u, gate_b, g_conv, g_mla, q, ckvn, krr


def _pre_prompt_kernel(dims, scale, tm, dv, dvp,
                       x_ref, ctab_ref, stab_ref, gpre_ref, wa_ref, wkr_ref, wg_ref, convw_ref, wco_ref, qg_ref,
                       wuqa_ref, wuqb_ref, kvg_ref, wuk_ref, wuvt_ref,
                       q_out, k_out, vt_out, ckv_out, kr_out, mc_out, sg_out, conv_out, ubuf):
    hist = SUBLANES

    @pl.when(pl.program_id(1) == 0)
    def _():
        ubuf[0:hist, :] = jnp.zeros((hist, dims.conv_dim), F32)

    u, gate_b, g_conv, g_mla, q, ckvn, krr = _pre_common(
        dims, x_ref[...], ctab_ref[...], stab_ref[...], gpre_ref, wa_ref, wkr_ref, wg_ref, qg_ref, wuqa_ref,
        wuqb_ref, kvg_ref, scale)

    ubuf[hist:hist + tm, :] = u
    u1 = ubuf[hist - 1:hist - 1 + tm, :]
    u2 = ubuf[hist - 2:hist - 2 + tm, :]
    cw = convw_ref[...]
    conv = u2 * cw[0:1, :] + u1 * cw[1:2, :] + u * cw[2:3, :]
    y_conv = _bdot(gate_b * conv, wco_ref[...])
    mc_out[...] = (jax.nn.sigmoid(g_conv) * y_conv).astype(mc_out.dtype)
    sg_out[...] = jax.nn.sigmoid(g_mla).astype(sg_out.dtype)
    conv_out[...] = ubuf[hist + tm - (CONV_WIDTH - 1):hist + tm, :]
    ubuf[0:hist, :] = ubuf[tm:tm + hist, :]

    q_out[...] = q.astype(q_out.dtype)
    ckv_out[...] = ckvn
    kr_out[...] = krr[:, :kr_out.shape[-1]]
    ckvb = ckvn.astype(BF16)
    ka = jnp.dot(ckvb, wuk_ref[...], preferred_element_type=F32)
    k = ka + jnp.concatenate([krr] * dims.n_heads, axis=1)
    k_out[...] = k.astype(k_out.dtype)
    vt = lax.dot_general(wuvt_ref[...], ckvb, (((1,), (1,)), ((), ())), preferred_element_type=F32)
    rid = lax.broadcasted_iota(jnp.int32, (vt.shape[0], 1), 0)
    vt_out[...] = (vt + jnp.where(rid % dvp == dv, 1.0, 0.0)).astype(vt_out.dtype)


def _pre_prompt(dims, scale, x2d, ctab, stab, w, batch, seq, rope_dim, dv, dvp):
    tm = ROW_TILE
    assert seq % tm == 0
    nt = seq // tm
    rows = batch * seq
    d = dims
    hv = w["w_uv_t"].shape[0]

    def row_spec(width):
        return pl.BlockSpec((tm, width), lambda b, t: (b * nt + t, 0))

    def tab_spec():
        return pl.BlockSpec((tm, LANES), lambda b, t: (t, 0))

    in_specs = [row_spec(d.d_model), tab_spec(), tab_spec(),
                _const_spec((1, d.d_model)), _const_spec((d.d_model, d.n_a)),
                _const_spec((d.d_model, 2 * LANES)), _const_spec((d.d_model, 2 * d.d_model)),
                _const_spec((CONV_WIDTH, d.conv_dim)), _const_spec((d.conv_dim, d.d_model)),
                _const_spec((1, d.q_rank)), _const_spec((d.q_rank, d.hw)),
                _const_spec((d.q_rank, d.n_heads * rope_dim)),
                _const_spec((1, d.kv_rank)), _const_spec((d.kv_rank, d.hw)), _const_spec((hv, d.kv_rank))]
    out_shape = (jax.ShapeDtypeStruct((rows, d.hw), BF16),
                 jax.ShapeDtypeStruct((rows, d.hw), BF16),
                 jax.ShapeDtypeStruct((batch, hv, seq), BF16),
                 jax.ShapeDtypeStruct((rows, d.kv_rank), F32),
                 jax.ShapeDtypeStruct((rows, rope_dim), F32),
                 jax.ShapeDtypeStruct((rows, d.d_model), BF16),
                 jax.ShapeDtypeStruct((rows, d.d_model), BF16),
                 jax.ShapeDtypeStruct((batch, CONV_WIDTH - 1, d.conv_dim), F32))
    out_specs = (row_spec(d.hw), row_spec(d.hw),
                 pl.BlockSpec((None, hv, tm), lambda b, t: (b, 0, t)), row_spec(d.kv_rank),
                 row_spec(rope_dim), row_spec(d.d_model), row_spec(d.d_model),
                 pl.BlockSpec((None, CONV_WIDTH - 1, d.conv_dim), lambda b, t: (b, 0, 0)))
    kern = functools.partial(_pre_prompt_kernel, dims, scale, tm, dv, dvp)
    return pl.pallas_call(
        kern, grid=(batch, nt), in_specs=in_specs, out_specs=out_specs, out_shape=out_shape,
        scratch_shapes=[pltpu.VMEM((tm + SUBLANES, d.conv_dim), F32)],
        compiler_params=_params("arbitrary", "arbitrary"), name="pre_prompt",
    )(x2d, ctab, stab, w["g_pre"], w["w_in_a"], w["w_in_kr"], w["w_in_g"], w["conv_w"], w["w_conv_out"], w["q_g"],
      w["w_uq_a"], w["w_uq_b"], w["kv_g"], w["w_uk_pad"], w["w_uv_t"])


def _pre_sample_kernel(dims, scale,
                       x_ref, state_ref, ctab_ref, stab_ref, gpre_ref, wa_ref, wkr_ref, wg_ref, convw_ref, wco_ref,
                       qg_ref, wuqa_ref, wuqb_ref, kvg_ref, wukt_ref,
                       q_out, qlat_out, ckv_out, kr_out, mc_out, sg_out, conv_out):
    c = dims.conv_dim
    u, gate_b, g_conv, g_mla, q, ckvn, krr = _pre_common(
        dims, x_ref[...], ctab_ref[...], stab_ref[...], gpre_ref, wa_ref, wkr_ref, wg_ref, qg_ref, wuqa_ref,
        wuqb_ref, kvg_ref, scale)
    u2 = state_ref[:, 0:c]
    u1 = state_ref[:, c:2 * c]
    cw = convw_ref[...]
    conv = u2 * cw[0:1, :] + u1 * cw[1:2, :] + u * cw[2:3, :]
    y_conv = _bdot(gate_b * conv, wco_ref[...])
    mc_out[...] = jax.nn.sigmoid(g_conv) * y_conv
    sg_out[...] = jax.nn.sigmoid(g_mla)
    conv_out[:, 0:c] = u1
    conv_out[:, c:2 * c] = u
    q_out[...] = q
    ckv_out[...] = ckvn
    kr_out[...] = krr
    kvr = dims.kv_rank
    qb16 = q.astype(BF16)
    for hd in range(dims.n_heads):
        blk = qb16[:, hd * LANES:(hd + 1) * LANES]
        qlat_out[:, hd * kvr:(hd + 1) * kvr] = jnp.dot(
            blk, wukt_ref[hd * LANES:(hd + 1) * LANES, :], preferred_element_type=F32)


def _pre_sample(dims, scale, xs, state2d, ctab, stab, w):
    n = xs.shape[0]
    d = dims
    out_shape = (jax.ShapeDtypeStruct((n, d.hw), F32),
                 jax.ShapeDtypeStruct((n, d.n_heads * d.kv_rank), F32),
                 jax.ShapeDtypeStruct((n, d.kv_rank), F32),
                 jax.ShapeDtypeStruct((n, LANES), F32),
                 jax.ShapeDtypeStruct((n, d.d_model), F32),
                 jax.ShapeDtypeStruct((n, d.d_model), F32),
                 jax.ShapeDtypeStruct((n, (CONV_WIDTH - 1) * d.conv_dim), F32))
    kern = functools.partial(_pre_sample_kernel, dims, scale)
    return pl.pallas_call(kern, out_shape=out_shape, compiler_params=_params(), name="pre_sample")(
        xs, state2d, ctab, stab, w["g_pre"], w["w_in_a"], w["w_in_kr"], w["w_in_g"], w["conv_w"], w["w_conv_out"],
        w["q_g"],
        w["w_uq_a"], w["w_uq_b"], w["kv_g"], w["w_uk_t_pad"])


def _attn_kernel(tq, tk, hp, dv, dvp, nq, q_ref, k_ref, vt_ref, o_ref, sa_ref, sb_ref):
    assert tq == 2 * tk
    half = tq // 2

    def tile(qi, carry):
        q0 = pl.multiple_of(qi * tq, tq)

        def scores(j, dst, q_lo=0, qbase=None):
            start = pl.multiple_of(j * tk, tk)
            qb = q0 if qbase is None else qbase
            for h in range(hp):
                kh = k_ref[pl.ds(start, tk), h * LANES:(h + 1) * LANES]
                qh = q_ref[pl.ds(pl.multiple_of(qb + q_lo, half), tq - q_lo), h * LANES:(h + 1) * LANES]
                dst[h, :, 0:tq - q_lo] = lax.dot_general(kh, qh, (((1,), (1,)), ((), ())),
                                                         preferred_element_type=F32)

        def process(j, src, state, masked, q_lo=0):
            start = pl.multiple_of(j * tk, tk)
            n = tq - q_lo
            new_state = []
            for h in range(hp):
                m_prev, acc = state[h]
                st = src[h, :, 0:n]
                if masked:
                    key = lax.broadcasted_iota(jnp.int32, (tk, n), 0) + j * tk
                    qry = lax.broadcasted_iota(jnp.int32, (tk, n), 1) + qi * tq + q_lo
                    st = jnp.where(key <= qry, st, NEG_INF)
                m_new = jnp.maximum(m_prev[:, q_lo:], jnp.max(st, axis=0, keepdims=True))
                alpha = jnp.exp2(m_prev[:, q_lo:] - m_new)
                p = jnp.exp2(st - m_new).astype(BF16)
                vth = vt_ref[h * dvp:(h + 1) * dvp, pl.ds(start, tk)]
                acc_new = alpha * acc[:, q_lo:] + jnp.dot(vth, p, preferred_element_type=F32)
                if q_lo:
                    m_new = jnp.concatenate([m_prev[:, :q_lo], m_new], axis=1)
                    acc_new = jnp.concatenate([acc[:, :q_lo], acc_new], axis=1)
                new_state.append((m_new, acc_new))
            return tuple(new_state)

        init = tuple((jnp.full((1, tq), NEG_INF, F32), jnp.zeros((dvp, tq), F32)) for _ in range(hp))

        @pl.when(qi == 0)
        def _():
            scores(0, sa_ref)

        def fused(j, src, dst, state):
            start = pl.multiple_of(j * tk, tk)
            start1 = pl.multiple_of((j + 1) * tk, tk)
            new_state = []
            for h in range(hp):
                m_prev, acc = state[h]
                st = src[h]
                m_new = jnp.maximum(m_prev, jnp.max(st, axis=0, keepdims=True))
                alpha = jnp.exp2(m_prev - m_new)
                p = jnp.exp2(st - m_new).astype(BF16)
                vth = vt_ref[h * dvp:(h + 1) * dvp, pl.ds(start, tk)]
                kh = k_ref[pl.ds(start1, tk), h * LANES:(h + 1) * LANES]
                pieces = []
                for n in range(tq // MXU_WIDTH):
                    ql = slice(n * MXU_WIDTH, (n + 1) * MXU_WIDTH)
                    qh = q_ref[pl.ds(pl.multiple_of(q0 + n * MXU_WIDTH, MXU_WIDTH), MXU_WIDTH),
                               h * LANES:(h + 1) * LANES]
                    dst[h, :, ql] = lax.dot_general(kh, qh, (((1,), (1,)), ((), ())), preferred_element_type=F32)
                    pieces.append(alpha[:, ql] * acc[:, ql] + jnp.dot(vth, p[:, ql], preferred_element_type=F32))
                new_state.append((m_new, jnp.concatenate(pieces, axis=1)))
            return tuple(new_state)

        def pair(t, state):
            state = fused(2 * t, sa_ref, sb_ref, state)
            return fused(2 * t + 1, sb_ref, sa_ref, state)

        state = lax.fori_loop(0, qi, pair, init)
        scores(2 * qi + 1, sb_ref, q_lo=half)
        state = process(2 * qi, sa_ref, state, True)
        scores(0, sa_ref, qbase=pl.multiple_of(jnp.minimum(qi + 1, nq - 1) * tq, tq))
        state = process(2 * qi + 1, sb_ref, state, True, q_lo=half)
        ot = jnp.concatenate([acc[0:dv] / acc[dv:dv + 1] for (_, acc) in state], axis=0)
        o_ref[pl.ds(q0, tq), :] = ot.T.astype(o_ref.dtype)
        return carry

    lax.fori_loop(0, nq, tile, 0)


def _prompt_attention(q, k, vt, batch, seq, n_heads, dv):
    tq, tk = ATTN_TQ, ATTN_TK
    assert seq % tq == 0 and tq % tk == 0
    hp = LANES // dv
    assert n_heads % hp == 0
    dvp = vt.shape[1] // n_heads
    nq = seq // tq
    kern = functools.partial(_attn_kernel, tq, tk, hp, dv, dvp, nq)
    seq_spec = pl.BlockSpec((seq, hp * LANES), lambda b, g: (b, g))
    return pl.pallas_call(
        kern, grid=(batch, n_heads // hp),
        in_specs=[seq_spec, seq_spec, pl.BlockSpec((None, hp * dvp, seq), lambda b, g: (b, g, 0))],
        out_specs=pl.BlockSpec((seq, hp * dv), lambda b, g: (b, g)),
        out_shape=jax.ShapeDtypeStruct((batch * seq, n_heads * dv), BF16),
        scratch_shapes=[pltpu.VMEM((hp, tk, tq), F32), pltpu.VMEM((hp, tk, tq), F32)],
        compiler_params=_params("arbitrary", "arbitrary"), name="prompt_attn",
    )(q, k, vt)


def _mix_and_query(x, o, mc, sg, wmla_ref, wmix_ref, gmixpost_ref, gcapre_ref, wcaq_ref, ca_scale):
    y_mla = _bdot(o, wmla_ref[...])
    merged = mc.astype(F32) + sg.astype(F32) * y_mla
    y = _bdot(merged, wmix_ref[...])
    x1 = x + _rms(y, gmixpost_ref[...])
    qc = _bdot(_rms(x1, gcapre_ref[...]), wcaq_ref[...]) * ca_scale
    return x1, qc


def _mlp_tail(x1, oc, wcao_ref, gcapost_ref, gmlppre_ref, wup_ref, wdown_ref, gmlppost_ref):
    ca = _bdot(oc, wcao_ref[...])
    x2 = x1 + _rms(ca, gcapost_ref[...])
    hid = jnp.square(jnp.maximum(_bdot(_rms(x2, gmlppre_ref[...]), wup_ref[...]), 0.0))
    return x2 + _rms(_bdot(hid, wdown_ref[...]), gmlppost_ref[...])


def _post_prompt_kernel(ca_heads, ca_scale,
                        x_ref, o_ref, mc_ref, sg_ref, mk_ref, mv_ref,
                        wmla_ref, wmix_ref, gmixpost_ref, gcapre_ref, wcaq_ref, wcao_ref, gcapost_ref,
                        gmlppre_ref, wup_ref, wdown_ref, gmlppost_ref, y_ref):
    x1, qc = _mix_and_query(x_ref[...], o_ref[...], mc_ref[...], sg_ref[...], wmla_ref, wmix_ref,
                            gmixpost_ref, gcapre_ref, wcaq_ref, ca_scale)
    hd = qc.shape[1] // ca_heads
    outs = []
    for h in range(ca_heads):
        sl = slice(h * hd, (h + 1) * hd)
        p = _softmax_rows(_bdot_nt(qc[:, sl], mk_ref[:, sl]))
        outs.append(_bdot(p, mv_ref[:, sl]))
    oc = jnp.concatenate(outs, axis=1)
    y_ref[...] = _mlp_tail(x1, oc, wcao_ref, gcapost_ref, gmlppre_ref, wup_ref, wdown_ref, gmlppost_ref)


def _post_prompt(x2d, o, mc, sg, mk, mv, w, batch, seq, ca_heads, n_mem):
    tm = ROW_TILE
    nt = seq // tm
    rows, dm = x2d.shape
    dff = w["w_ff_up"].shape[1]
    ca_scale = float((dm // ca_heads) ** -0.5)

    def row_spec(width):
        return pl.BlockSpec((tm, width), lambda i: (i, 0))

    mem_spec = pl.BlockSpec((n_mem, dm), lambda i: (i // nt, 0))
    in_specs = [row_spec(dm), row_spec(o.shape[1]), row_spec(dm), row_spec(dm), mem_spec, mem_spec,
                _const_spec((o.shape[1], dm)), _const_spec((dm, dm)), _const_spec((1, dm)),
                _const_spec((1, dm)), _const_spec((dm, dm)), _const_spec((dm, dm)), _const_spec((1, dm)),
                _const_spec((1, dm)), _const_spec((dm, dff)), _const_spec((dff, dm)), _const_spec((1, dm))]
    kern = functools.partial(_post_prompt_kernel, ca_heads, ca_scale)
    return pl.pallas_call(
        kern, grid=(rows // tm,), in_specs=in_specs, out_specs=row_spec(dm),
        out_shape=jax.ShapeDtypeStruct((rows, dm), F32),
        compiler_params=_params("arbitrary"), name="post_prompt",
    )(x2d, o, mc, sg, mk, mv, w["w_mla"], w["w_mix"], w["g_mix_post"], w["g_ca_pre"], w["w_ca_q"],
      w["w_ca_o"], w["g_ca_post"], w["g_mlp_pre"], w["w_ff_up"], w["w_ff_down"], w["g_mlp_post"])


def _post_sample_a_kernel(n_heads, v_dim, ca_scale,
                          x_ref, olat_ref, mc_ref, sg_ref, wuv_ref, wmla_ref, wmix_ref, gmixpost_ref,
                          gcapre_ref, wcaq_ref, x1_ref, qc_ref):
    n = x_ref.shape[0]
    full = _bdot(olat_ref[...], wuv_ref[...])
    full = full.reshape(n, n_heads, n_heads * v_dim)
    hidx = lax.broadcasted_iota(jnp.int32, full.shape, 1)
    lane_head = lax.broadcasted_iota(jnp.int32, full.shape, 2) // v_dim
    o = jnp.sum(jnp.where(hidx == lane_head, full, 0.0), axis=1)
    x1, qc = _mix_and_query(x_ref[...], o, mc_ref[...], sg_ref[...], wmla_ref, wmix_ref,
                            gmixpost_ref, gcapre_ref, wcaq_ref, ca_scale)
    x1_ref[...] = x1
    qc_ref[...] = qc


def _post_sample_b_kernel(ca_heads, q_ref, mk_ref, mv_ref, o_ref):
    g = q_ref.shape[0]
    rows = mk_ref.shape[1]
    s8 = 2 * ca_heads
    assert s8 == SUBLANES and mk_ref.shape[2] == LANES
    lane = lax.broadcasted_iota(jnp.int32, (s8, rows), 1)
    sub = lax.broadcasted_iota(jnp.int32, (s8, rows), 0)
    own = (lane % s8) == sub
    low_half = (lax.broadcasted_iota(jnp.int32, (1, rows), 1) % s8) < ca_heads
    for i in range(g):
        q = q_ref[i]
        q8 = jnp.concatenate([q[:, (2 * (j % ca_heads) + j // ca_heads) * LANES:
                                   (2 * (j % ca_heads) + j // ca_heads + 1) * LANES] for j in range(s8)], axis=0)
        kb = mk_ref[i].astype(BF16)
        p_all = lax.dot_general(q8.astype(BF16), kb, (((1,), (1,)), ((), ())), preferred_element_type=F32)
        d = jnp.sum(jnp.where(own, p_all, 0.0), axis=0, keepdims=True)
        d = d + jnp.where(low_half, pltpu.roll(d, rows - ca_heads, 1), pltpu.roll(d, ca_heads, 1))
        dm = jnp.where(own, jnp.broadcast_to(d, (s8, rows)), NEG_INF)
        e = jnp.exp(dm - jnp.max(dm, axis=-1, keepdims=True))
        w = (e / jnp.sum(e, axis=-1, keepdims=True)).astype(BF16)
        o8 = jnp.dot(w, mv_ref[i].astype(BF16), preferred_element_type=F32)
        for j in range(s8):
            blk = 2 * (j % ca_heads) + j // ca_heads
            o_ref[i, :, blk * LANES:(blk + 1) * LANES] = o8[j:j + 1, :]


def _post_sample_c_kernel(x1_ref, oc_ref, wcao_ref, gcapost_ref, gmlppre_ref, wup_ref, wdown_ref,
                          gmlppost_ref, y_ref):
    y_ref[...] = _mlp_tail(x1_ref[...], oc_ref[...], wcao_ref, gcapost_ref, gmlppre_ref, wup_ref,
                           wdown_ref, gmlppost_ref)


def _post_sample(xs, olat, mc, sg, mem_k, mem_v, w, n_heads, v_dim, ca_heads):
    n, dm = xs.shape
    ca_scale = float((dm // ca_heads) ** -0.5)
    kern_a = functools.partial(_post_sample_a_kernel, n_heads, v_dim, ca_scale)
    x1, qc = pl.pallas_call(
        kern_a, out_shape=(jax.ShapeDtypeStruct((n, dm), F32), jax.ShapeDtypeStruct((n, dm), F32)),
        compiler_params=_params(), name="post_sample_a",
    )(xs, olat, mc, sg, w["w_uv_flat"], w["w_mla"], w["w_mix"], w["g_mix_post"], w["g_ca_pre"], w["w_ca_q"])

    g = CA_SAMPLES_PER_STEP
    assert n % g == 0
    kern_b = functools.partial(_post_sample_b_kernel, ca_heads)
    mem_spec = pl.BlockSpec((g,) + mem_k.shape[1:], lambda i: (i, 0, 0))
    oc = pl.pallas_call(
        kern_b, grid=(n // g,),
        in_specs=[pl.BlockSpec((g, 1, dm), lambda i: (i, 0, 0)), mem_spec, mem_spec],
        out_specs=pl.BlockSpec((g, 1, dm), lambda i: (i, 0, 0)),
        out_shape=jax.ShapeDtypeStruct((n, 1, dm), F32),
        compiler_params=_params("arbitrary"), name="post_sample_b",
    )(qc.reshape(n, 1, dm), mem_k, mem_v)

    return pl.pallas_call(
        _post_sample_c_kernel, out_shape=jax.ShapeDtypeStruct((n, dm), F32),
        compiler_params=_params(), name="post_sample_c",
    )(x1, oc.reshape(n, dm), w["w_ca_o"], w["g_ca_post"], w["g_mlp_pre"], w["w_ff_up"], w["w_ff_down"],
      w["g_mlp_post"])


def _dec_attn_kernel(n_samples, n_pages, n_heads, rope_dim,
                     pt_ref, qlat_ref, qrope_ref, ckvn_ref, krn_ref, cache_ckv, cache_krt,
                     o_ref, cbuf, kbuf, sems):
    ch = DEC_PAGES_PER_STEP
    nch = n_pages // ch
    total = n_samples * nch
    page, kvr = cbuf.shape[2], cbuf.shape[3]

    def copies(g, slot):
        b = g // nch
        c = g % nch
        out = []
        for i in range(ch):
            pg = pt_ref[b, c * ch + i]
            out.append(pltpu.make_async_copy(cache_ckv.at[pg], cbuf.at[slot, i], sems.at[0, slot]))
            out.append(pltpu.make_async_copy(cache_krt.at[pg], kbuf.at[slot, i], sems.at[1, slot]))
        return out

    def start(g):
        for n, cp in enumerate(copies(g, g % DEC_NBUF)):
            cp.start(priority=(n // 2) % 2)

    def wait(g):
        for cp in copies(g, g % DEC_NBUF):
            cp.wait()

    def q_rows(b):
        row0 = pl.multiple_of(b * n_heads, n_heads)
        return qlat_ref[pl.ds(row0, n_heads), :], qrope_ref[pl.ds(row0, n_heads), :][:, :rope_dim]

    def scores(g):
        ql, qr = q_rows(g // nch)
        slot = g % DEC_NBUF
        ckv = cbuf[slot].reshape(ch * page, kvr).astype(BF16)
        qrb = qr.astype(BF16)
        s_rope = jnp.concatenate(
            [jnp.dot(qrb, kbuf[slot, i].astype(BF16), preferred_element_type=F32) for i in range(ch)], axis=1)
        return lax.dot_general(ql.astype(BF16), ckv, (((1,), (1,)), ((), ())),
                               preferred_element_type=F32) + s_rope

    for g0 in range(DEC_NBUF - 1):
        start(g0)
    wait(0)
    s0 = scores(0)

    def body(g, carry):
        s_cur, m, l, acc = carry
        b = g // nch
        c = g % nch

        @pl.when(g + 1 < total)
        def _():
            wait(g + 1)

        @pl.when(g + (DEC_NBUF - 1) < total)
        def _():
            start(g + (DEC_NBUF - 1))

        s_next = scores(jnp.minimum(g + 1, total - 1))

        ql, qr = q_rows(b)
        c_new = ckvn_ref[pl.ds(b, 1), :]
        r_new = krn_ref[pl.ds(b, 1), :][:, :rope_dim]
        s_new = jnp.sum(ql * c_new, axis=-1, keepdims=True) + jnp.sum(qr * r_new, axis=-1, keepdims=True)
        first = c == 0
        m = jnp.where(first, s_new, m)
        l = jnp.where(first, 1.0, l)
        acc = jnp.where(first, jnp.broadcast_to(c_new, acc.shape), acc)

        ckv = cbuf[g % DEC_NBUF].reshape(ch * page, kvr).astype(BF16)
        m_new = jnp.maximum(m, jnp.max(s_cur, axis=-1, keepdims=True))
        alpha = jnp.exp(m - m_new)
        p = jnp.exp(s_cur - m_new)
        l = alpha * l + jnp.sum(p, axis=-1, keepdims=True)
        acc = alpha * acc + jnp.dot(p.astype(BF16), ckv, preferred_element_type=F32)
        o_ref[pl.ds(pl.multiple_of(b * n_heads, n_heads), n_heads), :] = acc / l
        return s_next, m_new, l, acc

    init = (s0, jnp.zeros((n_heads, 1), F32), jnp.zeros((n_heads, 1), F32), jnp.zeros((n_heads, kvr), F32))
    lax.fori_loop(0, total, body, init)


def _decode_attention(page_table, qlat, qrope, ckv_new, kr_new, cache_ckv, cache_krt, n_heads, rope_dim):
    n_samples, n_pages = page_table.shape
    page, kvr = cache_ckv.shape[1], cache_ckv.shape[2]
    ch = DEC_PAGES_PER_STEP
    assert n_pages % ch == 0

    def whole(shape):
        nd = len(shape)
        return pl.BlockSpec(shape, lambda i, pt: (0,) * nd)

    kern = functools.partial(_dec_attn_kernel, n_samples, n_pages, n_heads, rope_dim)
    grid_spec = pltpu.PrefetchScalarGridSpec(
        num_scalar_prefetch=1, grid=(1,),
        in_specs=[whole(qlat.shape), whole(qrope.shape), whole(ckv_new.shape), whole(kr_new.shape),
                  pl.BlockSpec(memory_space=pl.ANY), pl.BlockSpec(memory_space=pl.ANY)],
        out_specs=whole(qlat.shape),
        scratch_shapes=[pltpu.VMEM((DEC_NBUF, ch, page, kvr), F32),
                        pltpu.VMEM((DEC_NBUF, ch, rope_dim, page), F32),
                        pltpu.SemaphoreType.DMA((2, DEC_NBUF))])
    return pl.pallas_call(
        kern, grid_spec=grid_spec, out_shape=jax.ShapeDtypeStruct(qlat.shape, F32),
        compiler_params=_params("arbitrary"), name="decode_attn",
    )(page_table, qlat, qrope, ckv_new, kr_new, cache_ckv, cache_krt)


def _rot_half(w):
    half = w.shape[-1] // 2
    return jnp.concatenate([-w[..., half:], w[..., :half]], axis=-1)


def _mem_rows(cache):
    _, n, n_mem, heads, hd = cache.shape
    assert hd == 2 * LANES
    x = cache.reshape(n, n_mem, heads, 2, LANES)
    return jnp.transpose(x, (0, 1, 3, 2, 4)).reshape(n, n_mem * 2 * heads, LANES)


def _rope_lane_freqs(rope_dim):
    inv = 1.0 / (ROPE_BASE ** (jnp.arange(0, rope_dim, 2, dtype=F32) / rope_dim))
    return jnp.concatenate([inv, inv, jnp.zeros((LANES - rope_dim,), F32)])


def _rope_tabs(pos, rope_dim, nope_dim):
    ang = pos.astype(F32)[:, None] * _rope_lane_freqs(rope_dim)[None, :]
    keep = (jnp.arange(LANES) < rope_dim + nope_dim).astype(F32)
    return jnp.cos(ang) * keep, jnp.sin(ang)


def _rope_tabs_range(seq, rope_dim, nope_dim):
    assert seq % LANES == 0
    freqs = _rope_lane_freqs(rope_dim)
    a = (jnp.arange(seq // LANES, dtype=F32) * LANES)[:, None, None] * freqs
    b = jnp.arange(LANES, dtype=F32)[None, :, None] * freqs
    ca, sa, cb, sb = jnp.cos(a), jnp.sin(a), jnp.cos(b), jnp.sin(b)
    keep = (jnp.arange(LANES) < rope_dim + nope_dim).astype(F32)
    ctab = (ca * cb - sa * sb) * keep
    stab = sa * cb + ca * sb
    return ctab.reshape(seq, LANES), stab.reshape(seq, LANES)


def kernel(x_prompt, x_sample, mem_prompt, cache_ckv, cache_krope, state_conv, cache_mem_k, cache_mem_v,
           page_table, norm_mix_pre_g, w_in, conv_w, w_conv_out, q_norm_g, w_uq, kv_norm_g, w_uk, w_uv,
           w_mla_out, w_mix_out, norm_mix_post_g, norm_ca_pre_g, mem_norm_g, w_ca_q, w_ca_k, w_ca_v, w_ca_o,
           norm_ca_post_g, norm_mlp_pre_g, w_ff_up, w_ff_down, norm_mlp_post_g):
    depth = w_in.shape[0]
    assert depth == 1, "single-layer step"
    batch, seq, dm = x_prompt.shape
    n_s, t_s, _ = x_sample.shape
    assert t_s == 1
    conv_dim = conv_w.shape[2]
    q_rank, n_heads, qk_dim = w_uq.shape[1:]
    kv_rank, _, nope_dim = w_uk.shape[1:]
    v_dim = w_uv.shape[3]
    rope_dim = qk_dim - nope_dim
    n_mem, ca_heads, ca_hd = cache_mem_k.shape[2:]
    n_pool, page = cache_ckv.shape[1:3]
    past_len = page_table.shape[1] * page
    assert rope_dim + nope_dim <= LANES and v_dim <= LANES
    dims = _PreDims(dm, conv_dim, q_rank, kv_rank, n_heads)
    scale = float(qk_dim ** -0.5)

    w0 = w_in[0]
    o_kr = 3 * conv_dim + q_rank + kv_rank
    w_kr = w0[:, o_kr:o_kr + rope_dim]
    zpad = jnp.zeros((dm, LANES - rope_dim), F32)
    w_in_kr = jnp.concatenate([w_kr, zpad, _rot_half(w_kr), zpad], axis=1).astype(BF16)
    uq = w_uq[0]
    uq_nope, uq_rope = uq[:, :, :nope_dim], uq[:, :, nope_dim:]
    hz = lambda r, width: jnp.zeros((r, n_heads, width), F32)
    pad_tail = LANES - rope_dim - nope_dim
    w_uq_a = jnp.concatenate([uq_rope, uq_nope, hz(q_rank, pad_tail)], axis=2).reshape(q_rank, n_heads * LANES)
    assert LANES % rope_dim == 0
    w_uq_b = _rot_half(uq_rope).reshape(q_rank, n_heads * rope_dim)
    uk = w_uk[0]
    w_uk_pad = jnp.concatenate([hz(kv_rank, rope_dim), uk, hz(kv_rank, pad_tail)], axis=2).reshape(
        kv_rank, n_heads * LANES)
    w_uk_t_pad = jnp.transpose(w_uk_pad.reshape(kv_rank, n_heads, LANES), (1, 2, 0)).reshape(
        n_heads * LANES, kv_rank)
    uv = w_uv[0].reshape(kv_rank, n_heads * v_dim)
    dvp = v_dim + ATTN_V_EXTRA_ROWS
    w_uv_t = jnp.concatenate([jnp.transpose(w_uv[0], (1, 2, 0)),
                              jnp.zeros((n_heads, ATTN_V_EXTRA_ROWS, kv_rank), F32)], axis=1).reshape(
        n_heads * dvp, kv_rank)
    w_mla = w_mla_out[0]
    w = {
        "g_pre": norm_mix_pre_g, "w_in_a": w0[:, :o_kr].astype(BF16), "w_in_kr": w_in_kr,
        "w_in_g": w0[:, o_kr + rope_dim:].astype(BF16), "conv_w": conv_w[0], "w_conv_out": w_conv_out[0].astype(BF16),
        "q_g": q_norm_g, "w_uq_a": w_uq_a.astype(BF16), "w_uq_b": w_uq_b.astype(BF16), "kv_g": kv_norm_g,
        "w_uk_pad": w_uk_pad.astype(BF16), "w_uk_t_pad": w_uk_t_pad.astype(BF16),
        "w_uv_t": w_uv_t.astype(BF16), "w_uv_flat": uv.astype(BF16),
        "w_mla": w_mla.astype(BF16), "w_mix": w_mix_out[0].astype(BF16),
        "g_mix_post": norm_mix_post_g, "g_ca_pre": norm_ca_pre_g,
        "w_ca_q": w_ca_q[0].reshape(dm, ca_heads * ca_hd).astype(BF16),
        "w_ca_o": w_ca_o[0].reshape(ca_heads * ca_hd, dm).astype(BF16),
        "g_ca_post": norm_ca_post_g, "g_mlp_pre": norm_mlp_pre_g, "w_ff_up": w_ff_up[0].astype(BF16),
        "w_ff_down": w_ff_down[0].astype(BF16), "g_mlp_post": norm_mlp_post_g,
    }

    mk_p, mv_p = _memory_kv(mem_prompt.reshape(batch * n_mem, dm), mem_norm_g,
                            w_ca_k[0].reshape(dm, ca_heads * ca_hd).astype(BF16),
                            w_ca_v[0].reshape(dm, ca_heads * ca_hd).astype(BF16))
    ctab_p, stab_p = _rope_tabs_range(seq, rope_dim, nope_dim)
    x2d = x_prompt.reshape(batch * seq, dm)
    q, k, vt, ckv_p, kr_p, mc_p, sg_p, conv_p = _pre_prompt(dims, scale * LOG2_E, x2d, ctab_p, stab_p, w, batch,
                                                           seq, rope_dim, v_dim, dvp)
    o_p = _prompt_attention(q, k, vt, batch, seq, n_heads, v_dim)
    y_p = _post_prompt(x2d, o_p, mc_p, sg_p, mk_p, mv_p, w, batch, seq, ca_heads, n_mem)

    ctab_s, stab_s = _rope_tabs(jnp.full((n_s,), past_len, jnp.int32), rope_dim, nope_dim)
    xs = x_sample.reshape(n_s, dm)
    q_s, qlat_s, ckv_s, kr_s, mc_s, sg_s, conv_s = _pre_sample(
        dims, scale, xs, state_conv.reshape(n_s, (CONV_WIDTH - 1) * conv_dim), ctab_s, stab_s, w)
    olat = _decode_attention(page_table, qlat_s.reshape(n_s * n_heads, kv_rank),
                             q_s.reshape(n_s * n_heads, LANES), ckv_s, kr_s,
                             cache_ckv.reshape(n_pool, page, kv_rank),
                             jnp.swapaxes(cache_krope.reshape(n_pool, page, rope_dim), 1, 2), n_heads, rope_dim)
    y_s = _post_sample(xs, olat, mc_s, sg_s, _mem_rows(cache_mem_k), _mem_rows(cache_mem_v), w, n_heads, v_dim,
                       ca_heads)

    return (y_p.reshape(batch, seq, dm),
            y_s.reshape(n_s, t_s, dm),
            ckv_p.reshape(depth, batch, seq, kv_rank),
            kr_p.reshape(depth, batch, seq, rope_dim),
            conv_p.reshape(depth, batch, CONV_WIDTH - 1, conv_dim),
            mk_p.reshape(depth, batch, n_mem, ca_heads, ca_hd),
            mv_p.reshape(depth, batch, n_mem, ca_heads, ca_hd),
            ckv_s.reshape(depth, n_s, t_s, kv_rank),
            kr_s[:, :rope_dim].reshape(depth, n_s, t_s, rope_dim),
            conv_s.reshape(depth, n_s, CONV_WIDTH - 1, conv_dim))
```

```python
import functools

import jax
import jax.numpy as jnp
from jax import lax
from jax.experimental import pallas as pl
from jax.experimental.pallas import tpu as pltpu

F32 = jnp.float32
BF16 = jnp.bfloat16

RMS_EPS = 1e-6
NEG_INF = -1e30
ROPE_BASE = 10000.0
CONV_WIDTH = 3
LOG2_E = 1.4426950408889634

LANES = 128
SUBLANES = 8
MXU_WIDTH = 256
VMEM_LIMIT_BYTES = 56 * 1024 * 1024

ROW_TILE = 512
ATTN_TQ = 1024
ATTN_TK = 512
ATTN_V_EXTRA_ROWS = 16
DEC_PAGES_PER_STEP = 64
DEC_NBUF = 4
CA_SAMPLES_PER_STEP = 8


def _rms(x, g):
    return x * lax.rsqrt(jnp.mean(x * x, axis=-1, keepdims=True) + RMS_EPS) * g


def _bdot(a, b):
    return jnp.dot(a.astype(BF16), b.astype(BF16), preferred_element_type=F32)


def _bdot_nt(a, b):
    return lax.dot_general(a.astype(BF16), b.astype(BF16), (((1,), (1,)), ((), ())),
                           preferred_element_type=F32)


def _softmax_rows(s):
    m = jnp.max(s, axis=-1, keepdims=True)
    p = jnp.exp(s - m)
    return p / jnp.sum(p, axis=-1, keepdims=True)


def _const_spec(shape):
    nd = len(shape)
    return pl.BlockSpec(shape, lambda *_: (0,) * nd, pipeline_mode=pl.Buffered(1))


def _params(*sem):
    return pltpu.CompilerParams(dimension_semantics=tuple(sem) if sem else None,
                                vmem_limit_bytes=VMEM_LIMIT_BYTES)


def _memkv_kernel(mem_ref, g_ref, wk_ref, wv_ref, k_ref, v_ref):
    mn = _rms(mem_ref[...], g_ref[...]).astype(BF16)
    k_ref[...] = jnp.dot(mn, wk_ref[...], preferred_element_type=F32)
    v_ref[...] = jnp.dot(mn, wv_ref[...], preferred_element_type=F32)


def _memory_kv(mem2d, g, wk, wv):
    rows, d = mem2d.shape
    out = jax.ShapeDtypeStruct((rows, wk.shape[1]), F32)
    return pl.pallas_call(_memkv_kernel, out_shape=(out, out), name="mem_kv",
                          compiler_params=_params())(mem2d, g, wk, wv)


class _PreDims:
    def __init__(self, d_model, conv_dim, q_rank, kv_rank, n_heads):
        self.d_model, self.conv_dim, self.q_rank, self.kv_rank = d_model, conv_dim, q_rank, kv_rank
        self.n_heads = n_heads
        c = conv_dim
        self.o_h, self.o_gb, self.o_gc = 0, c, 2 * c
        self.o_cq = 3 * c
        self.o_ckv = self.o_cq + q_rank
        self.n_a = self.o_ckv + kv_rank
        self.hw = n_heads * LANES


def _pre_common(dims, x, ctab, stab, gpre_ref, wa_ref, wkr_ref, wg_ref, qg_ref, wuqa_ref, wuqb_ref, kvg_ref,
                scale):
    d = dims
    xn = _rms(x, gpre_ref[...]).astype(BF16)

    def proj(w_ref, lo, hi):
        return jnp.dot(xn, w_ref[:, lo:hi], preferred_element_type=F32)

    h = proj(wa_ref, d.o_h, d.o_gb)
    gate_b = proj(wa_ref, d.o_gb, d.o_gc)
    gate_c = proj(wa_ref, d.o_gc, d.o_cq)
    cq = proj(wa_ref, d.o_cq, d.o_ckv)
    ckv = proj(wa_ref, d.o_ckv, d.n_a)
    kr2 = proj(wkr_ref, 0, 2 * LANES)
    kra, krb = kr2[:, :LANES], kr2[:, LANES:]
    g_conv = proj(wg_ref, 0, d.d_model)
    g_mla = proj(wg_ref, d.d_model, 2 * d.d_model)

    u = gate_c * h
    cqn = _rms(cq, qg_ref[...]).astype(BF16)
    qa = jnp.dot(cqn, wuqa_ref[...], preferred_element_type=F32)
    qbc = jnp.dot(cqn, wuqb_ref[...], preferred_element_type=F32)
    r = wuqb_ref.shape[1] // d.n_heads
    pieces = []
    for hd in range(d.n_heads):
        blk, off = divmod(hd * r, LANES)
        piece = qbc[:, blk * LANES:(blk + 1) * LANES]
        pieces.append(pltpu.roll(piece, LANES - off, 1) if off else piece)
    qb = jnp.concatenate(pieces, axis=1)
    ct = jnp.concatenate([ctab] * d.n_heads, axis=1)
    st = jnp.concatenate([stab] * d.n_heads, axis=1)
    q = (qa * ct + qb * st) * scale
    ckvn = _rms(ckv, kvg_ref[...])
    krr = kra * ctab + krb * stab
    return u, gate_b, g_conv, g_mla, q, ckvn, krr


def _pre_prompt_kernel(dims, scale, tm, dv, dvp,
                       x_ref, ctab_ref, stab_ref, gpre_ref, wa_ref, wkr_ref, wg_ref, convw_ref, wco_ref, qg_ref,
                       wuqa_ref, wuqb_ref, kvg_ref, wuk_ref, wuvt_ref,
                       q_out, k_out, vt_out, ckv_out, kr_out, mc_out, sg_out, conv_out, ubuf):
    hist = SUBLANES

    @pl.when(pl.program_id(1) == 0)
    def _():
        ubuf[0:hist, :] = jnp.zeros((hist, dims.conv_dim), F32)

    u, gate_b, g_conv, g_mla, q, ckvn, krr = _pre_common(
        dims, x_ref[...], ctab_ref[...], stab_ref[...], gpre_ref, wa_ref, wkr_ref, wg_ref, qg_ref, wuqa_ref,
        wuqb_ref, kvg_ref, scale)

    ubuf[hist:hist + tm, :] = u
    u1 = ubuf[hist - 1:hist - 1 + tm, :]
    u2 = ubuf[hist - 2:hist - 2 + tm, :]
    cw = convw_ref[...]
    conv = u2 * cw[0:1, :] + u1 * cw[1:2, :] + u * cw[2:3, :]
    y_conv = _bdot(gate_b * conv, wco_ref[...])
    mc_out[...] = (jax.nn.sigmoid(g_conv) * y_conv).astype(mc_out.dtype)
    sg_out[...] = jax.nn.sigmoid(g_mla).astype(sg_out.dtype)
    conv_out[...] = ubuf[hist + tm - (CONV_WIDTH - 1):hist + tm, :]
    ubuf[0:hist, :] = ubuf[tm:tm + hist, :]

    q_out[...] = q.astype(q_out.dtype)
    ckv_out[...] = ckvn
    kr_out[...] = krr[:, :kr_out.shape[-1]]
    ckvb = ckvn.astype(BF16)
    ka = jnp.dot(ckvb, wuk_ref[...], preferred_element_type=F32)
    k = ka + jnp.concatenate([krr] * dims.n_heads, axis=1)
    k_out[...] = k.astype(k_out.dtype)
    vt = lax.dot_general(wuvt_ref[...], ckvb, (((1,), (1,)), ((), ())), preferred_element_type=F32)
    rid = lax.broadcasted_iota(jnp.int32, (vt.shape[0], 1), 0)
    vt_out[...] = (vt + jnp.where(rid % dvp == dv, 1.0, 0.0)).astype(vt_out.dtype)


def _pre_prompt(dims, scale, x2d, ctab, stab, w, batch, seq, rope_dim, dv, dvp):
    tm = ROW_TILE
    assert seq % tm == 0
    nt = seq // tm
    rows = batch * seq
    d = dims
    hv = w["w_uv_t"].shape[0]

    def row_spec(width):
        return pl.BlockSpec((tm, width), lambda b, t: (b * nt + t, 0))

    def tab_spec():
        return pl.BlockSpec((tm, LANES), lambda b, t: (t, 0))

    in_specs = [row_spec(d.d_model), tab_spec(), tab_spec(),
                _const_spec((1, d.d_model)), _const_spec((d.d_model, d.n_a)),
                _const_spec((d.d_model, 2 * LANES)), _const_spec((d.d_model, 2 * d.d_model)),
                _const_spec((CONV_WIDTH, d.conv_dim)), _const_spec((d.conv_dim, d.d_model)),
                _const_spec((1, d.q_rank)), _const_spec((d.q_rank, d.hw)),
                _const_spec((d.q_rank, d.n_heads * rope_dim)),
                _const_spec((1, d.kv_rank)), _const_spec((d.kv_rank, d.hw)), _const_spec((hv, d.kv_rank))]
    out_shape = (jax.ShapeDtypeStruct((rows, d.hw), BF16),
                 jax.ShapeDtypeStruct((rows, d.hw), BF16),
                 jax.ShapeDtypeStruct((batch, hv, seq), BF16),
                 jax.ShapeDtypeStruct((rows, d.kv_rank), F32),
                 jax.ShapeDtypeStruct((rows, rope_dim), F32),
                 jax.ShapeDtypeStruct((rows, d.d_model), BF16),
                 jax.ShapeDtypeStruct((rows, d.d_model), BF16),
                 jax.ShapeDtypeStruct((batch, CONV_WIDTH - 1, d.conv_dim), F32))
    out_specs = (row_spec(d.hw), row_spec(d.hw),
                 pl.BlockSpec((None, hv, tm), lambda b, t: (b, 0, t)), row_spec(d.kv_rank),
                 row_spec(rope_dim), row_spec(d.d_model), row_spec(d.d_model),
                 pl.BlockSpec((None, CONV_WIDTH - 1, d.conv_dim), lambda b, t: (b, 0, 0)))
    kern = functools.partial(_pre_prompt_kernel, dims, scale, tm, dv, dvp)
    return pl.pallas_call(
        kern, grid=(batch, nt), in_specs=in_specs, out_specs=out_specs, out_shape=out_shape,
        scratch_shapes=[pltpu.VMEM((tm + SUBLANES, d.conv_dim), F32)],
        compiler_params=_params("arbitrary", "arbitrary"), name="pre_prompt",
    )(x2d, ctab, stab, w["g_pre"], w["w_in_a"], w["w_in_kr"], w["w_in_g"], w["conv_w"], w["w_conv_out"], w["q_g"],
      w["w_uq_a"], w["w_uq_b"], w["kv_g"], w["w_uk_pad"], w["w_uv_t"])


def _pre_sample_kernel(dims, scale,
                       x_ref, state_ref, ctab_ref, stab_ref, gpre_ref, wa_ref, wkr_ref, wg_ref, convw_ref, wco_ref,
                       qg_ref, wuqa_ref, wuqb_ref, kvg_ref, wukt_ref,
                       q_out, qlat_out, ckv_out, kr_out, mc_out, sg_out, conv_out):
    c = dims.conv_dim
    u, gate_b, g_conv, g_mla, q, ckvn, krr = _pre_common(
        dims, x_ref[...], ctab_ref[...], stab_ref[...], gpre_ref, wa_ref, wkr_ref, wg_ref, qg_ref, wuqa_ref,
        wuqb_ref, kvg_ref, scale)
    u2 = state_ref[:, 0:c]
    u1 = state_ref[:, c:2 * c]
    cw = convw_ref[...]
    conv = u2 * cw[0:1, :] + u1 * cw[1:2, :] + u * cw[2:3, :]
    y_conv = _bdot(gate_b * conv, wco_ref[...])
    mc_out[...] = jax.nn.sigmoid(g_conv) * y_conv
    sg_out[...] = jax.nn.sigmoid(g_mla)
    conv_out[:, 0:c] = u1
    conv_out[:, c:2 * c] = u
    q_out[...] = q
    ckv_out[...] = ckvn
    kr_out[...] = krr
    kvr = dims.kv_rank
    qb16 = q.astype(BF16)
    for hd in range(dims.n_heads):
        blk = qb16[:, hd * LANES:(hd + 1) * LANES]
        qlat_out[:, hd * kvr:(hd + 1) * kvr] = jnp.dot(
            blk, wukt_ref[hd * LANES:(hd + 1) * LANES, :], preferred_element_type=F32)


def _pre_sample(dims, scale, xs, state2d, ctab, stab, w):
    n = xs.shape[0]
    d = dims
    out_shape = (jax.ShapeDtypeStruct((n, d.hw), F32),
                 jax.ShapeDtypeStruct((n, d.n_heads * d.kv_rank), F32),
                 jax.ShapeDtypeStruct((n, d.kv_rank), F32),
                 jax.ShapeDtypeStruct((n, LANES), F32),
                 jax.ShapeDtypeStruct((n, d.d_model), F32),
                 jax.ShapeDtypeStruct((n, d.d_model), F32),
                 jax.ShapeDtypeStruct((n, (CONV_WIDTH - 1) * d.conv_dim), F32))
    kern = functools.partial(_pre_sample_kernel, dims, scale)
    return pl.pallas_call(kern, out_shape=out_shape, compiler_params=_params(), name="pre_sample")(
        xs, state2d, ctab, stab, w["g_pre"], w["w_in_a"], w["w_in_kr"], w["w_in_g"], w["conv_w"], w["w_conv_out"],
        w["q_g"],
        w["w_uq_a"], w["w_uq_b"], w["kv_g"], w["w_uk_t_pad"])


def _attn_kernel(tq, tk, hp, dv, dvp, nq, q_ref, k_ref, vt_ref, o_ref, sa_ref, sb_ref):
    assert tq == 2 * tk
    half = tq // 2

    def tile(qi, carry):
        q0 = pl.multiple_of(qi * tq, tq)

        def score_group(h, j, dst, n, q_lo, qbase):
            kh = k_ref[pl.ds(pl.multiple_of(j * tk, tk), tk), h * LANES:(h + 1) * LANES]
            qh = q_ref[pl.ds(pl.multiple_of(qbase + q_lo + n * MXU_WIDTH, MXU_WIDTH), MXU_WIDTH),
                       h * LANES:(h + 1) * LANES]
            dst[h, :, n * MXU_WIDTH:(n + 1) * MXU_WIDTH] = lax.dot_general(
                kh, qh, (((1,), (1,)), ((), ())), preferred_element_type=F32)

        def step(state, proc=None, sc=None):
            n_s = (tq - sc[2]) // MXU_WIDTH if sc else 0
            if proc is None:
                for h in range(hp):
                    for n in range(n_s):
                        score_group(h, sc[0], sc[1], n, sc[2], sc[3])
                return state
            j, src, masked, q_lo = proc
            start = pl.multiple_of(j * tk, tk)
            nq_p = tq - q_lo
            n_p = nq_p // MXU_WIDTH
            new_state = []
            for h in range(hp):
                m_prev, acc = state[h]
                st = src[h, :, 0:nq_p]
                if masked:
                    key = lax.broadcasted_iota(jnp.int32, (tk, nq_p), 0) + j * tk
                    qry = lax.broadcasted_iota(jnp.int32, (tk, nq_p), 1) + qi * tq + q_lo
                    st = jnp.where(key <= qry, st, NEG_INF)
                m_new = jnp.maximum(m_prev[:, q_lo:], jnp.max(st, axis=0, keepdims=True))
                alpha = jnp.exp2(m_prev[:, q_lo:] - m_new)
                p = jnp.exp2(st - m_new).astype(BF16)
                vth = vt_ref[h * dvp:(h + 1) * dvp, pl.ds(start, tk)]
                pieces = [acc[:, :q_lo]] if q_lo else []
                for n in range(max(n_p, n_s)):
                    if n < n_s:
                        score_group(h, sc[0], sc[1], n, sc[2], sc[3])
                    if n < n_p:
                        ql = slice(n * MXU_WIDTH, (n + 1) * MXU_WIDTH)
                        pieces.append(alpha[:, ql] * acc[:, q_lo + n * MXU_WIDTH:q_lo + (n + 1) * MXU_WIDTH]
                                      + jnp.dot(vth, p[:, ql], preferred_element_type=F32))
                if q_lo:
                    m_new = jnp.concatenate([m_prev[:, :q_lo], m_new], axis=1)
                new_state.append((m_new, jnp.concatenate(pieces, axis=1)))
            return tuple(new_state)

        init = tuple((jnp.full((1, tq), NEG_INF, F32), jnp.zeros((dvp, tq), F32)) for _ in range(hp))

        @pl.when(qi == 0)
        def _():
            step(None, sc=(0, sa_ref, 0, q0))

        def pair(t, state):
            state = step(state, proc=(2 * t, sa_ref, False, 0), sc=(2 * t + 1, sb_ref, 0, q0))
            return step(state, proc=(2 * t + 1, sb_ref, False, 0), sc=(2 * t + 2, sa_ref, 0, q0))

        state = lax.fori_loop(0, qi, pair, init)
        q_next = pl.multiple_of(jnp.minimum(qi + 1, nq - 1) * tq, tq)
        state = step(state, proc=(2 * qi, sa_ref, True, 0), sc=(2 * qi + 1, sb_ref, half, q0))
        state = step(state, proc=(2 * qi + 1, sb_ref, True, half), sc=(0, sa_ref, 0, q_next))
        ot = jnp.concatenate([acc[0:dv] / acc[dv:dv + 1] for (_, acc) in state], axis=0)
        o_ref[pl.ds(q0, tq), :] = ot.T.astype(o_ref.dtype)
        return carry

    lax.fori_loop(0, nq, tile, 0)


def _prompt_attention(q, k, vt, batch, seq, n_heads, dv):
    tq, tk = ATTN_TQ, ATTN_TK
    assert seq % tq == 0 and tq % tk == 0
    hp = LANES // dv
    assert n_heads % hp == 0
    dvp = vt.shape[1] // n_heads
    nq = seq // tq
    kern = functools.partial(_attn_kernel, tq, tk, hp, dv, dvp, nq)
    seq_spec = pl.BlockSpec((seq, hp * LANES), lambda b, g: (b, g))
    return pl.pallas_call(
        kern, grid=(batch, n_heads // hp),
        in_specs=[seq_spec, seq_spec, pl.BlockSpec((None, hp * dvp, seq), lambda b, g: (b, g, 0))],
        out_specs=pl.BlockSpec((seq, hp * dv), lambda b, g: (b, g)),
        out_shape=jax.ShapeDtypeStruct((batch * seq, n_heads * dv), BF16),
        scratch_shapes=[pltpu.VMEM((hp, tk, tq), F32), pltpu.VMEM((hp, tk, tq), F32)],
        compiler_params=_params("arbitrary", "arbitrary"), name="prompt_attn",
    )(q, k, vt)


def _mix_and_query(x, o, mc, sg, wmla_ref, wmix_ref, gmixpost_ref, gcapre_ref, wcaq_ref, ca_scale):
    y_mla = _bdot(o, wmla_ref[...])
    merged = mc.astype(F32) + sg.astype(F32) * y_mla
    y = _bdot(merged, wmix_ref[...])
    x1 = x + _rms(y, gmixpost_ref[...])
    qc = _bdot(_rms(x1, gcapre_ref[...]), wcaq_ref[...]) * ca_scale
    return x1, qc


def _mlp_tail(x1, oc, wcao_ref, gcapost_ref, gmlppre_ref, wup_ref, wdown_ref, gmlppost_ref):
    ca = _bdot(oc, wcao_ref[...])
    x2 = x1 + _rms(ca, gcapost_ref[...])
    hid = jnp.square(jnp.maximum(_bdot(_rms(x2, gmlppre_ref[...]), wup_ref[...]), 0.0))
    return x2 + _rms(_bdot(hid, wdown_ref[...]), gmlppost_ref[...])


def _post_prompt_kernel(ca_heads, ca_scale,
                        x_ref, o_ref, mc_ref, sg_ref, mk_ref, mv_ref,
                        wmla_ref, wmix_ref, gmixpost_ref, gcapre_ref, wcaq_ref, wcao_ref, gcapost_ref,
                        gmlppre_ref, wup_ref, wdown_ref, gmlppost_ref, y_ref):
    x1, qc = _mix_and_query(x_ref[...], o_ref[...], mc_ref[...], sg_ref[...], wmla_ref, wmix_ref,
                            gmixpost_ref, gcapre_ref, wcaq_ref, ca_scale)
    hd = qc.shape[1] // ca_heads
    outs = []
    for h in range(ca_heads):
        sl = slice(h * hd, (h + 1) * hd)
        p = _softmax_rows(_bdot_nt(qc[:, sl], mk_ref[:, sl]))
        outs.append(_bdot(p, mv_ref[:, sl]))
    oc = jnp.concatenate(outs, axis=1)
    y_ref[...] = _mlp_tail(x1, oc, wcao_ref, gcapost_ref, gmlppre_ref, wup_ref, wdown_ref, gmlppost_ref)


def _post_prompt(x2d, o, mc, sg, mk, mv, w, batch, seq, ca_heads, n_mem):
    tm = ROW_TILE
    nt = seq // tm
    rows, dm = x2d.shape
    dff = w["w_ff_up"].shape[1]
    ca_scale = float((dm // ca_heads) ** -0.5)

    def row_spec(width):
        return pl.BlockSpec((tm, width), lambda i: (i, 0))

    mem_spec = pl.BlockSpec((n_mem, dm), lambda i: (i // nt, 0))
    in_specs = [row_spec(dm), row_spec(o.shape[1]), row_spec(dm), row_spec(dm), mem_spec, mem_spec,
                _const_spec((o.shape[1], dm)), _const_spec((dm, dm)), _const_spec((1, dm)),
                _const_spec((1, dm)), _const_spec((dm, dm)), _const_spec((dm, dm)), _const_spec((1, dm)),
                _const_spec((1, dm)), _const_spec((dm, dff)), _const_spec((dff, dm)), _const_spec((1, dm))]
    kern = functools.partial(_post_prompt_kernel, ca_heads, ca_scale)
    return pl.pallas_call(
        kern, grid=(rows // tm,), in_specs=in_specs, out_specs=row_spec(dm),
        out_shape=jax.ShapeDtypeStruct((rows, dm), F32),
        compiler_params=_params("arbitrary"), name="post_prompt",
    )(x2d, o, mc, sg, mk, mv, w["w_mla"], w["w_mix"], w["g_mix_post"], w["g_ca_pre"], w["w_ca_q"],
      w["w_ca_o"], w["g_ca_post"], w["g_mlp_pre"], w["w_ff_up"], w["w_ff_down"], w["g_mlp_post"])


def _post_sample_a_kernel(n_heads, v_dim, ca_scale,
                          x_ref, olat_ref, mc_ref, sg_ref, wuv_ref, wmla_ref, wmix_ref, gmixpost_ref,
                          gcapre_ref, wcaq_ref, x1_ref, qc_ref):
    n = x_ref.shape[0]
    full = _bdot(olat_ref[...], wuv_ref[...])
    full = full.reshape(n, n_heads, n_heads * v_dim)
    hidx = lax.broadcasted_iota(jnp.int32, full.shape, 1)
    lane_head = lax.broadcasted_iota(jnp.int32, full.shape, 2) // v_dim
    o = jnp.sum(jnp.where(hidx == lane_head, full, 0.0), axis=1)
    x1, qc = _mix_and_query(x_ref[...], o, mc_ref[...], sg_ref[...], wmla_ref, wmix_ref,
                            gmixpost_ref, gcapre_ref, wcaq_ref, ca_scale)
    x1_ref[...] = x1
    qc_ref[...] = qc


def _post_sample_b_kernel(ca_heads, q_ref, mk_ref, mv_ref, o_ref):
    g = q_ref.shape[0]
    rows = mk_ref.shape[1]
    s8 = 2 * ca_heads
    assert s8 == SUBLANES and mk_ref.shape[2] == LANES
    lane = lax.broadcasted_iota(jnp.int32, (s8, rows), 1)
    sub = lax.broadcasted_iota(jnp.int32, (s8, rows), 0)
    own = (lane % s8) == sub
    low_half = (lax.broadcasted_iota(jnp.int32, (1, rows), 1) % s8) < ca_heads
    for i in range(g):
        q = q_ref[i]
        q8 = jnp.concatenate([q[:, (2 * (j % ca_heads) + j // ca_heads) * LANES:
                                   (2 * (j % ca_heads) + j // ca_heads + 1) * LANES] for j in range(s8)], axis=0)
        kb = mk_ref[i].astype(BF16)
        p_all = lax.dot_general(q8.astype(BF16), kb, (((1,), (1,)), ((), ())), preferred_element_type=F32)
        d = jnp.sum(jnp.where(own, p_all, 0.0), axis=0, keepdims=True)
        d = d + jnp.where(low_half, pltpu.roll(d, rows - ca_heads, 1), pltpu.roll(d, ca_heads, 1))
        dm = jnp.where(own, jnp.broadcast_to(d, (s8, rows)), NEG_INF)
        e = jnp.exp(dm - jnp.max(dm, axis=-1, keepdims=True))
        w = (e / jnp.sum(e, axis=-1, keepdims=True)).astype(BF16)
        o8 = jnp.dot(w, mv_ref[i].astype(BF16), preferred_element_type=F32)
        for j in range(s8):
            blk = 2 * (j % ca_heads) + j // ca_heads
            o_ref[i, :, blk * LANES:(blk + 1) * LANES] = o8[j:j + 1, :]


def _post_sample_c_kernel(x1_ref, oc_ref, wcao_ref, gcapost_ref, gmlppre_ref, wup_ref, wdown_ref,
                          gmlppost_ref, y_ref):
    y_ref[...] = _mlp_tail(x1_ref[...], oc_ref[...], wcao_ref, gcapost_ref, gmlppre_ref, wup_ref,
                           wdown_ref, gmlppost_ref)


def _post_sample(xs, olat, mc, sg, mem_k, mem_v, w, n_heads, v_dim, ca_heads):
    n, dm = xs.shape
    ca_scale = float((dm // ca_heads) ** -0.5)
    kern_a = functools.partial(_post_sample_a_kernel, n_heads, v_dim, ca_scale)
    x1, qc = pl.pallas_call(
        kern_a, out_shape=(jax.ShapeDtypeStruct((n, dm), F32), jax.ShapeDtypeStruct((n, dm), F32)),
        compiler_params=_params(), name="post_sample_a",
    )(xs, olat, mc, sg, w["w_uv_flat"], w["w_mla"], w["w_mix"], w["g_mix_post"], w["g_ca_pre"], w["w_ca_q"])

    g = CA_SAMPLES_PER_STEP
    assert n % g == 0
    kern_b = functools.partial(_post_sample_b_kernel, ca_heads)
    mem_spec = pl.BlockSpec((g,) + mem_k.shape[1:], lambda i: (i, 0, 0))
    oc = pl.pallas_call(
        kern_b, grid=(n // g,),
        in_specs=[pl.BlockSpec((g, 1, dm), lambda i: (i, 0, 0)), mem_spec, mem_spec],
        out_specs=pl.BlockSpec((g, 1, dm), lambda i: (i, 0, 0)),
        out_shape=jax.ShapeDtypeStruct((n, 1, dm), F32),
        compiler_params=_params("arbitrary"), name="post_sample_b",
    )(qc.reshape(n, 1, dm), mem_k, mem_v)

    return pl.pallas_call(
        _post_sample_c_kernel, out_shape=jax.ShapeDtypeStruct((n, dm), F32),
        compiler_params=_params(), name="post_sample_c",
    )(x1, oc.reshape(n, dm), w["w_ca_o"], w["g_ca_post"], w["g_mlp_pre"], w["w_ff_up"], w["w_ff_down"],
      w["g_mlp_post"])


def _dec_attn_kernel(n_samples, n_pages, n_heads, rope_dim,
                     pt_ref, qlat_ref, qrope_ref, ckvn_ref, krn_ref, cache_ckv, cache_krt,
                     o_ref, cbuf, kbuf, sems):
    ch = DEC_PAGES_PER_STEP
    nch = n_pages // ch
    total = n_samples * nch
    page, kvr = cbuf.shape[2], cbuf.shape[3]

    def copies(g, slot):
        b = g // nch
        c = g % nch
        out = []
        for i in range(ch):
            pg = pt_ref[b, c * ch + i]
            out.append(pltpu.make_async_copy(cache_ckv.at[pg], cbuf.at[slot, i], sems.at[0, slot]))
            out.append(pltpu.make_async_copy(cache_krt.at[pg], kbuf.at[slot, i], sems.at[1, slot]))
        return out

    def start(g):
        for n, cp in enumerate(copies(g, g % DEC_NBUF)):
            cp.start(priority=(n // 2) % 2)

    def wait(g):
        for cp in copies(g, g % DEC_NBUF):
            cp.wait()

    def q_rows(b):
        row0 = pl.multiple_of(b * n_heads, n_heads)
        return qlat_ref[pl.ds(row0, n_heads), :], qrope_ref[pl.ds(row0, n_heads), :][:, :rope_dim]

    def scores(g):
        ql, qr = q_rows(g // nch)
        slot = g % DEC_NBUF
        ckv = cbuf[slot].reshape(ch * page, kvr).astype(BF16)
        qrb = qr.astype(BF16)
        s_rope = jnp.concatenate(
            [jnp.dot(qrb, kbuf[slot, i].astype(BF16), preferred_element_type=F32) for i in range(ch)], axis=1)
        return lax.dot_general(ql.astype(BF16), ckv, (((1,), (1,)), ((), ())),
                               preferred_element_type=F32) + s_rope

    for g0 in range(DEC_NBUF - 1):
        start(g0)
    wait(0)
    s0 = scores(0)

    def body(g, carry):
        s_cur, m, l, acc = carry
        b = g // nch
        c = g % nch

        @pl.when(g + 1 < total)
        def _():
            wait(g + 1)

        @pl.when(g + (DEC_NBUF - 1) < total)
        def _():
            start(g + (DEC_NBUF - 1))

        s_next = scores(jnp.minimum(g + 1, total - 1))

        ql, qr = q_rows(b)
        c_new = ckvn_ref[pl.ds(b, 1), :]
        r_new = krn_ref[pl.ds(b, 1), :][:, :rope_dim]
        s_new = jnp.sum(ql * c_new, axis=-1, keepdims=True) + jnp.sum(qr * r_new, axis=-1, keepdims=True)
        first = c == 0
        m = jnp.where(first, s_new, m)
        l = jnp.where(first, 1.0, l)
        acc = jnp.where(first, jnp.broadcast_to(c_new, acc.shape), acc)

        ckv = cbuf[g % DEC_NBUF].reshape(ch * page, kvr).astype(BF16)
        m_new = jnp.maximum(m, jnp.max(s_cur, axis=-1, keepdims=True))
        alpha = jnp.exp(m - m_new)
        p = jnp.exp(s_cur - m_new)
        l = alpha * l + jnp.sum(p, axis=-1, keepdims=True)
        acc = alpha * acc + jnp.dot(p.astype(BF16), ckv, preferred_element_type=F32)
        o_ref[pl.ds(pl.multiple_of(b * n_heads, n_heads), n_heads), :] = acc / l
        return s_next, m_new, l, acc

    init = (s0, jnp.zeros((n_heads, 1), F32), jnp.zeros((n_heads, 1), F32), jnp.zeros((n_heads, kvr), F32))
    lax.fori_loop(0, total, body, init)


def _decode_attention(page_table, qlat, qrope, ckv_new, kr_new, cache_ckv, cache_krt, n_heads, rope_dim):
    n_samples, n_pages = page_table.shape
    page, kvr = cache_ckv.shape[1], cache_ckv.shape[2]
    ch = DEC_PAGES_PER_STEP
    assert n_pages % ch == 0

    def whole(shape):
        nd = len(shape)
        return pl.BlockSpec(shape, lambda i, pt: (0,) * nd)

    kern = functools.partial(_dec_attn_kernel, n_samples, n_pages, n_heads, rope_dim)
    grid_spec = pltpu.PrefetchScalarGridSpec(
        num_scalar_prefetch=1, grid=(1,),
        in_specs=[whole(qlat.shape), whole(qrope.shape), whole(ckv_new.shape), whole(kr_new.shape),
                  pl.BlockSpec(memory_space=pl.ANY), pl.BlockSpec(memory_space=pl.ANY)],
        out_specs=whole(qlat.shape),
        scratch_shapes=[pltpu.VMEM((DEC_NBUF, ch, page, kvr), F32),
                        pltpu.VMEM((DEC_NBUF, ch, rope_dim, page), F32),
                        pltpu.SemaphoreType.DMA((2, DEC_NBUF))])
    return pl.pallas_call(
        kern, grid_spec=grid_spec, out_shape=jax.ShapeDtypeStruct(qlat.shape, F32),
        compiler_params=_params("arbitrary"), name="decode_attn",
    )(page_table, qlat, qrope, ckv_new, kr_new, cache_ckv, cache_krt)


def _rot_half(w):
    half = w.shape[-1] // 2
    return jnp.concatenate([-w[..., half:], w[..., :half]], axis=-1)


def _mem_rows(cache):
    _, n, n_mem, heads, hd = cache.shape
    assert hd == 2 * LANES
    x = cache.reshape(n, n_mem, heads, 2, LANES)
    return jnp.transpose(x, (0, 1, 3, 2, 4)).reshape(n, n_mem * 2 * heads, LANES)


def _rope_lane_freqs(rope_dim):
    inv = 1.0 / (ROPE_BASE ** (jnp.arange(0, rope_dim, 2, dtype=F32) / rope_dim))
    return jnp.concatenate([inv, inv, jnp.zeros((LANES - rope_dim,), F32)])


def _rope_tabs(pos, rope_dim, nope_dim):
    ang = pos.astype(F32)[:, None] * _rope_lane_freqs(rope_dim)[None, :]
    keep = (jnp.arange(LANES) < rope_dim + nope_dim).astype(F32)
    return jnp.cos(ang) * keep, jnp.sin(ang)


def _rope_tabs_range(seq, rope_dim, nope_dim):
    assert seq % LANES == 0
    freqs = _rope_lane_freqs(rope_dim)
    a = (jnp.arange(seq // LANES, dtype=F32) * LANES)[:, None, None] * freqs
    b = jnp.arange(LANES, dtype=F32)[None, :, None] * freqs
    ca, sa, cb, sb = jnp.cos(a), jnp.sin(a), jnp.cos(b), jnp.sin(b)
    keep = (jnp.arange(LANES) < rope_dim + nope_dim).astype(F32)
    ctab = (ca * cb - sa * sb) * keep
    stab = sa * cb + ca * sb
    return ctab.reshape(seq, LANES), stab.reshape(seq, LANES)


def kernel(x_prompt, x_sample, mem_prompt, cache_ckv, cache_krope, state_conv, cache_mem_k, cache_mem_v,
           page_table, norm_mix_pre_g, w_in, conv_w, w_conv_out, q_norm_g, w_uq, kv_norm_g, w_uk, w_uv,
           w_mla_out, w_mix_out, norm_mix_post_g, norm_ca_pre_g, mem_norm_g, w_ca_q, w_ca_k, w_ca_v, w_ca_o,
           norm_ca_post_g, norm_mlp_pre_g, w_ff_up, w_ff_down, norm_mlp_post_g):
    depth = w_in.shape[0]
    assert depth == 1, "single-layer step"
    batch, seq, dm = x_prompt.shape
    n_s, t_s, _ = x_sample.shape
    assert t_s == 1
    conv_dim = conv_w.shape[2]
    q_rank, n_heads, qk_dim = w_uq.shape[1:]
    kv_rank, _, nope_dim = w_uk.shape[1:]
    v_dim = w_uv.shape[3]
    rope_dim = qk_dim - nope_dim
    n_mem, ca_heads, ca_hd = cache_mem_k.shape[2:]
    n_pool, page = cache_ckv.shape[1:3]
    past_len = page_table.shape[1] * page
    assert rope_dim + nope_dim <= LANES and v_dim <= LANES
    dims = _PreDims(dm, conv_dim, q_rank, kv_rank, n_heads)
    scale = float(qk_dim ** -0.5)

    w0 = w_in[0]
    o_kr = 3 * conv_dim + q_rank + kv_rank
    w_kr = w0[:, o_kr:o_kr + rope_dim]
    zpad = jnp.zeros((dm, LANES - rope_dim), F32)
    w_in_kr = jnp.concatenate([w_kr, zpad, _rot_half(w_kr), zpad], axis=1).astype(BF16)
    uq = w_uq[0]
    uq_nope, uq_rope = uq[:, :, :nope_dim], uq[:, :, nope_dim:]
    hz = lambda r, width: jnp.zeros((r, n_heads, width), F32)
    pad_tail = LANES - rope_dim - nope_dim
    w_uq_a = jnp.concatenate([uq_rope, uq_nope, hz(q_rank, pad_tail)], axis=2).reshape(q_rank, n_heads * LANES)
    assert LANES % rope_dim == 0
    w_uq_b = _rot_half(uq_rope).reshape(q_rank, n_heads * rope_dim)
    uk = w_uk[0]
    w_uk_pad = jnp.concatenate([hz(kv_rank, rope_dim), uk, hz(kv_rank, pad_tail)], axis=2).reshape(
        kv_rank, n_heads * LANES)
    w_uk_t_pad = jnp.transpose(w_uk_pad.reshape(kv_rank, n_heads, LANES), (1, 2, 0)).reshape(
        n_heads * LANES, kv_rank)
    uv = w_uv[0].reshape(kv_rank, n_heads * v_dim)
    dvp = v_dim + ATTN_V_EXTRA_ROWS
    w_uv_t = jnp.concatenate([jnp.transpose(w_uv[0], (1, 2, 0)),
                              jnp.zeros((n_heads, ATTN_V_EXTRA_ROWS, kv_rank), F32)], axis=1).reshape(
        n_heads * dvp, kv_rank)
    w_mla = w_mla_out[0]
    w = {
        "g_pre": norm_mix_pre_g, "w_in_a": w0[:, :o_kr].astype(BF16), "w_in_kr": w_in_kr,
        "w_in_g": w0[:, o_kr + rope_dim:].astype(BF16), "conv_w": conv_w[0], "w_conv_out": w_conv_out[0].astype(BF16),
        "q_g": q_norm_g, "w_uq_a": w_uq_a.astype(BF16), "w_uq_b": w_uq_b.astype(BF16), "kv_g": kv_norm_g,
        "w_uk_pad": w_uk_pad.astype(BF16), "w_uk_t_pad": w_uk_t_pad.astype(BF16),
        "w_uv_t": w_uv_t.astype(BF16), "w_uv_flat": uv.astype(BF16),
        "w_mla": w_mla.astype(BF16), "w_mix": w_mix_out[0].astype(BF16),
        "g_mix_post": norm_mix_post_g, "g_ca_pre": norm_ca_pre_g,
        "w_ca_q": w_ca_q[0].reshape(dm, ca_heads * ca_hd).astype(BF16),
        "w_ca_o": w_ca_o[0].reshape(ca_heads * ca_hd, dm).astype(BF16),
        "g_ca_post": norm_ca_post_g, "g_mlp_pre": norm_mlp_pre_g, "w_ff_up": w_ff_up[0].astype(BF16),
        "w_ff_down": w_ff_down[0].astype(BF16), "g_mlp_post": norm_mlp_post_g,
    }

    mk_p, mv_p = _memory_kv(mem_prompt.reshape(batch * n_mem, dm), mem_norm_g,
                            w_ca_k[0].reshape(dm, ca_heads * ca_hd).astype(BF16),
                            w_ca_v[0].reshape(dm, ca_heads * ca_hd).astype(BF16))
    ctab_p, stab_p = _rope_tabs_range(seq, rope_dim, nope_dim)
    x2d = x_prompt.reshape(batch * seq, dm)
    q, k, vt, ckv_p, kr_p, mc_p, sg_p, conv_p = _pre_prompt(dims, scale * LOG2_E, x2d, ctab_p, stab_p, w, batch,
                                                           seq, rope_dim, v_dim, dvp)
    o_p = _prompt_attention(q, k, vt, batch, seq, n_heads, v_dim)
    y_p = _post_prompt(x2d, o_p, mc_p, sg_p, mk_p, mv_p, w, batch, seq, ca_heads, n_mem)

    ctab_s, stab_s = _rope_tabs(jnp.full((n_s,), past_len, jnp.int32), rope_dim, nope_dim)
    xs = x_sample.reshape(n_s, dm)
    q_s, qlat_s, ckv_s, kr_s, mc_s, sg_s, conv_s = _pre_sample(
        dims, scale, xs, state_conv.reshape(n_s, (CONV_WIDTH - 1) * conv_dim), ctab_s, stab_s, w)
    olat = _decode_attention(page_table, qlat_s.reshape(n_s * n_heads, kv_rank),
                             q_s.reshape(n_s * n_heads, LANES), ckv_s, kr_s,
                             cache_ckv.reshape(n_pool, page, kv_rank),
                             jnp.swapaxes(cache_krope.reshape(n_pool, page, rope_dim), 1, 2), n_heads, rope_dim)
    y_s = _post_sample(xs, olat, mc_s, sg_s, _mem_rows(cache_mem_k), _mem_rows(cache_mem_v), w, n_heads, v_dim,
                       ca_heads)

    return (y_p.reshape(batch, seq, dm),
            y_s.reshape(n_s, t_s, dm),
            ckv_p.reshape(depth, batch, seq, kv_rank),
            kr_p.reshape(depth, batch, seq, rope_dim),
            conv_p.reshape(depth, batch, CONV_WIDTH - 1, conv_dim),
            mk_p.reshape(depth, batch, n_mem, ca_heads, ca_hd),
            mv_p.reshape(depth, batch, n_mem, ca_heads, ca_hd),
            ckv_s.reshape(depth, n_s, t_s, kv_rank),
            kr_s[:, :rope_dim].reshape(depth, n_s, t_s, rope_dim),
            conv_s.reshape(depth, n_s, CONV_WIDTH - 1, conv_dim))
```

```python
import functools

import jax
import jax.numpy as jnp
from jax import lax
from jax.experimental import pallas as pl
from jax.experimental.pallas import tpu as pltpu

F32 = jnp.float32
BF16 = jnp.bfloat16

RMS_EPS = 1e-6
NEG_INF = -1e30
ROPE_BASE = 10000.0
CONV_WIDTH = 3
LOG2_E = 1.4426950408889634

LANES = 128
SUBLANES = 8
MXU_WIDTH = 256
VMEM_LIMIT_BYTES = 56 * 1024 * 1024

ROW_TILE = 512
ATTN_TQ = 1024
ATTN_TK = 512
ATTN_V_EXTRA_ROWS = 16
DEC_PAGES_PER_STEP = 32
DEC_NBUF = 5
CA_SAMPLES_PER_STEP = 8


def _rms(x, g):
    return x * lax.rsqrt(jnp.mean(x * x, axis=-1, keepdims=True) + RMS_EPS) * g


def _bdot(a, b):
    return jnp.dot(a.astype(BF16), b.astype(BF16), preferred_element_type=F32)


def _bdot_nt(a, b):
    return lax.dot_general(a.astype(BF16), b.astype(BF16), (((1,), (1,)), ((), ())),
                           preferred_element_type=F32)


def _softmax_rows(s):
    m = jnp.max(s, axis=-1, keepdims=True)
    p = jnp.exp(s - m)
    return p / jnp.sum(p, axis=-1, keepdims=True)


def _const_spec(shape):
    nd = len(shape)
    return pl.BlockSpec(shape, lambda *_: (0,) * nd, pipeline_mode=pl.Buffered(1))


def _params(*sem):
    return pltpu.CompilerParams(dimension_semantics=tuple(sem) if sem else None,
                                vmem_limit_bytes=VMEM_LIMIT_BYTES)


def _memkv_kernel(mem_ref, g_ref, wk_ref, wv_ref, k_ref, v_ref):
    mn = _rms(mem_ref[...], g_ref[...]).astype(BF16)
    k_ref[...] = jnp.dot(mn, wk_ref[...], preferred_element_type=F32)
    v_ref[...] = jnp.dot(mn, wv_ref[...], preferred_element_type=F32)


def _memory_kv(mem2d, g, wk, wv):
    rows, d = mem2d.shape
    out = jax.ShapeDtypeStruct((rows, wk.shape[1]), F32)
    return pl.pallas_call(_memkv_kernel, out_shape=(out, out), name="mem_kv",
                          compiler_params=_params())(mem2d, g, wk, wv)


class _PreDims:
    def __init__(self, d_model, conv_dim, q_rank, kv_rank, n_heads):
        self.d_model, self.conv_dim, self.q_rank, self.kv_rank = d_model, conv_dim, q_rank, kv_rank
        self.n_heads = n_heads
        c = conv_dim
        self.o_h, self.o_gb, self.o_gc = 0, c, 2 * c
        self.o_cq = 3 * c
        self.o_ckv = self.o_cq + q_rank
        self.n_a = self.o_ckv + kv_rank
        self.hw = n_heads * LANES


def _pre_common(dims, x, ctab, stab, gpre_ref, wa_ref, wkr_ref, wg_ref, qg_ref, wuqa_ref, wuqb_ref, kvg_ref,
                scale):
    d = dims
    xn = _rms(x, gpre_ref[...]).astype(BF16)

    def proj(w_ref, lo, hi):
        return jnp.dot(xn, w_ref[:, lo:hi], preferred_element_type=F32)

    h = proj(wa_ref, d.o_h, d.o_gb)
    gate_b = proj(wa_ref, d.o_gb, d.o_gc)
    gate_c = proj(wa_ref, d.o_gc, d.o_cq)
    cq = proj(wa_ref, d.o_cq, d.o_ckv)
    ckv = proj(wa_ref, d.o_ckv, d.n_a)
    kr2 = proj(wkr_ref, 0, 2 * LANES)
    kra, krb = kr2[:, :LANES], kr2[:, LANES:]
    g_conv = proj(wg_ref, 0, d.d_model)
    g_mla = proj(wg_ref, d.d_model, 2 * d.d_model)

    u = gate_c * h
    cqn = _rms(cq, qg_ref[...]).astype(BF16)
    qa = jnp.dot(cqn, wuqa_ref[...], preferred_element_type=F32)
    qbc = jnp.dot(cqn, wuqb_ref[...], preferred_element_type=F32)
    r = wuqb_ref.shape[1] // d.n_heads
    pieces = []
    for hd in range(d.n_heads):
        blk, off = divmod(hd * r, LANES)
        piece = qbc[:, blk * LANES:(blk + 1) * LANES]
        pieces.append(pltpu.roll(piece, LANES - off, 1) if off else piece)
    qb = jnp.concatenate(pieces, axis=1)
    ct = jnp.concatenate([ctab] * d.n_heads, axis=1)
    st = jnp.concatenate([stab] * d.n_heads, axis=1)
    q = (qa * ct + qb * st) * scale
    ckvn = _rms(ckv, kvg_ref[...])
    krr = kra * ctab + krb * stab
    return u, gate_b, g_conv, g_mla, q, ckvn, krr, cqn


def _pre_prompt_kernel(dims, scale, tm, dv, dvp,
                       x_ref, ctab_ref, stab_ref, ctabt_ref, stabt_ref, gpre_ref, wa_ref, wkr_ref, wg_ref, convw_ref,
                       wco_ref, qg_ref, wuqa_ref, wuqb_ref, wuqat_ref, wuqbt_ref, kvg_ref, wuk_ref, wuvt_ref,
                       qt_out, k_out, vt_out, ckv_out, kr_out, mc_out, sg_out, conv_out, ubuf):
    hist = SUBLANES

    @pl.when(pl.program_id(1) == 0)
    def _():
        ubuf[0:hist, :] = jnp.zeros((hist, dims.conv_dim), F32)

    u, gate_b, g_conv, g_mla, _, ckvn, krr, cqn = _pre_common(
        dims, x_ref[...], ctab_ref[...], stab_ref[...], gpre_ref, wa_ref, wkr_ref, wg_ref, qg_ref, wuqa_ref,
        wuqb_ref, kvg_ref, scale)

    ubuf[hist:hist + tm, :] = u
    u1 = ubuf[hist - 1:hist - 1 + tm, :]
    u2 = ubuf[hist - 2:hist - 2 + tm, :]
    cw = convw_ref[...]
    conv = u2 * cw[0:1, :] + u1 * cw[1:2, :] + u * cw[2:3, :]
    y_conv = _bdot(gate_b * conv, wco_ref[...])
    mc_out[...] = (jax.nn.sigmoid(g_conv) * y_conv).astype(mc_out.dtype)
    sg_out[...] = jax.nn.sigmoid(g_mla).astype(sg_out.dtype)
    conv_out[...] = ubuf[hist + tm - (CONV_WIDTH - 1):hist + tm, :]
    ubuf[0:hist, :] = ubuf[tm:tm + hist, :]

    nt_dims = (((1,), (1,)), ((), ()))
    qat = lax.dot_general(wuqat_ref[...], cqn, nt_dims, preferred_element_type=F32)
    qbt_c = lax.dot_general(wuqbt_ref[...], cqn, nt_dims, preferred_element_type=F32)
    r = wuqbt_ref.shape[0] // dims.n_heads
    zrows = jnp.zeros((LANES - r, tm), F32)
    qbt = jnp.concatenate([piece for hd in range(dims.n_heads)
                           for piece in (qbt_c[hd * r:(hd + 1) * r, :], zrows)], axis=0)
    ctt = jnp.concatenate([ctabt_ref[...]] * dims.n_heads, axis=0)
    stt = jnp.concatenate([stabt_ref[...]] * dims.n_heads, axis=0)
    qt_out[...] = ((qat * ctt + qbt * stt) * scale).astype(qt_out.dtype)
    ckv_out[...] = ckvn
    kr_out[...] = krr[:, :kr_out.shape[-1]]
    ckvb = ckvn.astype(BF16)
    ka = jnp.dot(ckvb, wuk_ref[...], preferred_element_type=F32)
    k = ka + jnp.concatenate([krr] * dims.n_heads, axis=1)
    k_out[...] = k.astype(k_out.dtype)
    vt = lax.dot_general(wuvt_ref[...], ckvb, (((1,), (1,)), ((), ())), preferred_element_type=F32)
    rid = lax.broadcasted_iota(jnp.int32, (vt.shape[0], 1), 0)
    vt_out[...] = (vt + jnp.where(rid % dvp == dv, 1.0, 0.0)).astype(vt_out.dtype)


def _pre_prompt(dims, scale, x2d, ctab, stab, w, batch, seq, rope_dim, dv, dvp):
    tm = ROW_TILE
    assert seq % tm == 0
    nt = seq // tm
    rows = batch * seq
    d = dims
    hv = w["w_uv_t"].shape[0]

    def row_spec(width):
        return pl.BlockSpec((tm, width), lambda b, t: (b * nt + t, 0))

    def tab_spec():
        return pl.BlockSpec((tm, LANES), lambda b, t: (t, 0))

    def tabt_spec():
        return pl.BlockSpec((LANES, tm), lambda b, t: (0, t))

    in_specs = [row_spec(d.d_model), tab_spec(), tab_spec(), tabt_spec(), tabt_spec(),
                _const_spec((1, d.d_model)), _const_spec((d.d_model, d.n_a)),
                _const_spec((d.d_model, 2 * LANES)), _const_spec((d.d_model, 2 * d.d_model)),
                _const_spec((CONV_WIDTH, d.conv_dim)), _const_spec((d.conv_dim, d.d_model)),
                _const_spec((1, d.q_rank)), _const_spec((d.q_rank, d.hw)),
                _const_spec((d.q_rank, d.n_heads * rope_dim)),
                _const_spec((d.hw, d.q_rank)), _const_spec((d.n_heads * rope_dim, d.q_rank)),
                _const_spec((1, d.kv_rank)), _const_spec((d.kv_rank, d.hw)), _const_spec((hv, d.kv_rank))]
    out_shape = (jax.ShapeDtypeStruct((batch, d.hw, seq), BF16),
                 jax.ShapeDtypeStruct((rows, d.hw), BF16),
                 jax.ShapeDtypeStruct((batch, hv, seq), BF16),
                 jax.ShapeDtypeStruct((rows, d.kv_rank), F32),
                 jax.ShapeDtypeStruct((rows, rope_dim), F32),
                 jax.ShapeDtypeStruct((rows, d.d_model), BF16),
                 jax.ShapeDtypeStruct((rows, d.d_model), BF16),
                 jax.ShapeDtypeStruct((batch, CONV_WIDTH - 1, d.conv_dim), F32))
    out_specs = (pl.BlockSpec((None, d.hw, tm), lambda b, t: (b, 0, t)), row_spec(d.hw),
                 pl.BlockSpec((None, hv, tm), lambda b, t: (b, 0, t)), row_spec(d.kv_rank),
                 row_spec(rope_dim), row_spec(d.d_model), row_spec(d.d_model),
                 pl.BlockSpec((None, CONV_WIDTH - 1, d.conv_dim), lambda b, t: (b, 0, 0)))
    kern = functools.partial(_pre_prompt_kernel, dims, scale, tm, dv, dvp)
    return pl.pallas_call(
        kern, grid=(batch, nt), in_specs=in_specs, out_specs=out_specs, out_shape=out_shape,
        scratch_shapes=[pltpu.VMEM((tm + SUBLANES, d.conv_dim), F32)],
        compiler_params=_params("arbitrary", "arbitrary"), name="pre_prompt",
    )(x2d, ctab, stab, ctab.T, stab.T, w["g_pre"], w["w_in_a"], w["w_in_kr"], w["w_in_g"], w["conv_w"],
      w["w_conv_out"], w["q_g"], w["w_uq_a"], w["w_uq_b"], w["w_uq_a"].T, w["w_uq_b"].T, w["kv_g"], w["w_uk_pad"],
      w["w_uv_t"])


def _pre_sample_kernel(dims, scale,
                       x_ref, state_ref, ctab_ref, stab_ref, gpre_ref, wa_ref, wkr_ref, wg_ref, convw_ref, wco_ref,
                       qg_ref, wuqa_ref, wuqb_ref, kvg_ref, wukt_ref,
                       q_out, qlat_out, ckv_out, kr_out, mc_out, sg_out, conv_out):
    c = dims.conv_dim
    u, gate_b, g_conv, g_mla, q, ckvn, krr, _ = _pre_common(
        dims, x_ref[...], ctab_ref[...], stab_ref[...], gpre_ref, wa_ref, wkr_ref, wg_ref, qg_ref, wuqa_ref,
        wuqb_ref, kvg_ref, scale)
    u2 = state_ref[:, 0:c]
    u1 = state_ref[:, c:2 * c]
    cw = convw_ref[...]
    conv = u2 * cw[0:1, :] + u1 * cw[1:2, :] + u * cw[2:3, :]
    y_conv = _bdot(gate_b * conv, wco_ref[...])
    mc_out[...] = jax.nn.sigmoid(g_conv) * y_conv
    sg_out[...] = jax.nn.sigmoid(g_mla)
    conv_out[:, 0:c] = u1
    conv_out[:, c:2 * c] = u
    q_out[...] = q
    ckv_out[...] = ckvn
    kr_out[...] = krr
    kvr = dims.kv_rank
    qb16 = q.astype(BF16)
    for hd in range(dims.n_heads):
        blk = qb16[:, hd * LANES:(hd + 1) * LANES]
        qlat_out[:, hd * kvr:(hd + 1) * kvr] = jnp.dot(
            blk, wukt_ref[hd * LANES:(hd + 1) * LANES, :], preferred_element_type=F32)


def _pre_sample(dims, scale, xs, state2d, ctab, stab, w):
    n = xs.shape[0]
    d = dims
    out_shape = (jax.ShapeDtypeStruct((n, d.hw), F32),
                 jax.ShapeDtypeStruct((n, d.n_heads * d.kv_rank), F32),
                 jax.ShapeDtypeStruct((n, d.kv_rank), F32),
                 jax.ShapeDtypeStruct((n, LANES), F32),
                 jax.ShapeDtypeStruct((n, d.d_model), F32),
                 jax.ShapeDtypeStruct((n, d.d_model), F32),
                 jax.ShapeDtypeStruct((n, (CONV_WIDTH - 1) * d.conv_dim), F32))
    kern = functools.partial(_pre_sample_kernel, dims, scale)
    return pl.pallas_call(kern, out_shape=out_shape, compiler_params=_params(), name="pre_sample")(
        xs, state2d, ctab, stab, w["g_pre"], w["w_in_a"], w["w_in_kr"], w["w_in_g"], w["conv_w"], w["w_conv_out"],
        w["q_g"],
        w["w_uq_a"], w["w_uq_b"], w["kv_g"], w["w_uk_t_pad"])


def _attn_kernel(tq, tk, hp, dv, dvp, nq, qt_ref, k_ref, vt_ref, o_ref, sa_ref, sb_ref):
    assert tq == 2 * tk
    half = tq // 2

    def tile(qi, carry):
        q0 = pl.multiple_of(qi * tq, tq)

        def score_group(h, j, dst, n, q_lo, qbase):
            kh = k_ref[pl.ds(pl.multiple_of(j * tk, tk), tk), h * LANES:(h + 1) * LANES]
            qth = qt_ref[h * LANES:(h + 1) * LANES,
                         pl.ds(pl.multiple_of(qbase + q_lo + n * MXU_WIDTH, MXU_WIDTH), MXU_WIDTH)]
            dst[h, :, n * MXU_WIDTH:(n + 1) * MXU_WIDTH] = jnp.dot(kh, qth, preferred_element_type=F32)

        def step(state, proc=None, sc=None):
            n_s = (tq - sc[2]) // MXU_WIDTH if sc else 0
            if proc is None:
                for h in range(hp):
                    for n in range(n_s):
                        score_group(h, sc[0], sc[1], n, sc[2], sc[3])
                return state
            j, src, masked, q_lo = proc
            start = pl.multiple_of(j * tk, tk)
            nq_p = tq - q_lo
            n_p = nq_p // MXU_WIDTH
            new_state = []
            for h in range(hp):
                m_prev, acc = state[h]
                st = src[h, :, 0:nq_p]
                if masked:
                    key = lax.broadcasted_iota(jnp.int32, (tk, nq_p), 0) + j * tk
                    qry = lax.broadcasted_iota(jnp.int32, (tk, nq_p), 1) + qi * tq + q_lo
                    st = jnp.where(key <= qry, st, NEG_INF)
                m_new = jnp.maximum(m_prev[:, q_lo:], jnp.max(st, axis=0, keepdims=True))
                alpha = jnp.exp2(m_prev[:, q_lo:] - m_new)
                p = jnp.exp2(st - m_new).astype(BF16)
                vth = vt_ref[h * dvp:(h + 1) * dvp, pl.ds(start, tk)]
                pieces = [acc[:, :q_lo]] if q_lo else []
                for n in range(max(n_p, n_s)):
                    if n < n_s:
                        score_group(h, sc[0], sc[1], n, sc[2], sc[3])
                    if n < n_p:
                        ql = slice(n * MXU_WIDTH, (n + 1) * MXU_WIDTH)
                        pieces.append(alpha[:, ql] * acc[:, q_lo + n * MXU_WIDTH:q_lo + (n + 1) * MXU_WIDTH]
                                      + jnp.dot(vth, p[:, ql], preferred_element_type=F32))
                if q_lo:
                    m_new = jnp.concatenate([m_prev[:, :q_lo], m_new], axis=1)
                new_state.append((m_new, jnp.concatenate(pieces, axis=1)))
            return tuple(new_state)

        init = tuple((jnp.full((1, tq), NEG_INF, F32), jnp.zeros((dvp, tq), F32)) for _ in range(hp))

        @pl.when(qi == 0)
        def _():
            step(None, sc=(0, sa_ref, 0, q0))

        def pair(t, state):
            state = step(state, proc=(2 * t, sa_ref, False, 0), sc=(2 * t + 1, sb_ref, 0, q0))
            return step(state, proc=(2 * t + 1, sb_ref, False, 0), sc=(2 * t + 2, sa_ref, 0, q0))

        state = lax.fori_loop(0, qi, pair, init)
        q_next = pl.multiple_of(jnp.minimum(qi + 1, nq - 1) * tq, tq)
        state = step(state, proc=(2 * qi, sa_ref, True, 0), sc=(2 * qi + 1, sb_ref, half, q0))
        state = step(state, proc=(2 * qi + 1, sb_ref, True, half), sc=(0, sa_ref, 0, q_next))
        ot = jnp.concatenate([acc[0:dv] / acc[dv:dv + 1] for (_, acc) in state], axis=0)
        o_ref[pl.ds(q0, tq), :] = ot.T.astype(o_ref.dtype)
        return carry

    lax.fori_loop(0, nq, tile, 0)


def _prompt_attention(qt, k, vt, batch, seq, n_heads, dv):
    tq, tk = ATTN_TQ, ATTN_TK
    assert seq % tq == 0 and tq % tk == 0
    hp = LANES // dv
    assert n_heads % hp == 0
    dvp = vt.shape[1] // n_heads
    nq = seq // tq
    kern = functools.partial(_attn_kernel, tq, tk, hp, dv, dvp, nq)
    return pl.pallas_call(
        kern, grid=(batch, n_heads // hp),
        in_specs=[pl.BlockSpec((None, hp * LANES, seq), lambda b, g: (b, g, 0)),
                  pl.BlockSpec((seq, hp * LANES), lambda b, g: (b, g)),
                  pl.BlockSpec((None, hp * dvp, seq), lambda b, g: (b, g, 0))],
        out_specs=pl.BlockSpec((seq, hp * dv), lambda b, g: (b, g)),
        out_shape=jax.ShapeDtypeStruct((batch * seq, n_heads * dv), BF16),
        scratch_shapes=[pltpu.VMEM((hp, tk, tq), F32), pltpu.VMEM((hp, tk, tq), F32)],
        compiler_params=_params("arbitrary", "arbitrary"), name="prompt_attn",
    )(qt, k, vt)


def _mix_and_query(x, o, mc, sg, wmla_ref, wmix_ref, gmixpost_ref, gcapre_ref, wcaq_ref, ca_scale):
    y_mla = _bdot(o, wmla_ref[...])
    merged = mc.astype(F32) + sg.astype(F32) * y_mla
    y = _bdot(merged, wmix_ref[...])
    x1 = x + _rms(y, gmixpost_ref[...])
    qc = _bdot(_rms(x1, gcapre_ref[...]), wcaq_ref[...]) * ca_scale
    return x1, qc


def _mlp_tail(x1, oc, wcao_ref, gcapost_ref, gmlppre_ref, wup_ref, wdown_ref, gmlppost_ref):
    ca = _bdot(oc, wcao_ref[...])
    x2 = x1 + _rms(ca, gcapost_ref[...])
    hid = jnp.square(jnp.maximum(_bdot(_rms(x2, gmlppre_ref[...]), wup_ref[...]), 0.0))
    return x2 + _rms(_bdot(hid, wdown_ref[...]), gmlppost_ref[...])


def _post_prompt_kernel(ca_heads, ca_scale,
                        x_ref, o_ref, mc_ref, sg_ref, mk_ref, mv_ref,
                        wmla_ref, wmix_ref, gmixpost_ref, gcapre_ref, wcaq_ref, wcao_ref, gcapost_ref,
                        gmlppre_ref, wup_ref, wdown_ref, gmlppost_ref, y_ref):
    x1, qc = _mix_and_query(x_ref[...], o_ref[...], mc_ref[...], sg_ref[...], wmla_ref, wmix_ref,
                            gmixpost_ref, gcapre_ref, wcaq_ref, ca_scale)
    hd = qc.shape[1] // ca_heads
    outs = []
    for h in range(ca_heads):
        sl = slice(h * hd, (h + 1) * hd)
        p = _softmax_rows(_bdot_nt(qc[:, sl], mk_ref[:, sl]))
        outs.append(_bdot(p, mv_ref[:, sl]))
    oc = jnp.concatenate(outs, axis=1)
    y_ref[...] = _mlp_tail(x1, oc, wcao_ref, gcapost_ref, gmlppre_ref, wup_ref, wdown_ref, gmlppost_ref)


def _post_prompt(x2d, o, mc, sg, mk, mv, w, batch, seq, ca_heads, n_mem):
    tm = ROW_TILE
    nt = seq // tm
    rows, dm = x2d.shape
    dff = w["w_ff_up"].shape[1]
    ca_scale = float((dm // ca_heads) ** -0.5)

    def row_spec(width):
        return pl.BlockSpec((tm, width), lambda i: (i, 0))

    mem_spec = pl.BlockSpec((n_mem, dm), lambda i: (i // nt, 0))
    in_specs = [row_spec(dm), row_spec(o.shape[1]), row_spec(dm), row_spec(dm), mem_spec, mem_spec,
                _const_spec((o.shape[1], dm)), _const_spec((dm, dm)), _const_spec((1, dm)),
                _const_spec((1, dm)), _const_spec((dm, dm)), _const_spec((dm, dm)), _const_spec((1, dm)),
                _const_spec((1, dm)), _const_spec((dm, dff)), _const_spec((dff, dm)), _const_spec((1, dm))]
    kern = functools.partial(_post_prompt_kernel, ca_heads, ca_scale)
    return pl.pallas_call(
        kern, grid=(rows // tm,), in_specs=in_specs, out_specs=row_spec(dm),
        out_shape=jax.ShapeDtypeStruct((rows, dm), F32),
        compiler_params=_params("arbitrary"), name="post_prompt",
    )(x2d, o, mc, sg, mk, mv, w["w_mla"], w["w_mix"], w["g_mix_post"], w["g_ca_pre"], w["w_ca_q"],
      w["w_ca_o"], w["g_ca_post"], w["g_mlp_pre"], w["w_ff_up"], w["w_ff_down"], w["g_mlp_post"])


def _post_sample_a_kernel(n_heads, v_dim, ca_scale,
                          x_ref, olat_ref, mc_ref, sg_ref, wuv_ref, wmla_ref, wmix_ref, gmixpost_ref,
                          gcapre_ref, wcaq_ref, x1_ref, qc_ref):
    n = x_ref.shape[0]
    full = _bdot(olat_ref[...], wuv_ref[...])
    full = full.reshape(n, n_heads, n_heads * v_dim)
    hidx = lax.broadcasted_iota(jnp.int32, full.shape, 1)
    lane_head = lax.broadcasted_iota(jnp.int32, full.shape, 2) // v_dim
    o = jnp.sum(jnp.where(hidx == lane_head, full, 0.0), axis=1)
    x1, qc = _mix_and_query(x_ref[...], o, mc_ref[...], sg_ref[...], wmla_ref, wmix_ref,
                            gmixpost_ref, gcapre_ref, wcaq_ref, ca_scale)
    x1_ref[...] = x1
    qc_ref[...] = qc


def _post_sample_b_kernel(ca_heads, q_ref, mk_ref, mv_ref, o_ref):
    g = q_ref.shape[0]
    rows = mk_ref.shape[1]
    s8 = 2 * ca_heads
    assert s8 == SUBLANES and mk_ref.shape[2] == LANES
    lane = lax.broadcasted_iota(jnp.int32, (s8, rows), 1)
    sub = lax.broadcasted_iota(jnp.int32, (s8, rows), 0)
    own = (lane % s8) == sub
    low_half = (lax.broadcasted_iota(jnp.int32, (1, rows), 1) % s8) < ca_heads
    for i in range(g):
        q = q_ref[i]
        q8 = jnp.concatenate([q[:, (2 * (j % ca_heads) + j // ca_heads) * LANES:
                                   (2 * (j % ca_heads) + j // ca_heads + 1) * LANES] for j in range(s8)], axis=0)
        kb = mk_ref[i].astype(BF16)
        p_all = lax.dot_general(q8.astype(BF16), kb, (((1,), (1,)), ((), ())), preferred_element_type=F32)
        d = jnp.sum(jnp.where(own, p_all, 0.0), axis=0, keepdims=True)
        d = d + jnp.where(low_half, pltpu.roll(d, rows - ca_heads, 1), pltpu.roll(d, ca_heads, 1))
        dm = jnp.where(own, jnp.broadcast_to(d, (s8, rows)), NEG_INF)
        e = jnp.exp(dm - jnp.max(dm, axis=-1, keepdims=True))
        w = (e / jnp.sum(e, axis=-1, keepdims=True)).astype(BF16)
        o8 = jnp.dot(w, mv_ref[i].astype(BF16), preferred_element_type=F32)
        for j in range(s8):
            blk = 2 * (j % ca_heads) + j // ca_heads
            o_ref[i, :, blk * LANES:(blk + 1) * LANES] = o8[j:j + 1, :]


def _post_sample_c_kernel(x1_ref, oc_ref, wcao_ref, gcapost_ref, gmlppre_ref, wup_ref, wdown_ref,
                          gmlppost_ref, y_ref):
    y_ref[...] = _mlp_tail(x1_ref[...], oc_ref[...], wcao_ref, gcapost_ref, gmlppre_ref, wup_ref,
                           wdown_ref, gmlppost_ref)


def _post_sample(xs, olat, mc, sg, mem_k, mem_v, w, n_heads, v_dim, ca_heads):
    n, dm = xs.shape
    ca_scale = float((dm // ca_heads) ** -0.5)
    kern_a = functools.partial(_post_sample_a_kernel, n_heads, v_dim, ca_scale)
    x1, qc = pl.pallas_call(
        kern_a, out_shape=(jax.ShapeDtypeStruct((n, dm), F32), jax.ShapeDtypeStruct((n, dm), F32)),
        compiler_params=_params(), name="post_sample_a",
    )(xs, olat, mc, sg, w["w_uv_flat"], w["w_mla"], w["w_mix"], w["g_mix_post"], w["g_ca_pre"], w["w_ca_q"])

    g = CA_SAMPLES_PER_STEP
    assert n % g == 0
    kern_b = functools.partial(_post_sample_b_kernel, ca_heads)
    mem_spec = pl.BlockSpec((g,) + mem_k.shape[1:], lambda i: (i, 0, 0))
    oc = pl.pallas_call(
        kern_b, grid=(n // g,),
        in_specs=[pl.BlockSpec((g, 1, dm), lambda i: (i, 0, 0)), mem_spec, mem_spec],
        out_specs=pl.BlockSpec((g, 1, dm), lambda i: (i, 0, 0)),
        out_shape=jax.ShapeDtypeStruct((n, 1, dm), F32),
        compiler_params=_params("arbitrary"), name="post_sample_b",
    )(qc.reshape(n, 1, dm), mem_k, mem_v)

    return pl.pallas_call(
        _post_sample_c_kernel, out_shape=jax.ShapeDtypeStruct((n, dm), F32),
        compiler_params=_params(), name="post_sample_c",
    )(x1, oc.reshape(n, dm), w["w_ca_o"], w["g_ca_post"], w["g_mlp_pre"], w["w_ff_up"], w["w_ff_down"],
      w["g_mlp_post"])


def _dec_attn_kernel(n_samples, n_pages, n_heads, rope_dim,
                     pt_ref, qlat_ref, qrope_ref, ckvn_ref, krn_ref, cache_ckv, cache_krt,
                     o_ref, cbuf, kbuf, sems):
    ch = DEC_PAGES_PER_STEP
    nch = n_pages // ch
    total = n_samples * nch
    page, kvr = cbuf.shape[2], cbuf.shape[3]

    def copies(g, slot):
        b = g // nch
        c = g % nch
        out = []
        for i in range(ch):
            pg = pt_ref[b, c * ch + i]
            out.append(pltpu.make_async_copy(cache_ckv.at[pg], cbuf.at[slot, i], sems.at[0, slot]))
            out.append(pltpu.make_async_copy(cache_krt.at[pg], kbuf.at[slot, i], sems.at[1, slot]))
        return out

    def start(g):
        for n, cp in enumerate(copies(g, g % DEC_NBUF)):
            cp.start(priority=(n // 2) % 2)

    def wait(g):
        for cp in copies(g, g % DEC_NBUF):
            cp.wait()

    def q_rows(b):
        row0 = pl.multiple_of(b * n_heads, n_heads)
        return qlat_ref[pl.ds(row0, n_heads), :], qrope_ref[pl.ds(row0, n_heads), :][:, :rope_dim]

    def scores(g):
        ql, qr = q_rows(g // nch)
        slot = g % DEC_NBUF
        ckv = cbuf[slot].reshape(ch * page, kvr).astype(BF16)
        qrb = qr.astype(BF16)
        s_rope = jnp.concatenate(
            [jnp.dot(qrb, kbuf[slot, i].astype(BF16), preferred_element_type=F32) for i in range(ch)], axis=1)
        return lax.dot_general(ql.astype(BF16), ckv, (((1,), (1,)), ((), ())),
                               preferred_element_type=F32) + s_rope

    for g0 in range(DEC_NBUF - 1):
        start(g0)
    wait(0)
    s0 = scores(0)

    def body(g, carry):
        s_cur, m, l, acc = carry
        b = g // nch
        c = g % nch

        @pl.when(g + 1 < total)
        def _():
            wait(g + 1)

        @pl.when(g + (DEC_NBUF - 1) < total)
        def _():
            start(g + (DEC_NBUF - 1))

        s_next = scores(jnp.minimum(g + 1, total - 1))

        ql, qr = q_rows(b)
        c_new = ckvn_ref[pl.ds(b, 1), :]
        r_new = krn_ref[pl.ds(b, 1), :][:, :rope_dim]
        s_new = jnp.sum(ql * c_new, axis=-1, keepdims=True) + jnp.sum(qr * r_new, axis=-1, keepdims=True)
        first = c == 0
        m = jnp.where(first, s_new, m)
        l = jnp.where(first, 1.0, l)
        acc = jnp.where(first, jnp.broadcast_to(c_new, acc.shape), acc)

        ckv = cbuf[g % DEC_NBUF].reshape(ch * page, kvr).astype(BF16)
        m_new = jnp.maximum(m, jnp.max(s_cur, axis=-1, keepdims=True))
        alpha = jnp.exp(m - m_new)
        p = jnp.exp(s_cur - m_new)
        l = alpha * l + jnp.sum(p, axis=-1, keepdims=True)
        acc = alpha * acc + jnp.dot(p.astype(BF16), ckv, preferred_element_type=F32)
        o_ref[pl.ds(pl.multiple_of(b * n_heads, n_heads), n_heads), :] = acc / l
        return s_next, m_new, l, acc

    init = (s0, jnp.zeros((n_heads, 1), F32), jnp.zeros((n_heads, 1), F32), jnp.zeros((n_heads, kvr), F32))
    lax.fori_loop(0, total, body, init)


def _decode_attention(page_table, qlat, qrope, ckv_new, kr_new, cache_ckv, cache_krt, n_heads, rope_dim):
    n_samples, n_pages = page_table.shape
    page, kvr = cache_ckv.shape[1], cache_ckv.shape[2]
    ch = DEC_PAGES_PER_STEP
    assert n_pages % ch == 0

    def whole(shape):
        nd = len(shape)
        return pl.BlockSpec(shape, lambda i, pt: (0,) * nd)

    kern = functools.partial(_dec_attn_kernel, n_samples, n_pages, n_heads, rope_dim)
    grid_spec = pltpu.PrefetchScalarGridSpec(
        num_scalar_prefetch=1, grid=(1,),
        in_specs=[whole(qlat.shape), whole(qrope.shape), whole(ckv_new.shape), whole(kr_new.shape),
                  pl.BlockSpec(memory_space=pl.ANY), pl.BlockSpec(memory_space=pl.ANY)],
        out_specs=whole(qlat.shape),
        scratch_shapes=[pltpu.VMEM((DEC_NBUF, ch, page, kvr), F32),
                        pltpu.VMEM((DEC_NBUF, ch, rope_dim, page), F32),
                        pltpu.SemaphoreType.DMA((2, DEC_NBUF))])
    return pl.pallas_call(
        kern, grid_spec=grid_spec, out_shape=jax.ShapeDtypeStruct(qlat.shape, F32),
        compiler_params=_params("arbitrary"), name="decode_attn",
    )(page_table, qlat, qrope, ckv_new, kr_new, cache_ckv, cache_krt)


def _rot_half(w):
    half = w.shape[-1] // 2
    return jnp.concatenate([-w[..., half:], w[..., :half]], axis=-1)


def _mem_rows(cache):
    _, n, n_mem, heads, hd = cache.shape
    assert hd == 2 * LANES
    x = cache.reshape(n, n_mem, heads, 2, LANES)
    return jnp.transpose(x, (0, 1, 3, 2, 4)).reshape(n, n_mem * 2 * heads, LANES)


def _rope_lane_freqs(rope_dim):
    inv = 1.0 / (ROPE_BASE ** (jnp.arange(0, rope_dim, 2, dtype=F32) / rope_dim))
    return jnp.concatenate([inv, inv, jnp.zeros((LANES - rope_dim,), F32)])


def _rope_tabs(pos, rope_dim, nope_dim):
    ang = pos.astype(F32)[:, None] * _rope_lane_freqs(rope_dim)[None, :]
    keep = (jnp.arange(LANES) < rope_dim + nope_dim).astype(F32)
    return jnp.cos(ang) * keep, jnp.sin(ang)


def _rope_tabs_range(seq, rope_dim, nope_dim):
    assert seq % LANES == 0
    freqs = _rope_lane_freqs(rope_dim)
    a = (jnp.arange(seq // LANES, dtype=F32) * LANES)[:, None, None] * freqs
    b = jnp.arange(LANES, dtype=F32)[None, :, None] * freqs
    ca, sa, cb, sb = jnp.cos(a), jnp.sin(a), jnp.cos(b), jnp.sin(b)
    keep = (jnp.arange(LANES) < rope_dim + nope_dim).astype(F32)
    ctab = (ca * cb - sa * sb) * keep
    stab = sa * cb + ca * sb
    return ctab.reshape(seq, LANES), stab.reshape(seq, LANES)


def kernel(x_prompt, x_sample, mem_prompt, cache_ckv, cache_krope, state_conv, cache_mem_k, cache_mem_v,
           page_table, norm_mix_pre_g, w_in, conv_w, w_conv_out, q_norm_g, w_uq, kv_norm_g, w_uk, w_uv,
           w_mla_out, w_mix_out, norm_mix_post_g, norm_ca_pre_g, mem_norm_g, w_ca_q, w_ca_k, w_ca_v, w_ca_o,
           norm_ca_post_g, norm_mlp_pre_g, w_ff_up, w_ff_down, norm_mlp_post_g):
    depth = w_in.shape[0]
    assert depth == 1, "single-layer step"
    batch, seq, dm = x_prompt.shape
    n_s, t_s, _ = x_sample.shape
    assert t_s == 1
    conv_dim = conv_w.shape[2]
    q_rank, n_heads, qk_dim = w_uq.shape[1:]
    kv_rank, _, nope_dim = w_uk.shape[1:]
    v_dim = w_uv.shape[3]
    rope_dim = qk_dim - nope_dim
    n_mem, ca_heads, ca_hd = cache_mem_k.shape[2:]
    n_pool, page = cache_ckv.shape[1:3]
    past_len = page_table.shape[1] * page
    assert rope_dim + nope_dim <= LANES and v_dim <= LANES
    dims = _PreDims(dm, conv_dim, q_rank, kv_rank, n_heads)
    scale = float(qk_dim ** -0.5)

    w0 = w_in[0]
    o_kr = 3 * conv_dim + q_rank + kv_rank
    w_kr = w0[:, o_kr:o_kr + rope_dim]
    zpad = jnp.zeros((dm, LANES - rope_dim), F32)
    w_in_kr = jnp.concatenate([w_kr, zpad, _rot_half(w_kr), zpad], axis=1).astype(BF16)
    uq = w_uq[0]
    uq_nope, uq_rope = uq[:, :, :nope_dim], uq[:, :, nope_dim:]
    hz = lambda r, width: jnp.zeros((r, n_heads, width), F32)
    pad_tail = LANES - rope_dim - nope_dim
    w_uq_a = jnp.concatenate([uq_rope, uq_nope, hz(q_rank, pad_tail)], axis=2).reshape(q_rank, n_heads * LANES)
    assert LANES % rope_dim == 0
    w_uq_b = _rot_half(uq_rope).reshape(q_rank, n_heads * rope_dim)
    uk = w_uk[0]
    w_uk_pad = jnp.concatenate([hz(kv_rank, rope_dim), uk, hz(kv_rank, pad_tail)], axis=2).reshape(
        kv_rank, n_heads * LANES)
    w_uk_t_pad = jnp.transpose(w_uk_pad.reshape(kv_rank, n_heads, LANES), (1, 2, 0)).reshape(
        n_heads * LANES, kv_rank)
    uv = w_uv[0].reshape(kv_rank, n_heads * v_dim)
    dvp = v_dim + ATTN_V_EXTRA_ROWS
    w_uv_t = jnp.concatenate([jnp.transpose(w_uv[0], (1, 2, 0)),
                              jnp.zeros((n_heads, ATTN_V_EXTRA_ROWS, kv_rank), F32)], axis=1).reshape(
        n_heads * dvp, kv_rank)
    w_mla = w_mla_out[0]
    w = {
        "g_pre": norm_mix_pre_g, "w_in_a": w0[:, :o_kr].astype(BF16), "w_in_kr": w_in_kr,
        "w_in_g": w0[:, o_kr + rope_dim:].astype(BF16), "conv_w": conv_w[0], "w_conv_out": w_conv_out[0].astype(BF16),
        "q_g": q_norm_g, "w_uq_a": w_uq_a.astype(BF16), "w_uq_b": w_uq_b.astype(BF16), "kv_g": kv_norm_g,
        "w_uk_pad": w_uk_pad.astype(BF16), "w_uk_t_pad": w_uk_t_pad.astype(BF16),
        "w_uv_t": w_uv_t.astype(BF16), "w_uv_flat": uv.astype(BF16),
        "w_mla": w_mla.astype(BF16), "w_mix": w_mix_out[0].astype(BF16),
        "g_mix_post": norm_mix_post_g, "g_ca_pre": norm_ca_pre_g,
        "w_ca_q": w_ca_q[0].reshape(dm, ca_heads * ca_hd).astype(BF16),
        "w_ca_o": w_ca_o[0].reshape(ca_heads * ca_hd, dm).astype(BF16),
        "g_ca_post": norm_ca_post_g, "g_mlp_pre": norm_mlp_pre_g, "w_ff_up": w_ff_up[0].astype(BF16),
        "w_ff_down": w_ff_down[0].astype(BF16), "g_mlp_post": norm_mlp_post_g,
    }

    mk_p, mv_p = _memory_kv(mem_prompt.reshape(batch * n_mem, dm), mem_norm_g,
                            w_ca_k[0].reshape(dm, ca_heads * ca_hd).astype(BF16),
                            w_ca_v[0].reshape(dm, ca_heads * ca_hd).astype(BF16))
    ctab_p, stab_p = _rope_tabs_range(seq, rope_dim, nope_dim)
    x2d = x_prompt.reshape(batch * seq, dm)
    q, k, vt, ckv_p, kr_p, mc_p, sg_p, conv_p = _pre_prompt(dims, scale * LOG2_E, x2d, ctab_p, stab_p, w, batch,
                                                           seq, rope_dim, v_dim, dvp)
    o_p = _prompt_attention(q, k, vt, batch, seq, n_heads, v_dim)
    y_p = _post_prompt(x2d, o_p, mc_p, sg_p, mk_p, mv_p, w, batch, seq, ca_heads, n_mem)

    ctab_s, stab_s = _rope_tabs(jnp.full((n_s,), past_len, jnp.int32), rope_dim, nope_dim)
    xs = x_sample.reshape(n_s, dm)
    q_s, qlat_s, ckv_s, kr_s, mc_s, sg_s, conv_s = _pre_sample(
        dims, scale, xs, state_conv.reshape(n_s, (CONV_WIDTH - 1) * conv_dim), ctab_s, stab_s, w)
    olat = _decode_attention(page_table, qlat_s.reshape(n_s * n_heads, kv_rank),
                             q_s.reshape(n_s * n_heads, LANES), ckv_s, kr_s,
                             cache_ckv.reshape(n_pool, page, kv_rank),
                             jnp.swapaxes(cache_krope.reshape(n_pool, page, rope_dim), 1, 2), n_heads, rope_dim)
    y_s = _post_sample(xs, olat, mc_s, sg_s, _mem_rows(cache_mem_k), _mem_rows(cache_mem_v), w, n_heads, v_dim,
                       ca_heads)

    return (y_p.reshape(batch, seq, dm),
            y_s.reshape(n_s, t_s, dm),
            ckv_p.reshape(depth, batch, seq, kv_rank),
            kr_p.reshape(depth, batch, seq, rope_dim),
            conv_p.reshape(depth, batch, CONV_WIDTH - 1, conv_dim),
            mk_p.reshape(depth, batch, n_mem, ca_heads, ca_hd),
            mv_p.reshape(depth, batch, n_mem, ca_heads, ca_hd),
            ckv_s.reshape(depth, n_s, t_s, kv_rank),
            kr_s[:, :rope_dim].reshape(depth, n_s, t_s, rope_dim),
            conv_s.reshape(depth, n_s, CONV_WIDTH - 1, conv_dim))
```

```python
import functools

import jax
import jax.numpy as jnp
from jax import lax
from jax.experimental import pallas as pl
from jax.experimental.pallas import tpu as pltpu

F32 = jnp.float32
BF16 = jnp.bfloat16

RMS_EPS = 1e-6
NEG_INF = -1e30
ROPE_BASE = 10000.0
CONV_WIDTH = 3
LOG2_E = 1.4426950408889634

LANES = 128
SUBLANES = 8
MXU_WIDTH = 256
VMEM_LIMIT_BYTES = 56 * 1024 * 1024

ROW_TILE = 512
ATTN_TQ = 1024
ATTN_TK = 512
ATTN_V_EXTRA_ROWS = 16
DEC_PAGES_PER_STEP = 32
DEC_NBUF = 5
CA_SAMPLES_PER_STEP = 8


def _rms(x, g):
    return x * lax.rsqrt(jnp.mean(x * x, axis=-1, keepdims=True) + RMS_EPS) * g


def _bdot(a, b):
    return jnp.dot(a.astype(BF16), b.astype(BF16), preferred_element_type=F32)


def _bdot_nt(a, b):
    return lax.dot_general(a.astype(BF16), b.astype(BF16), (((1,), (1,)), ((), ())),
                           preferred_element_type=F32)


def _softmax_rows(s):
    m = jnp.max(s, axis=-1, keepdims=True)
    p = jnp.exp(s - m)
    return p / jnp.sum(p, axis=-1, keepdims=True)


def _const_spec(shape):
    nd = len(shape)
    return pl.BlockSpec(shape, lambda *_: (0,) * nd, pipeline_mode=pl.Buffered(1))


def _params(*sem):
    return pltpu.CompilerParams(dimension_semantics=tuple(sem) if sem else None,
                                vmem_limit_bytes=VMEM_LIMIT_BYTES)


def _memkv_kernel(mem_ref, g_ref, wk_ref, wv_ref, k_ref, v_ref):
    mn = _rms(mem_ref[...], g_ref[...]).astype(BF16)
    k_ref[...] = jnp.dot(mn, wk_ref[...], preferred_element_type=F32)
    v_ref[...] = jnp.dot(mn, wv_ref[...], preferred_element_type=F32)


def _memory_kv(mem2d, g, wk, wv):
    rows, d = mem2d.shape
    out = jax.ShapeDtypeStruct((rows, wk.shape[1]), F32)
    return pl.pallas_call(_memkv_kernel, out_shape=(out, out), name="mem_kv",
                          compiler_params=_params())(mem2d, g, wk, wv)


class _PreDims:
    def __init__(self, d_model, conv_dim, q_rank, kv_rank, n_heads):
        self.d_model, self.conv_dim, self.q_rank, self.kv_rank = d_model, conv_dim, q_rank, kv_rank
        self.n_heads = n_heads
        c = conv_dim
        self.o_h, self.o_gb, self.o_gc = 0, c, 2 * c
        self.o_cq = 3 * c
        self.o_ckv = self.o_cq + q_rank
        self.n_a = self.o_ckv + kv_rank
        self.hw = n_heads * LANES


def _pre_common(dims, x, ctab, stab, gpre_ref, wa_ref, wkr_ref, wg_ref, qg_ref, wuqa_ref, wuqb_ref, kvg_ref,
                scale):
    d = dims
    xn = _rms(x, gpre_ref[...]).astype(BF16)

    def proj(w_ref, lo, hi):
        return jnp.dot(xn, w_ref[:, lo:hi], preferred_element_type=F32)

    h = proj(wa_ref, d.o_h, d.o_gb)
    gate_b = proj(wa_ref, d.o_gb, d.o_gc)
    gate_c = proj(wa_ref, d.o_gc, d.o_cq)
    cq = proj(wa_ref, d.o_cq, d.o_ckv)
    ckv = proj(wa_ref, d.o_ckv, d.n_a)
    kr2 = proj(wkr_ref, 0, 2 * LANES)
    kra, krb = kr2[:, :LANES], kr2[:, LANES:]
    g_conv = proj(wg_ref, 0, d.d_model)
    g_mla = proj(wg_ref, d.d_model, 2 * d.d_model)

    u = gate_c * h
    cqn = _rms(cq, qg_ref[...]).astype(BF16)
    qa = jnp.dot(cqn, wuqa_ref[...], preferred_element_type=F32)
    qbc = jnp.dot(cqn, wuqb_ref[...], preferred_element_type=F32)
    r = wuqb_ref.shape[1] // d.n_heads
    pieces = []
    for hd in range(d.n_heads):
        blk, off = divmod(hd * r, LANES)
        piece = qbc[:, blk * LANES:(blk + 1) * LANES]
        pieces.append(pltpu.roll(piece, LANES - off, 1) if off else piece)
    qb = jnp.concatenate(pieces, axis=1)
    ct = jnp.concatenate([ctab] * d.n_heads, axis=1)
    st = jnp.concatenate([stab] * d.n_heads, axis=1)
    q = (qa * ct + qb * st) * scale
    ckvn = _rms(ckv, kvg_ref[...])
    krr = kra * ctab + krb * stab
    return u, gate_b, g_conv, g_mla, q, ckvn, krr, cqn


def _pre_prompt_kernel(dims, scale, tm, dv, dvp,
                       x_ref, ctab_ref, stab_ref, ctabt_ref, stabt_ref, gpre_ref, wa_ref, wkr_ref, wg_ref, convw_ref,
                       wco_ref, qg_ref, wuqa_ref, wuqb_ref, wuqat_ref, wuqbt_ref, kvg_ref, wuk_ref, wuvt_ref,
                       qt_out, k_out, vt_out, ckv_out, kr_out, mc_out, sg_out, conv_out, ubuf):
    hist = SUBLANES

    @pl.when(pl.program_id(1) == 0)
    def _():
        ubuf[0:hist, :] = jnp.zeros((hist, dims.conv_dim), F32)

    u, gate_b, g_conv, g_mla, _, ckvn, krr, cqn = _pre_common(
        dims, x_ref[...], ctab_ref[...], stab_ref[...], gpre_ref, wa_ref, wkr_ref, wg_ref, qg_ref, wuqa_ref,
        wuqb_ref, kvg_ref, scale)

    ubuf[hist:hist + tm, :] = u
    u1 = ubuf[hist - 1:hist - 1 + tm, :]
    u2 = ubuf[hist - 2:hist - 2 + tm, :]
    cw = convw_ref[...]
    conv = u2 * cw[0:1, :] + u1 * cw[1:2, :] + u * cw[2:3, :]
    y_conv = _bdot(gate_b * conv, wco_ref[...])
    mc_out[...] = (jax.nn.sigmoid(g_conv) * y_conv).astype(mc_out.dtype)
    sg_out[...] = jax.nn.sigmoid(g_mla).astype(sg_out.dtype)
    conv_out[...] = ubuf[hist + tm - (CONV_WIDTH - 1):hist + tm, :]
    ubuf[0:hist, :] = ubuf[tm:tm + hist, :]

    nt_dims = (((1,), (1,)), ((), ()))
    qat = lax.dot_general(wuqat_ref[...], cqn, nt_dims, preferred_element_type=F32)
    qbt_c = lax.dot_general(wuqbt_ref[...], cqn, nt_dims, preferred_element_type=F32)
    r = wuqbt_ref.shape[0] // dims.n_heads
    zrows = jnp.zeros((LANES - r, tm), F32)
    qbt = jnp.concatenate([piece for hd in range(dims.n_heads)
                           for piece in (qbt_c[hd * r:(hd + 1) * r, :], zrows)], axis=0)
    ctt = jnp.concatenate([ctabt_ref[...]] * dims.n_heads, axis=0)
    stt = jnp.concatenate([stabt_ref[...]] * dims.n_heads, axis=0)
    qt_out[...] = ((qat * ctt + qbt * stt) * scale).astype(qt_out.dtype)
    ckv_out[...] = ckvn
    kr_out[...] = krr[:, :kr_out.shape[-1]]
    ckvb = ckvn.astype(BF16)
    ka = jnp.dot(ckvb, wuk_ref[...], preferred_element_type=F32)
    k = ka + jnp.concatenate([krr] * dims.n_heads, axis=1)
    k_out[...] = k.astype(k_out.dtype)
    vt = lax.dot_general(wuvt_ref[...], ckvb, (((1,), (1,)), ((), ())), preferred_element_type=F32)
    rid = lax.broadcasted_iota(jnp.int32, (vt.shape[0], 1), 0)
    vt_out[...] = (vt + jnp.where(rid % dvp == dv, 1.0, 0.0)).astype(vt_out.dtype)


def _pre_prompt(dims, scale, x2d, ctab, stab, ctabt, stabt, w, batch, seq, rope_dim, dv, dvp):
    tm = ROW_TILE
    assert seq % tm == 0
    nt = seq // tm
    rows = batch * seq
    d = dims
    hv = w["w_uv_t"].shape[0]

    def row_spec(width):
        return pl.BlockSpec((tm, width), lambda b, t: (b * nt + t, 0))

    def tab_spec():
        return pl.BlockSpec((tm, LANES), lambda b, t: (t, 0))

    def tabt_spec():
        return pl.BlockSpec((LANES, tm), lambda b, t: (0, t))

    in_specs = [row_spec(d.d_model), tab_spec(), tab_spec(), tabt_spec(), tabt_spec(),
                _const_spec((1, d.d_model)), _const_spec((d.d_model, d.n_a)),
                _const_spec((d.d_model, 2 * LANES)), _const_spec((d.d_model, 2 * d.d_model)),
                _const_spec((CONV_WIDTH, d.conv_dim)), _const_spec((d.conv_dim, d.d_model)),
                _const_spec((1, d.q_rank)), _const_spec((d.q_rank, d.hw)),
                _const_spec((d.q_rank, d.n_heads * rope_dim)),
                _const_spec((d.hw, d.q_rank)), _const_spec((d.n_heads * rope_dim, d.q_rank)),
                _const_spec((1, d.kv_rank)), _const_spec((d.kv_rank, d.hw)), _const_spec((hv, d.kv_rank))]
    out_shape = (jax.ShapeDtypeStruct((batch, d.hw, seq), BF16),
                 jax.ShapeDtypeStruct((rows, d.hw), BF16),
                 jax.ShapeDtypeStruct((batch, hv, seq), BF16),
                 jax.ShapeDtypeStruct((rows, d.kv_rank), F32),
                 jax.ShapeDtypeStruct((rows, rope_dim), F32),
                 jax.ShapeDtypeStruct((rows, d.d_model), BF16),
                 jax.ShapeDtypeStruct((rows, d.d_model), BF16),
                 jax.ShapeDtypeStruct((batch, CONV_WIDTH - 1, d.conv_dim), F32))
    out_specs = (pl.BlockSpec((None, d.hw, tm), lambda b, t: (b, 0, t)), row_spec(d.hw),
                 pl.BlockSpec((None, hv, tm), lambda b, t: (b, 0, t)), row_spec(d.kv_rank),
                 row_spec(rope_dim), row_spec(d.d_model), row_spec(d.d_model),
                 pl.BlockSpec((None, CONV_WIDTH - 1, d.conv_dim), lambda b, t: (b, 0, 0)))
    kern = functools.partial(_pre_prompt_kernel, dims, scale, tm, dv, dvp)
    return pl.pallas_call(
        kern, grid=(batch, nt), in_specs=in_specs, out_specs=out_specs, out_shape=out_shape,
        scratch_shapes=[pltpu.VMEM((tm + SUBLANES, d.conv_dim), F32)],
        compiler_params=_params("arbitrary", "arbitrary"), name="pre_prompt",
    )(x2d, ctab, stab, ctabt, stabt, w["g_pre"], w["w_in_a"], w["w_in_kr"], w["w_in_g"], w["conv_w"],
      w["w_conv_out"], w["q_g"], w["w_uq_a"], w["w_uq_b"], w["w_uq_a"].T, w["w_uq_b"].T, w["kv_g"], w["w_uk_pad"],
      w["w_uv_t"])


def _pre_sample_kernel(dims, scale,
                       x_ref, state_ref, ctab_ref, stab_ref, gpre_ref, wa_ref, wkr_ref, wg_ref, convw_ref, wco_ref,
                       qg_ref, wuqa_ref, wuqb_ref, kvg_ref, wukt_ref,
                       q_out, qlat_out, ckv_out, kr_out, mc_out, sg_out, conv_out):
    c = dims.conv_dim
    u, gate_b, g_conv, g_mla, q, ckvn, krr, _ = _pre_common(
        dims, x_ref[...], ctab_ref[...], stab_ref[...], gpre_ref, wa_ref, wkr_ref, wg_ref, qg_ref, wuqa_ref,
        wuqb_ref, kvg_ref, scale)
    u2 = state_ref[:, 0:c]
    u1 = state_ref[:, c:2 * c]
    cw = convw_ref[...]
    conv = u2 * cw[0:1, :] + u1 * cw[1:2, :] + u * cw[2:3, :]
    y_conv = _bdot(gate_b * conv, wco_ref[...])
    mc_out[...] = jax.nn.sigmoid(g_conv) * y_conv
    sg_out[...] = jax.nn.sigmoid(g_mla)
    conv_out[:, 0:c] = u1
    conv_out[:, c:2 * c] = u
    q_out[...] = q
    ckv_out[...] = ckvn
    kr_out[...] = krr
    kvr = dims.kv_rank
    qb16 = q.astype(BF16)
    for hd in range(dims.n_heads):
        blk = qb16[:, hd * LANES:(hd + 1) * LANES]
        qlat_out[:, hd * kvr:(hd + 1) * kvr] = jnp.dot(
            blk, wukt_ref[hd * LANES:(hd + 1) * LANES, :], preferred_element_type=F32)


def _pre_sample(dims, scale, xs, state2d, ctab, stab, w):
    n = xs.shape[0]
    d = dims
    out_shape = (jax.ShapeDtypeStruct((n, d.hw), F32),
                 jax.ShapeDtypeStruct((n, d.n_heads * d.kv_rank), F32),
                 jax.ShapeDtypeStruct((n, d.kv_rank), F32),
                 jax.ShapeDtypeStruct((n, LANES), F32),
                 jax.ShapeDtypeStruct((n, d.d_model), F32),
                 jax.ShapeDtypeStruct((n, d.d_model), F32),
                 jax.ShapeDtypeStruct((n, (CONV_WIDTH - 1) * d.conv_dim), F32))
    kern = functools.partial(_pre_sample_kernel, dims, scale)
    return pl.pallas_call(kern, out_shape=out_shape, compiler_params=_params(), name="pre_sample")(
        xs, state2d, ctab, stab, w["g_pre"], w["w_in_a"], w["w_in_kr"], w["w_in_g"], w["conv_w"], w["w_conv_out"],
        w["q_g"],
        w["w_uq_a"], w["w_uq_b"], w["kv_g"], w["w_uk_t_pad"])


def _attn_kernel(tq, tk, hp, dv, dvp, nq, qt_ref, k_ref, vt_ref, o_ref, sa_ref, sb_ref):
    assert tq == 2 * tk
    half = tq // 2

    def tile(qi, carry):
        q0 = pl.multiple_of(qi * tq, tq)

        def score_group(h, j, dst, n, q_lo, qbase):
            kh = k_ref[pl.ds(pl.multiple_of(j * tk, tk), tk), h * LANES:(h + 1) * LANES]
            qth = qt_ref[h * LANES:(h + 1) * LANES,
                         pl.ds(pl.multiple_of(qbase + q_lo + n * MXU_WIDTH, MXU_WIDTH), MXU_WIDTH)]
            dst[h, :, n * MXU_WIDTH:(n + 1) * MXU_WIDTH] = jnp.dot(kh, qth, preferred_element_type=F32)

        def step(state, proc=None, sc=None):
            n_s = (tq - sc[2]) // MXU_WIDTH if sc else 0
            if proc is None:
                for h in range(hp):
                    for n in range(n_s):
                        score_group(h, sc[0], sc[1], n, sc[2], sc[3])
                return state
            j, src, masked, q_lo = proc
            start = pl.multiple_of(j * tk, tk)
            nq_p = tq - q_lo
            n_p = nq_p // MXU_WIDTH
            new_state = []
            for h in range(hp):
                m_prev, acc = state[h]
                st = src[h, :, 0:nq_p]
                if masked:
                    key = lax.broadcasted_iota(jnp.int32, (tk, nq_p), 0) + j * tk
                    qry = lax.broadcasted_iota(jnp.int32, (tk, nq_p), 1) + qi * tq + q_lo
                    st = jnp.where(key <= qry, st, NEG_INF)
                m_new = jnp.maximum(m_prev[:, q_lo:], jnp.max(st, axis=0, keepdims=True))
                alpha = jnp.exp2(m_prev[:, q_lo:] - m_new)
                p = jnp.exp2(st - m_new).astype(BF16)
                vth = vt_ref[h * dvp:(h + 1) * dvp, pl.ds(start, tk)]
                pieces = [acc[:, :q_lo]] if q_lo else []
                for n in range(max(n_p, n_s)):
                    if n < n_s:
                        score_group(h, sc[0], sc[1], n, sc[2], sc[3])
                    if n < n_p:
                        ql = slice(n * MXU_WIDTH, (n + 1) * MXU_WIDTH)
                        pieces.append(alpha[:, ql] * acc[:, q_lo + n * MXU_WIDTH:q_lo + (n + 1) * MXU_WIDTH]
                                      + jnp.dot(vth, p[:, ql], preferred_element_type=F32))
                if q_lo:
                    m_new = jnp.concatenate([m_prev[:, :q_lo], m_new], axis=1)
                new_state.append((m_new, jnp.concatenate(pieces, axis=1)))
            return tuple(new_state)

        init = tuple((jnp.full((1, tq), NEG_INF, F32), jnp.zeros((dvp, tq), F32)) for _ in range(hp))

        @pl.when(qi == 0)
        def _():
            step(None, sc=(0, sa_ref, 0, q0))

        def pair(t, state):
            state = step(state, proc=(2 * t, sa_ref, False, 0), sc=(2 * t + 1, sb_ref, 0, q0))
            return step(state, proc=(2 * t + 1, sb_ref, False, 0), sc=(2 * t + 2, sa_ref, 0, q0))

        state = lax.fori_loop(0, qi, pair, init)
        q_next = pl.multiple_of(jnp.minimum(qi + 1, nq - 1) * tq, tq)
        state = step(state, proc=(2 * qi, sa_ref, True, 0), sc=(2 * qi + 1, sb_ref, half, q0))
        state = step(state, proc=(2 * qi + 1, sb_ref, True, half), sc=(0, sa_ref, 0, q_next))
        ot = jnp.concatenate([acc[0:dv] / acc[dv:dv + 1] for (_, acc) in state], axis=0)
        o_ref[pl.ds(q0, tq), :] = ot.T.astype(o_ref.dtype)
        return carry

    lax.fori_loop(0, nq, tile, 0)


def _prompt_attention(qt, k, vt, batch, seq, n_heads, dv):
    tq, tk = ATTN_TQ, ATTN_TK
    assert seq % tq == 0 and tq % tk == 0
    hp = LANES // dv
    assert n_heads % hp == 0
    dvp = vt.shape[1] // n_heads
    nq = seq // tq
    kern = functools.partial(_attn_kernel, tq, tk, hp, dv, dvp, nq)
    return pl.pallas_call(
        kern, grid=(batch, n_heads // hp),
        in_specs=[pl.BlockSpec((None, hp * LANES, seq), lambda b, g: (b, g, 0)),
                  pl.BlockSpec((seq, hp * LANES), lambda b, g: (b, g)),
                  pl.BlockSpec((None, hp * dvp, seq), lambda b, g: (b, g, 0))],
        out_specs=pl.BlockSpec((seq, hp * dv), lambda b, g: (b, g)),
        out_shape=jax.ShapeDtypeStruct((batch * seq, n_heads * dv), BF16),
        scratch_shapes=[pltpu.VMEM((hp, tk, tq), F32), pltpu.VMEM((hp, tk, tq), F32)],
        compiler_params=_params("arbitrary", "arbitrary"), name="prompt_attn",
    )(qt, k, vt)


def _mix_and_query(x, o, mc, sg, wmla_ref, wmix_ref, gmixpost_ref, gcapre_ref, wcaq_ref, ca_scale):
    y_mla = _bdot(o, wmla_ref[...])
    merged = mc.astype(F32) + sg.astype(F32) * y_mla
    y = _bdot(merged, wmix_ref[...])
    x1 = x + _rms(y, gmixpost_ref[...])
    qc = _bdot(_rms(x1, gcapre_ref[...]), wcaq_ref[...]) * ca_scale
    return x1, qc


def _mlp_tail(x1, oc, wcao_ref, gcapost_ref, gmlppre_ref, wup_ref, wdown_ref, gmlppost_ref):
    ca = _bdot(oc, wcao_ref[...])
    x2 = x1 + _rms(ca, gcapost_ref[...])
    hid = jnp.square(jnp.maximum(_bdot(_rms(x2, gmlppre_ref[...]), wup_ref[...]), 0.0))
    return x2 + _rms(_bdot(hid, wdown_ref[...]), gmlppost_ref[...])


def _post_prompt_kernel(ca_heads, ca_scale,
                        x_ref, o_ref, mc_ref, sg_ref, mk_ref, mv_ref,
                        wmla_ref, wmix_ref, gmixpost_ref, gcapre_ref, wcaq_ref, wcao_ref, gcapost_ref,
                        gmlppre_ref, wup_ref, wdown_ref, gmlppost_ref, y_ref):
    x1, qc = _mix_and_query(x_ref[...], o_ref[...], mc_ref[...], sg_ref[...], wmla_ref, wmix_ref,
                            gmixpost_ref, gcapre_ref, wcaq_ref, ca_scale)
    hd = qc.shape[1] // ca_heads
    outs = []
    for h in range(ca_heads):
        sl = slice(h * hd, (h + 1) * hd)
        p = _softmax_rows(_bdot_nt(qc[:, sl], mk_ref[:, sl]))
        outs.append(_bdot(p, mv_ref[:, sl]))
    oc = jnp.concatenate(outs, axis=1)
    y_ref[...] = _mlp_tail(x1, oc, wcao_ref, gcapost_ref, gmlppre_ref, wup_ref, wdown_ref, gmlppost_ref)


def _post_prompt(x2d, o, mc, sg, mk, mv, w, batch, seq, ca_heads, n_mem):
    tm = ROW_TILE
    nt = seq // tm
    rows, dm = x2d.shape
    dff = w["w_ff_up"].shape[1]
    ca_scale = float((dm // ca_heads) ** -0.5)

    def row_spec(width):
        return pl.BlockSpec((tm, width), lambda i: (i, 0))

    mem_spec = pl.BlockSpec((n_mem, dm), lambda i: (i // nt, 0))
    in_specs = [row_spec(dm), row_spec(o.shape[1]), row_spec(dm), row_spec(dm), mem_spec, mem_spec,
                _const_spec((o.shape[1], dm)), _const_spec((dm, dm)), _const_spec((1, dm)),
                _const_spec((1, dm)), _const_spec((dm, dm)), _const_spec((dm, dm)), _const_spec((1, dm)),
                _const_spec((1, dm)), _const_spec((dm, dff)), _const_spec((dff, dm)), _const_spec((1, dm))]
    kern = functools.partial(_post_prompt_kernel, ca_heads, ca_scale)
    return pl.pallas_call(
        kern, grid=(rows // tm,), in_specs=in_specs, out_specs=row_spec(dm),
        out_shape=jax.ShapeDtypeStruct((rows, dm), F32),
        compiler_params=_params("arbitrary"), name="post_prompt",
    )(x2d, o, mc, sg, mk, mv, w["w_mla"], w["w_mix"], w["g_mix_post"], w["g_ca_pre"], w["w_ca_q"],
      w["w_ca_o"], w["g_ca_post"], w["g_mlp_pre"], w["w_ff_up"], w["w_ff_down"], w["g_mlp_post"])


def _post_sample_a_kernel(n_heads, v_dim, ca_scale,
                          x_ref, olat_ref, mc_ref, sg_ref, wuv_ref, wmla_ref, wmix_ref, gmixpost_ref,
                          gcapre_ref, wcaq_ref, x1_ref, qc_ref):
    n = x_ref.shape[0]
    full = _bdot(olat_ref[...], wuv_ref[...])
    full = full.reshape(n, n_heads, n_heads * v_dim)
    hidx = lax.broadcasted_iota(jnp.int32, full.shape, 1)
    lane_head = lax.broadcasted_iota(jnp.int32, full.shape, 2) // v_dim
    o = jnp.sum(jnp.where(hidx == lane_head, full, 0.0), axis=1)
    x1, qc = _mix_and_query(x_ref[...], o, mc_ref[...], sg_ref[...], wmla_ref, wmix_ref,
                            gmixpost_ref, gcapre_ref, wcaq_ref, ca_scale)
    x1_ref[...] = x1
    qc_ref[...] = qc


def _post_sample_b_kernel(ca_heads, q_ref, mk_ref, mv_ref, o_ref):
    g = q_ref.shape[0]
    rows = mk_ref.shape[1]
    s8 = 2 * ca_heads
    assert s8 == SUBLANES and mk_ref.shape[2] == LANES
    lane = lax.broadcasted_iota(jnp.int32, (s8, rows), 1)
    sub = lax.broadcasted_iota(jnp.int32, (s8, rows), 0)
    own = (lane % s8) == sub
    low_half = (lax.broadcasted_iota(jnp.int32, (1, rows), 1) % s8) < ca_heads
    for i in range(g):
        q = q_ref[i]
        q8 = jnp.concatenate([q[:, (2 * (j % ca_heads) + j // ca_heads) * LANES:
                                   (2 * (j % ca_heads) + j // ca_heads + 1) * LANES] for j in range(s8)], axis=0)
        kb = mk_ref[i].astype(BF16)
        p_all = lax.dot_general(q8.astype(BF16), kb, (((1,), (1,)), ((), ())), preferred_element_type=F32)
        d = jnp.sum(jnp.where(own, p_all, 0.0), axis=0, keepdims=True)
        d = d + jnp.where(low_half, pltpu.roll(d, rows - ca_heads, 1), pltpu.roll(d, ca_heads, 1))
        dm = jnp.where(own, jnp.broadcast_to(d, (s8, rows)), NEG_INF)
        e = jnp.exp(dm - jnp.max(dm, axis=-1, keepdims=True))
        w = (e / jnp.sum(e, axis=-1, keepdims=True)).astype(BF16)
        o8 = jnp.dot(w, mv_ref[i].astype(BF16), preferred_element_type=F32)
        for j in range(s8):
            blk = 2 * (j % ca_heads) + j // ca_heads
            o_ref[i, :, blk * LANES:(blk + 1) * LANES] = o8[j:j + 1, :]


def _post_sample_c_kernel(x1_ref, oc_ref, wcao_ref, gcapost_ref, gmlppre_ref, wup_ref, wdown_ref,
                          gmlppost_ref, y_ref):
    y_ref[...] = _mlp_tail(x1_ref[...], oc_ref[...], wcao_ref, gcapost_ref, gmlppre_ref, wup_ref,
                           wdown_ref, gmlppost_ref)


def _post_sample(xs, olat, mc, sg, mem_k, mem_v, w, n_heads, v_dim, ca_heads):
    n, dm = xs.shape
    ca_scale = float((dm // ca_heads) ** -0.5)
    kern_a = functools.partial(_post_sample_a_kernel, n_heads, v_dim, ca_scale)
    x1, qc = pl.pallas_call(
        kern_a, out_shape=(jax.ShapeDtypeStruct((n, dm), F32), jax.ShapeDtypeStruct((n, dm), F32)),
        compiler_params=_params(), name="post_sample_a",
    )(xs, olat, mc, sg, w["w_uv_flat"], w["w_mla"], w["w_mix"], w["g_mix_post"], w["g_ca_pre"], w["w_ca_q"])

    g = CA_SAMPLES_PER_STEP
    assert n % g == 0
    kern_b = functools.partial(_post_sample_b_kernel, ca_heads)
    mem_spec = pl.BlockSpec((g,) + mem_k.shape[1:], lambda i: (i, 0, 0))
    oc = pl.pallas_call(
        kern_b, grid=(n // g,),
        in_specs=[pl.BlockSpec((g, 1, dm), lambda i: (i, 0, 0)), mem_spec, mem_spec],
        out_specs=pl.BlockSpec((g, 1, dm), lambda i: (i, 0, 0)),
        out_shape=jax.ShapeDtypeStruct((n, 1, dm), F32),
        compiler_params=_params("arbitrary"), name="post_sample_b",
    )(qc.reshape(n, 1, dm), mem_k, mem_v)

    return pl.pallas_call(
        _post_sample_c_kernel, out_shape=jax.ShapeDtypeStruct((n, dm), F32),
        compiler_params=_params(), name="post_sample_c",
    )(x1, oc.reshape(n, dm), w["w_ca_o"], w["g_ca_post"], w["g_mlp_pre"], w["w_ff_up"], w["w_ff_down"],
      w["g_mlp_post"])


def _dec_attn_kernel(n_samples, n_pages, n_heads, rope_dim,
                     pt_ref, qlat_ref, qrope_ref, ckvn_ref, krn_ref, cache_ckv, cache_krt,
                     o_ref, cbuf, kbuf, sems):
    ch = DEC_PAGES_PER_STEP
    nch = n_pages // ch
    total = n_samples * nch
    page, kvr = cbuf.shape[2], cbuf.shape[3]

    def copies(g, slot):
        b = g // nch
        c = g % nch
        out = []
        for i in range(ch):
            pg = pt_ref[b, c * ch + i]
            out.append(pltpu.make_async_copy(cache_ckv.at[pg], cbuf.at[slot, i], sems.at[0, slot]))
            out.append(pltpu.make_async_copy(cache_krt.at[pg], kbuf.at[slot, i], sems.at[1, slot]))
        return out

    def start(g):
        for n, cp in enumerate(copies(g, g % DEC_NBUF)):
            cp.start(priority=(n // 2) % 2)

    def wait(g):
        for cp in copies(g, g % DEC_NBUF):
            cp.wait()

    def q_rows(b):
        row0 = pl.multiple_of(b * n_heads, n_heads)
        return qlat_ref[pl.ds(row0, n_heads), :], qrope_ref[pl.ds(row0, n_heads), :][:, :rope_dim]

    def scores(g):
        ql, qr = q_rows(g // nch)
        slot = g % DEC_NBUF
        ckv = cbuf[slot].reshape(ch * page, kvr).astype(BF16)
        qrb = qr.astype(BF16)
        s_rope = jnp.concatenate(
            [jnp.dot(qrb, kbuf[slot, i].astype(BF16), preferred_element_type=F32) for i in range(ch)], axis=1)
        return lax.dot_general(ql.astype(BF16), ckv, (((1,), (1,)), ((), ())),
                               preferred_element_type=F32) + s_rope

    for g0 in range(DEC_NBUF - 1):
        start(g0)
    wait(0)
    s0 = scores(0)

    def body(g, carry):
        s_cur, m, l, acc = carry
        b = g // nch
        c = g % nch

        @pl.when(g + 1 < total)
        def _():
            wait(g + 1)

        @pl.when(g + (DEC_NBUF - 1) < total)
        def _():
            start(g + (DEC_NBUF - 1))

        s_next = scores(jnp.minimum(g + 1, total - 1))

        ql, qr = q_rows(b)
        c_new = ckvn_ref[pl.ds(b, 1), :]
        r_new = krn_ref[pl.ds(b, 1), :][:, :rope_dim]
        s_new = jnp.sum(ql * c_new, axis=-1, keepdims=True) + jnp.sum(qr * r_new, axis=-1, keepdims=True)
        first = c == 0
        m = jnp.where(first, s_new, m)
        l = jnp.where(first, 1.0, l)
        acc = jnp.where(first, jnp.broadcast_to(c_new, acc.shape), acc)

        ckv = cbuf[g % DEC_NBUF].reshape(ch * page, kvr).astype(BF16)
        m_new = jnp.maximum(m, jnp.max(s_cur, axis=-1, keepdims=True))
        alpha = jnp.exp(m - m_new)
        p = jnp.exp(s_cur - m_new)
        l = alpha * l + jnp.sum(p, axis=-1, keepdims=True)
        acc = alpha * acc + jnp.dot(p.astype(BF16), ckv, preferred_element_type=F32)
        o_ref[pl.ds(pl.multiple_of(b * n_heads, n_heads), n_heads), :] = acc / l
        return s_next, m_new, l, acc

    init = (s0, jnp.zeros((n_heads, 1), F32), jnp.zeros((n_heads, 1), F32), jnp.zeros((n_heads, kvr), F32))
    lax.fori_loop(0, total, body, init)


def _decode_attention(page_table, qlat, qrope, ckv_new, kr_new, cache_ckv, cache_krt, n_heads, rope_dim):
    n_samples, n_pages = page_table.shape
    page, kvr = cache_ckv.shape[1], cache_ckv.shape[2]
    ch = DEC_PAGES_PER_STEP
    assert n_pages % ch == 0

    def whole(shape):
        nd = len(shape)
        return pl.BlockSpec(shape, lambda i, pt: (0,) * nd)

    kern = functools.partial(_dec_attn_kernel, n_samples, n_pages, n_heads, rope_dim)
    grid_spec = pltpu.PrefetchScalarGridSpec(
        num_scalar_prefetch=1, grid=(1,),
        in_specs=[whole(qlat.shape), whole(qrope.shape), whole(ckv_new.shape), whole(kr_new.shape),
                  pl.BlockSpec(memory_space=pl.ANY), pl.BlockSpec(memory_space=pl.ANY)],
        out_specs=whole(qlat.shape),
        scratch_shapes=[pltpu.VMEM((DEC_NBUF, ch, page, kvr), F32),
                        pltpu.VMEM((DEC_NBUF, ch, rope_dim, page), F32),
                        pltpu.SemaphoreType.DMA((2, DEC_NBUF))])
    return pl.pallas_call(
        kern, grid_spec=grid_spec, out_shape=jax.ShapeDtypeStruct(qlat.shape, F32),
        compiler_params=_params("arbitrary"), name="decode_attn",
    )(page_table, qlat, qrope, ckv_new, kr_new, cache_ckv, cache_krt)


def _rot_half(w):
    half = w.shape[-1] // 2
    return jnp.concatenate([-w[..., half:], w[..., :half]], axis=-1)


def _mem_rows(cache):
    _, n, n_mem, heads, hd = cache.shape
    assert hd == 2 * LANES
    x = cache.reshape(n, n_mem, heads, 2, LANES)
    return jnp.transpose(x, (0, 1, 3, 2, 4)).reshape(n, n_mem * 2 * heads, LANES)


def _rope_lane_freqs(rope_dim):
    inv = 1.0 / (ROPE_BASE ** (jnp.arange(0, rope_dim, 2, dtype=F32) / rope_dim))
    return jnp.concatenate([inv, inv, jnp.zeros((LANES - rope_dim,), F32)])


def _rope_tabs(pos, rope_dim, nope_dim):
    ang = pos.astype(F32)[:, None] * _rope_lane_freqs(rope_dim)[None, :]
    keep = (jnp.arange(LANES) < rope_dim + nope_dim).astype(F32)
    return jnp.cos(ang) * keep, jnp.sin(ang)


def _rope_tabs_range(seq, rope_dim, nope_dim, transposed=False):
    assert seq % LANES == 0
    freqs = _rope_lane_freqs(rope_dim)
    keep = (jnp.arange(LANES) < rope_dim + nope_dim).astype(F32)
    hi = jnp.arange(seq // LANES, dtype=F32) * LANES
    lo = jnp.arange(LANES, dtype=F32)
    if transposed:
        freqs, keep = freqs[:, None, None], keep[:, None, None]
        a, b = hi[None, :, None] * freqs, lo[None, None, :] * freqs
        shape = (LANES, seq)
    else:
        a, b = hi[:, None, None] * freqs, lo[None, :, None] * freqs
        shape = (seq, LANES)
    ca, sa, cb, sb = jnp.cos(a), jnp.sin(a), jnp.cos(b), jnp.sin(b)
    ctab = (ca * cb - sa * sb) * keep
    stab = sa * cb + ca * sb
    return ctab.reshape(shape), stab.reshape(shape)


def kernel(x_prompt, x_sample, mem_prompt, cache_ckv, cache_krope, state_conv, cache_mem_k, cache_mem_v,
           page_table, norm_mix_pre_g, w_in, conv_w, w_conv_out, q_norm_g, w_uq, kv_norm_g, w_uk, w_uv,
           w_mla_out, w_mix_out, norm_mix_post_g, norm_ca_pre_g, mem_norm_g, w_ca_q, w_ca_k, w_ca_v, w_ca_o,
           norm_ca_post_g, norm_mlp_pre_g, w_ff_up, w_ff_down, norm_mlp_post_g):
    depth = w_in.shape[0]
    assert depth == 1, "single-layer step"
    batch, seq, dm = x_prompt.shape
    n_s, t_s, _ = x_sample.shape
    assert t_s == 1
    conv_dim = conv_w.shape[2]
    q_rank, n_heads, qk_dim = w_uq.shape[1:]
    kv_rank, _, nope_dim = w_uk.shape[1:]
    v_dim = w_uv.shape[3]
    rope_dim = qk_dim - nope_dim
    n_mem, ca_heads, ca_hd = cache_mem_k.shape[2:]
    n_pool, page = cache_ckv.shape[1:3]
    past_len = page_table.shape[1] * page
    assert rope_dim + nope_dim <= LANES and v_dim <= LANES
    dims = _PreDims(dm, conv_dim, q_rank, kv_rank, n_heads)
    scale = float(qk_dim ** -0.5)

    w0 = w_in[0]
    o_kr = 3 * conv_dim + q_rank + kv_rank
    w_kr = w0[:, o_kr:o_kr + rope_dim]
    zpad = jnp.zeros((dm, LANES - rope_dim), F32)
    w_in_kr = jnp.concatenate([w_kr, zpad, _rot_half(w_kr), zpad], axis=1).astype(BF16)
    uq = w_uq[0]
    uq_nope, uq_rope = uq[:, :, :nope_dim], uq[:, :, nope_dim:]
    hz = lambda r, width: jnp.zeros((r, n_heads, width), F32)
    pad_tail = LANES - rope_dim - nope_dim
    w_uq_a = jnp.concatenate([uq_rope, uq_nope, hz(q_rank, pad_tail)], axis=2).reshape(q_rank, n_heads * LANES)
    assert LANES % rope_dim == 0
    w_uq_b = _rot_half(uq_rope).reshape(q_rank, n_heads * rope_dim)
    uk = w_uk[0]
    w_uk_pad = jnp.concatenate([hz(kv_rank, rope_dim), uk, hz(kv_rank, pad_tail)], axis=2).reshape(
        kv_rank, n_heads * LANES)
    w_uk_t_pad = jnp.transpose(w_uk_pad.reshape(kv_rank, n_heads, LANES), (1, 2, 0)).reshape(
        n_heads * LANES, kv_rank)
    uv = w_uv[0].reshape(kv_rank, n_heads * v_dim)
    dvp = v_dim + ATTN_V_EXTRA_ROWS
    w_uv_t = jnp.concatenate([jnp.transpose(w_uv[0], (1, 2, 0)),
                              jnp.zeros((n_heads, ATTN_V_EXTRA_ROWS, kv_rank), F32)], axis=1).reshape(
        n_heads * dvp, kv_rank)
    w_mla = w_mla_out[0]
    w = {
        "g_pre": norm_mix_pre_g, "w_in_a": w0[:, :o_kr].astype(BF16), "w_in_kr": w_in_kr,
        "w_in_g": w0[:, o_kr + rope_dim:].astype(BF16), "conv_w": conv_w[0], "w_conv_out": w_conv_out[0].astype(BF16),
        "q_g": q_norm_g, "w_uq_a": w_uq_a.astype(BF16), "w_uq_b": w_uq_b.astype(BF16), "kv_g": kv_norm_g,
        "w_uk_pad": w_uk_pad.astype(BF16), "w_uk_t_pad": w_uk_t_pad.astype(BF16),
        "w_uv_t": w_uv_t.astype(BF16), "w_uv_flat": uv.astype(BF16),
        "w_mla": w_mla.astype(BF16), "w_mix": w_mix_out[0].astype(BF16),
        "g_mix_post": norm_mix_post_g, "g_ca_pre": norm_ca_pre_g,
        "w_ca_q": w_ca_q[0].reshape(dm, ca_heads * ca_hd).astype(BF16),
        "w_ca_o": w_ca_o[0].reshape(ca_heads * ca_hd, dm).astype(BF16),
        "g_ca_post": norm_ca_post_g, "g_mlp_pre": norm_mlp_pre_g, "w_ff_up": w_ff_up[0].astype(BF16),
        "w_ff_down": w_ff_down[0].astype(BF16), "g_mlp_post": norm_mlp_post_g,
    }

    mk_p, mv_p = _memory_kv(mem_prompt.reshape(batch * n_mem, dm), mem_norm_g,
                            w_ca_k[0].reshape(dm, ca_heads * ca_hd).astype(BF16),
                            w_ca_v[0].reshape(dm, ca_heads * ca_hd).astype(BF16))
    ctab_p, stab_p = _rope_tabs_range(seq, rope_dim, nope_dim)
    ctabt_p, stabt_p = _rope_tabs_range(seq, rope_dim, nope_dim, transposed=True)
    x2d = x_prompt.reshape(batch * seq, dm)
    qt, k, vt, ckv_p, kr_p, mc_p, sg_p, conv_p = _pre_prompt(dims, scale * LOG2_E, x2d, ctab_p, stab_p, ctabt_p,
                                                            stabt_p, w, batch, seq, rope_dim, v_dim, dvp)
    o_p = _prompt_attention(qt, k, vt, batch, seq, n_heads, v_dim)
    y_p = _post_prompt(x2d, o_p, mc_p, sg_p, mk_p, mv_p, w, batch, seq, ca_heads, n_mem)

    ctab_s, stab_s = _rope_tabs(jnp.full((n_s,), past_len, jnp.int32), rope_dim, nope_dim)
    xs = x_sample.reshape(n_s, dm)
    q_s, qlat_s, ckv_s, kr_s, mc_s, sg_s, conv_s = _pre_sample(
        dims, scale, xs, state_conv.reshape(n_s, (CONV_WIDTH - 1) * conv_dim), ctab_s, stab_s, w)
    olat = _decode_attention(page_table, qlat_s.reshape(n_s * n_heads, kv_rank),
                             q_s.reshape(n_s * n_heads, LANES), ckv_s, kr_s,
                             cache_ckv.reshape(n_pool, page, kv_rank),
                             jnp.swapaxes(cache_krope.reshape(n_pool, page, rope_dim), 1, 2), n_heads, rope_dim)
    y_s = _post_sample(xs, olat, mc_s, sg_s, _mem_rows(cache_mem_k), _mem_rows(cache_mem_v), w, n_heads, v_dim,
                       ca_heads)

    return (y_p.reshape(batch, seq, dm),
            y_s.reshape(n_s, t_s, dm),
            ckv_p.reshape(depth, batch, seq, kv_rank),
            kr_p.reshape(depth, batch, seq, rope_dim),
            conv_p.reshape(depth, batch, CONV_WIDTH - 1, conv_dim),
            mk_p.reshape(depth, batch, n_mem, ca_heads, ca_hd),
            mv_p.reshape(depth, batch, n_mem, ca_heads, ca_hd),
            ckv_s.reshape(depth, n_s, t_s, kv_rank),
            kr_s[:, :rope_dim].reshape(depth, n_s, t_s, rope_dim),
            conv_s.reshape(depth, n_s, CONV_WIDTH - 1, conv_dim))
```
